```python
import math
import numpy as np
import jax
import jax.numpy as jnp
from jax import lax

D_MODEL = 1024
BATCH = 4
SEQ = 4096
DEPTH = 2
DEC_BATCH = 128
DEC_SEQ = 4
PAST_LEN = 2048
PAGE_SIZE = 128

D_MIX = D_MODEL
HEAD_DIM = 64
A_WIDTH = D_MIX // 4
A_HEADS = A_WIDTH // HEAD_DIM
A_DECAY_LORA = 64
A_ICL_LORA = 64
A_GATE_LORA = 128
A_IN = 3 * A_WIDTH + A_DECAY_LORA + A_ICL_LORA + A_GATE_LORA
GN_EPS = HEAD_DIM * 1e-5
B_WIDTH = D_MIX // 2
B_HEADS = B_WIDTH // HEAD_DIM
B_GROUPS = 2
D_STATE = 128
CONV_W = 4
SSD_CHUNK = 128
B_CONV_DIM = B_WIDTH + 2 * B_GROUPS * D_STATE
B_IN = B_WIDTH + B_CONV_DIM + B_HEADS
C_WIDTH = D_MIX - A_WIDTH - B_WIDTH
C_HEADS = C_WIDTH // HEAD_DIM
C_IN = 3 * C_WIDTH + C_HEADS
Q_BLOCK = 128
IN_WIDTH = A_IN + B_IN + C_IN
D_FF = 2816
N_EXPERTS = 8
TOP_K = 2
D_FF_EXPERT = 3584
N_DENSE = (DEPTH + 1) // 2
N_MOE = DEPTH // 2
NORM_EPS = 1e-6

kernel_name = 'hymba_rwkv7_ssd_fox_adaln_decode_step'


def _split(t, sizes):
    return jnp.split(t, np.cumsum(sizes)[:-1].tolist(), axis=-1)


def rms_norm(x, w):
    xf = x.astype(jnp.float32)
    y = xf * lax.rsqrt(jnp.mean(xf * xf, axis=-1, keepdims=True) + NORM_EPS)
    return (y * w.astype(jnp.float32)).astype(x.dtype)


def head_rms(t, w):
    tf = t.astype(jnp.float32)
    return (tf * lax.rsqrt(jnp.mean(tf * tf, axis=-1, keepdims=True) + NORM_EPS) * w).astype(t.dtype)


def ada_modulation(c, w_ada, b_ada):
    m = jax.nn.silu(c) @ w_ada + b_ada
    return jnp.split(m[:, None, :], 6, axis=-1)


def swiglu(t, w_gate, w_up, w_down):
    return (jax.nn.silu(t @ w_gate) * (t @ w_up)) @ w_down


def moe_swiglu(u, router_w, router_b, w_gate, w_up, w_down):
    n, L, d = u.shape
    t = u.reshape(n * L, d)
    logits = (t @ router_w).astype(jnp.float32) + router_b.astype(jnp.float32)
    top_val, top_idx = lax.top_k(logits, TOP_K)
    weights = jax.nn.softmax(top_val, axis=-1)
    gates = jnp.sum(jax.nn.one_hot(top_idx, N_EXPERTS, dtype=jnp.float32) * weights[..., None], axis=1)
    out = jnp.zeros((n * L, d), jnp.float32)
    for e in range(N_EXPERTS):
        out = out + gates[:, e:e + 1] * swiglu(t, w_gate[e], w_up[e], w_down[e])
    return out.astype(u.dtype).reshape(n, L, d)


def rwkv7_recurrence(r, decay, k, v, kk, a, s0):
    def step(s, inp):
        r_t, w_t, k_t, v_t, kk_t, a_t = inp
        s_kk = jnp.einsum('nhvk,nhk->nhv', s, kk_t)
        s = (s * w_t[:, :, None, :]
             - s_kk[..., None] * (kk_t * a_t)[:, :, None, :]
             + v_t[..., None] * k_t[:, :, None, :])
        return s, jnp.einsum('nhvk,nhk->nhv', s, r_t)
    seq_first = tuple(jnp.moveaxis(t, 1, 0) for t in (r, decay, k, v, kk, a))
    s_last, ys = lax.scan(step, s0, seq_first)
    return jnp.moveaxis(ys, 0, 1), s_last


def rwkv7_mixer(cols, shift0, s0, mu, w0, w2, a0, a2, g2, k_k, k_a, r_k, ln_w, ln_b):
    n, L, _ = cols.shape
    f32 = jnp.float32
    prev = jnp.concatenate([shift0[:, None, :].astype(cols.dtype), cols[:, :-1]], axis=1)
    xs = cols + (prev - cols) * mu
    r, k, v, xw, xa, xg = _split(xs, [A_WIDTH, A_WIDTH, A_WIDTH, A_DECAY_LORA, A_ICL_LORA, A_GATE_LORA])
    w = -jax.nn.softplus(-(w0 + jnp.tanh(xw) @ w2)) - 0.5
    decay = jnp.exp(-jnp.exp(w.astype(f32)))
    a = jax.nn.sigmoid(a0 + xa @ a2)
    g = jax.nn.sigmoid(xg) @ g2
    heads = lambda t: t.astype(f32).reshape(n, L, A_HEADS, HEAD_DIM)
    kk = heads(k * k_k)
    kk = kk / jnp.maximum(jnp.sqrt(jnp.sum(kk * kk, axis=-1, keepdims=True)), 1e-12)
    k = k * (1.0 + (a - 1.0) * k_a)
    r_h, k_h, v_h = heads(r), heads(k), heads(v)
    y, s_last = rwkv7_recurrence(r_h, heads(decay), k_h, v_h, kk, heads(a), s0.astype(f32))
    mean = jnp.mean(y, axis=-1, keepdims=True)
    var = jnp.mean(jnp.square(y - mean), axis=-1, keepdims=True)
    y = ((y - mean) * lax.rsqrt(var + GN_EPS)).reshape(n, L, A_WIDTH) * ln_w + ln_b
    bonus = jnp.sum(r_h * k_h * r_k, axis=-1, keepdims=True) * v_h
    y = (y + bonus.reshape(n, L, A_WIDTH)) * g
    return y.astype(cols.dtype), s_last, cols[:, -1]


def ssd_chunked(x, dt, A, B, C, h0):
    N, L, H, P = x.shape
    q = min(SSD_CHUNK, L)
    pad = (-L) % q
    if pad:
        padl = lambda t: jnp.pad(t, [(0, 0), (0, pad)] + [(0, 0)] * (t.ndim - 2))
        x, dt, B, C = padl(x), padl(dt), padl(B), padl(C)
    nc = (L + pad) // q
    ch = lambda t: t.reshape((N, nc, q) + t.shape[2:])
    xc, dtc, Bc, Cc = ch(x), ch(dt), ch(B), ch(C)
    cs = jnp.cumsum(dtc * A, axis=2)
    causal = jnp.tril(jnp.ones((q, q), bool))[None, None, :, :, None]
    seg = cs[:, :, :, None, :] - cs[:, :, None, :, :]
    decay = jnp.exp(jnp.where(causal, seg, -jnp.inf))
    scores = jnp.einsum('nclhd,ncshd->nclsh', Cc, Bc) * decay * dtc[:, :, None, :, :]
    y_intra = jnp.einsum('nclsh,ncshp->nclhp', scores, xc)
    to_end = jnp.exp(cs[:, :, -1:, :] - cs) * dtc
    chunk_states = jnp.einsum('ncsh,ncshd,ncshp->nchpd', to_end, Bc, xc)
    chunk_decay = jnp.exp(cs[:, :, -1, :])
    def carry(h, inp):
        dec, st = inp
        return h * dec[:, :, None, None] + st, h
    h_last, h_in = lax.scan(carry, h0, (jnp.moveaxis(chunk_decay, 1, 0), jnp.moveaxis(chunk_states, 1, 0)))
    h_in = jnp.moveaxis(h_in, 0, 1)
    y_inter = jnp.einsum('nclhd,nchpd->nclhp', Cc, h_in) * jnp.exp(cs)[..., None]
    y = (y_intra + y_inter).reshape(N, nc * q, H, P)[:, :L]
    return y, h_last


def mamba2_mixer(z, xbc, dt_raw, conv0, ssm0, conv_w, conv_b, dt_bias, a_log, d_skip, norm_w):
    n, L, _ = xbc.shape
    f32 = jnp.float32
    xpad = jnp.concatenate([conv0.astype(xbc.dtype), xbc], axis=1)
    conv = lax.conv_general_dilated(xpad, conv_w[:, None, :], window_strides=(1,), padding='VALID',
                                    dimension_numbers=('NWC', 'WIO', 'NWC'),
                                    feature_group_count=B_CONV_DIM) + conv_b
    xs, Bm, Cm = _split(jax.nn.silu(conv), [B_WIDTH, B_GROUPS * D_STATE, B_GROUPS * D_STATE])
    xh = xs.astype(f32).reshape(n, L, B_HEADS, HEAD_DIM)
    rep = B_HEADS // B_GROUPS
    Bh = jnp.repeat(Bm.astype(f32).reshape(n, L, B_GROUPS, D_STATE), rep, axis=2)
    Ch = jnp.repeat(Cm.astype(f32).reshape(n, L, B_GROUPS, D_STATE), rep, axis=2)
    dt = jax.nn.softplus(dt_raw.astype(f32) + dt_bias)
    A = -jnp.exp(a_log.astype(f32))
    y, h_last = ssd_chunked(xh, dt, A, Bh, Ch, ssm0.astype(f32))
    y = (y + d_skip[:, None] * xh).reshape(n, L, B_WIDTH) * jax.nn.silu(z.astype(f32))
    yg = y.reshape(n, L, B_GROUPS, B_WIDTH // B_GROUPS)
    yg = yg * lax.rsqrt(jnp.mean(yg * yg, axis=-1, keepdims=True) + NORM_EPS)
    y = yg.reshape(n, L, B_WIDTH) * norm_w
    return y.astype(z.dtype), h_last, xpad[:, -(CONV_W - 1):]


def fox_attention(q, k, v, logf, past):
    n, L, H, hd = q.shape
    f32 = jnp.float32
    cum = jnp.cumsum(logf, axis=1)
    if past is None:
        P = 0
        keys_k, keys_v, key_c = k, v, cum
    else:
        pool_k, pool_v, pool_logf, page_table = past
        P = page_table.shape[1] * PAGE_SIZE
        gather = lambda pool: pool[page_table].reshape((n, P) + pool.shape[2:])
        plf = gather(pool_logf).astype(f32)
        suffix = jnp.cumsum(plf[:, ::-1], axis=1)[:, ::-1] - plf
        keys_k = jnp.concatenate([gather(pool_k).astype(k.dtype), k], axis=1)
        keys_v = jnp.concatenate([gather(pool_v).astype(v.dtype), v], axis=1)
        key_c = jnp.concatenate([-suffix, cum], axis=1)
    kc = jnp.moveaxis(key_c, 2, 1)
    kpos = jnp.arange(P + L)
    qb = Q_BLOCK if L % Q_BLOCK == 0 else L
    nb = L // qb
    scale = hd ** -0.5

    def block(args):
        q_blk, c_blk, t0 = args
        s = jnp.einsum('nqhd,nkhd->nhqk', q_blk, keys_k, preferred_element_type=f32) * scale
        s = s + jnp.moveaxis(c_blk, 2, 1)[..., None] - kc[:, :, None, :]
        mask = kpos[None, :] <= (t0 + jnp.arange(qb))[:, None]
        prob = jax.nn.softmax(jnp.where(mask, s, -jnp.inf), axis=-1)
        return jnp.einsum('nhqk,nkhd->nqhd', prob.astype(keys_v.dtype), keys_v)

    q_blocks = jnp.moveaxis(q.reshape(n, nb, qb, H, hd), 1, 0)
    c_blocks = jnp.moveaxis(cum.reshape(n, nb, qb, H), 1, 0)
    starts = P + jnp.arange(nb) * qb
    out = lax.map(block, (q_blocks, c_blocks, starts))
    return jnp.moveaxis(out, 0, 1).reshape(n, L, H, hd)


def fox_mixer(q, k, v, f_raw, f_bias, qn_w, kn_w, past):
    n, L, _ = q.shape
    qh = head_rms(q.reshape(n, L, C_HEADS, HEAD_DIM), qn_w)
    kh = head_rms(k.reshape(n, L, C_HEADS, HEAD_DIM), kn_w)
    vh = v.reshape(n, L, C_HEADS, HEAD_DIM)
    logf = jax.nn.log_sigmoid(f_raw.astype(jnp.float32) + f_bias)
    o = fox_attention(qh, kh, vh, logf, past)
    return o.reshape(n, L, C_WIDTH).astype(q.dtype), kh, vh, logf


def decoder_layer(x, c, l, p, s0, shift0, ssm0, conv0, past):
    sh1, sc1, g1, sh2, sc2, g2 = ada_modulation(c, p['w_ada'][l], p['b_ada'][l])
    u = rms_norm(x, p['norm1_w'][l]) * (1 + sc1) + sh1
    proj = u @ p['w_in'][l]
    a_cols, b_z, b_xbc, b_dt, c_q, c_k, c_v, c_f = _split(
        proj, [A_IN, B_WIDTH, B_CONV_DIM, B_HEADS, C_WIDTH, C_WIDTH, C_WIDTH, C_HEADS])
    y_a, s_new, shift_new = rwkv7_mixer(
        a_cols, shift0, s0, p['a_mu'][l], p['a_w0'][l], p['a_w2'][l], p['a_a0'][l], p['a_a2'][l],
        p['a_g2'][l], p['a_kk'][l], p['a_ka'][l], p['a_rk'][l], p['a_ln_w'][l], p['a_ln_b'][l])
    y_b, ssm_new, conv_new = mamba2_mixer(
        b_z, b_xbc, b_dt, conv0, ssm0, p['b_conv_w'][l], p['b_conv_b'][l], p['b_dt_bias'][l],
        p['b_a_log'][l], p['b_d'][l], p['b_norm_w'][l])
    fox_past = None if past is None else (past[0][l], past[1][l], past[2][l], past[3])
    y_c, k_rows, v_rows, logf_rows = fox_mixer(
        c_q, c_k, c_v, c_f, p['c_f_bias'][l], p['c_qnorm_w'][l], p['c_knorm_w'][l], fox_past)
    x = x + g1 * (jnp.concatenate([y_a, y_b, y_c], axis=-1) @ p['w_out'][l])
    u2 = rms_norm(x, p['norm2_w'][l]) * (1 + sc2) + sh2
    if l % 2 == 0:
        j = l // 2
        f = swiglu(u2, p['ffn_w_gate'][j], p['ffn_w_up'][j], p['ffn_w_down'][j])
    else:
        j = l // 2
        f = moe_swiglu(u2, p['moe_router_w'][j], p['moe_router_b'][j], p['moe_w_gate'][j],
                       p['moe_w_up'][j], p['moe_w_down'][j])
    x = x + g2 * f
    dt = x.dtype
    return x, (s_new.astype(dt), shift_new.astype(dt), ssm_new.astype(dt), conv_new.astype(dt),
               k_rows.astype(dt), v_rows.astype(dt), logf_rows.astype(dt))


def run_trunk(x, c, p, init_states, past):
    outs = []
    for l in range(DEPTH):
        x, st = decoder_layer(x, c, l, p, *init_states[l], past)
        outs.append(st)
    stacked = [jnp.stack([o[i] for o in outs]) for i in range(7)]
    return x, stacked


def setup_inputs(seed: int = 0) -> dict:
    key = jax.random.key(seed)
    keys = iter(jax.random.split(key, 64))
    f32 = jnp.float32
    def normal(shape, scale=1.0):
        return jax.random.normal(next(keys), shape, f32) * scale
    def uniform(shape, lo, hi):
        return jax.random.uniform(next(keys), shape, f32, lo, hi)
    def gain(shape):
        return 1.0 + normal(shape, 0.02)
    n_pages = PAST_LEN // PAGE_SIZE
    n_used = DEC_BATCH * n_pages
    n_phys = n_used + max(1, n_used // 4)
    page_table = jax.random.permutation(next(keys), n_phys)[:n_used].reshape(DEC_BATCH, n_pages).astype(jnp.int32)
    dt0 = jnp.exp(uniform((DEPTH, B_HEADS), math.log(1e-3), math.log(1e-1)))
    return {
        'x_prompt': normal((BATCH, SEQ, D_MODEL)),
        'x_sample': normal((DEC_BATCH, DEC_SEQ, D_MODEL)),
        'cache_k': normal((DEPTH, n_phys, PAGE_SIZE, C_HEADS, HEAD_DIM)),
        'cache_v': normal((DEPTH, n_phys, PAGE_SIZE, C_HEADS, HEAD_DIM)),
        'cache_logf': jax.nn.log_sigmoid(7.0 + normal((DEPTH, n_phys, PAGE_SIZE, C_HEADS), 0.5)),
        'state_rwkv': normal((DEPTH, DEC_BATCH, A_HEADS, HEAD_DIM, HEAD_DIM), 0.3),
        'state_shift': normal((DEPTH, DEC_BATCH, A_IN)),
        'state_ssm': normal((DEPTH, DEC_BATCH, B_HEADS, HEAD_DIM, D_STATE), 0.3),
        'state_conv': normal((DEPTH, DEC_BATCH, CONV_W - 1, B_CONV_DIM)),
        'page_table': page_table,
        'c_prompt': normal((BATCH, D_MODEL)),
        'c_sample': normal((DEC_BATCH, D_MODEL)),
        'norm1_w': gain((DEPTH, D_MODEL)),
        'norm2_w': gain((DEPTH, D_MODEL)),
        'w_ada': normal((DEPTH, D_MODEL, 6 * D_MODEL), 0.5 * D_MODEL ** -0.5),
        'b_ada': normal((DEPTH, 6 * D_MODEL), 0.02),
        'w_in': normal((DEPTH, D_MODEL, IN_WIDTH), D_MODEL ** -0.5),
        'w_out': normal((DEPTH, D_MIX, D_MODEL), D_MIX ** -0.5),
        'a_mu': uniform((DEPTH, A_IN), 0.0, 1.0),
        'a_w0': uniform((DEPTH, A_WIDTH), -6.0, -1.0),
        'a_w2': normal((DEPTH, A_DECAY_LORA, A_WIDTH), 0.5 * A_DECAY_LORA ** -0.5),
        'a_a0': normal((DEPTH, A_WIDTH), 0.1),
        'a_a2': normal((DEPTH, A_ICL_LORA, A_WIDTH), 0.5 * A_ICL_LORA ** -0.5),
        'a_g2': normal((DEPTH, A_GATE_LORA, A_WIDTH), A_GATE_LORA ** -0.5),
        'a_kk': 0.85 + normal((DEPTH, A_WIDTH), 0.02),
        'a_ka': gain((DEPTH, A_WIDTH)),
        'a_rk': normal((DEPTH, A_HEADS, HEAD_DIM), 0.1),
        'a_ln_w': gain((DEPTH, A_WIDTH)),
        'a_ln_b': normal((DEPTH, A_WIDTH), 0.02),
        'b_conv_w': normal((DEPTH, CONV_W, B_CONV_DIM), CONV_W ** -0.5),
        'b_conv_b': normal((DEPTH, B_CONV_DIM), 0.02),
        'b_dt_bias': dt0 + jnp.log(-jnp.expm1(-dt0)),
        'b_a_log': jnp.log(uniform((DEPTH, B_HEADS), 1.0, 16.0)),
        'b_d': gain((DEPTH, B_HEADS)),
        'b_norm_w': gain((DEPTH, B_WIDTH)),
        'c_f_bias': uniform((DEPTH, C_HEADS), 4.0, 7.0),
        'c_qnorm_w': gain((DEPTH, HEAD_DIM)),
        'c_knorm_w': gain((DEPTH, HEAD_DIM)),
        'ffn_w_gate': normal((N_DENSE, D_MODEL, D_FF), D_MODEL ** -0.5),
        'ffn_w_up': normal((N_DENSE, D_MODEL, D_FF), D_MODEL ** -0.5),
        'ffn_w_down': normal((N_DENSE, D_FF, D_MODEL), D_FF ** -0.5),
        'moe_router_w': normal((N_MOE, D_MODEL, N_EXPERTS), D_MODEL ** -0.5),
        'moe_router_b': normal((N_MOE, N_EXPERTS), 0.01),
        'moe_w_gate': normal((N_MOE, N_EXPERTS, D_MODEL, D_FF_EXPERT), D_MODEL ** -0.5),
        'moe_w_up': normal((N_MOE, N_EXPERTS, D_MODEL, D_FF_EXPERT), D_MODEL ** -0.5),
        'moe_w_down': normal((N_MOE, N_EXPERTS, D_FF_EXPERT, D_MODEL), D_FF_EXPERT ** -0.5),
    }


def reference(x_prompt, x_sample, cache_k, cache_v, cache_logf, state_rwkv, state_shift, state_ssm,
              state_conv, page_table, c_prompt, c_sample, norm1_w, norm2_w, w_ada, b_ada, w_in, w_out,
              a_mu, a_w0, a_w2, a_a0, a_a2, a_g2, a_kk, a_ka, a_rk, a_ln_w, a_ln_b,
              b_conv_w, b_conv_b, b_dt_bias, b_a_log, b_d, b_norm_w,
              c_f_bias, c_qnorm_w, c_knorm_w, ffn_w_gate, ffn_w_up, ffn_w_down,
              moe_router_w, moe_router_b, moe_w_gate, moe_w_up, moe_w_down):
    p = dict(norm1_w=norm1_w, norm2_w=norm2_w, w_ada=w_ada, b_ada=b_ada, w_in=w_in, w_out=w_out,
             a_mu=a_mu, a_w0=a_w0, a_w2=a_w2, a_a0=a_a0, a_a2=a_a2, a_g2=a_g2, a_kk=a_kk, a_ka=a_ka,
             a_rk=a_rk, a_ln_w=a_ln_w, a_ln_b=a_ln_b, b_conv_w=b_conv_w, b_conv_b=b_conv_b,
             b_dt_bias=b_dt_bias, b_a_log=b_a_log, b_d=b_d, b_norm_w=b_norm_w, c_f_bias=c_f_bias,
             c_qnorm_w=c_qnorm_w, c_knorm_w=c_knorm_w, ffn_w_gate=ffn_w_gate, ffn_w_up=ffn_w_up,
             ffn_w_down=ffn_w_down, moe_router_w=moe_router_w, moe_router_b=moe_router_b,
             moe_w_gate=moe_w_gate, moe_w_up=moe_w_up, moe_w_down=moe_w_down)
    n_p = x_prompt.shape[0]
    zero_state = (jnp.zeros((n_p, A_HEADS, HEAD_DIM, HEAD_DIM), jnp.float32),
                  jnp.zeros((n_p, A_IN), x_prompt.dtype),
                  jnp.zeros((n_p, B_HEADS, HEAD_DIM, D_STATE), jnp.float32),
                  jnp.zeros((n_p, CONV_W - 1, B_CONV_DIM), x_prompt.dtype))
    y_prompt, st_p = run_trunk(x_prompt, c_prompt, p, [zero_state] * DEPTH, None)
    rwkv_prompt, shift_prompt, ssm_prompt, conv_prompt, k_prompt, v_prompt, logf_prompt = st_p
    init_s = [(state_rwkv[l], state_shift[l], state_ssm[l], state_conv[l]) for l in range(DEPTH)]
    y_sample, st_s = run_trunk(x_sample, c_sample, p, init_s, (cache_k, cache_v, cache_logf, page_table))
    rwkv_sample, shift_sample, ssm_sample, conv_sample, k_sample, v_sample, logf_sample = st_s
    return (y_prompt, y_sample,
            k_prompt, v_prompt, logf_prompt, rwkv_prompt, shift_prompt, ssm_prompt, conv_prompt,
            k_sample, v_sample, logf_sample, rwkv_sample, shift_sample, ssm_sample, conv_sample)
```

```python
import math
import numpy as np
import jax
import jax.numpy as jnp
from jax import lax
from jax.experimental import pallas as pl
from jax.experimental.pallas import tpu as pltpu

D_MODEL = 1024
BATCH = 4
SEQ = 4096
DEPTH = 2
DEC_BATCH = 128
DEC_SEQ = 4
PAST_LEN = 2048
PAGE_SIZE = 128

D_MIX = D_MODEL
HEAD_DIM = 64
A_WIDTH = D_MIX // 4
A_HEADS = A_WIDTH // HEAD_DIM
A_DECAY_LORA = 64
A_ICL_LORA = 64
A_GATE_LORA = 128
A_IN = 3 * A_WIDTH + A_DECAY_LORA + A_ICL_LORA + A_GATE_LORA
GN_EPS = HEAD_DIM * 1e-5
B_WIDTH = D_MIX // 2
B_HEADS = B_WIDTH // HEAD_DIM
B_GROUPS = 2
D_STATE = 128
CONV_W = 4
SSD_CHUNK = 128
B_CONV_DIM = B_WIDTH + 2 * B_GROUPS * D_STATE
B_IN = B_WIDTH + B_CONV_DIM + B_HEADS
C_WIDTH = D_MIX - A_WIDTH - B_WIDTH
C_HEADS = C_WIDTH // HEAD_DIM
C_IN = 3 * C_WIDTH + C_HEADS
Q_BLOCK = 128
IN_WIDTH = A_IN + B_IN + C_IN
D_FF = 2816
N_EXPERTS = 8
TOP_K = 2
D_FF_EXPERT = 3584
N_DENSE = (DEPTH + 1) // 2
N_MOE = DEPTH // 2
NORM_EPS = 1e-6


def _gated_residual_kernel(x_ref, g_ref, f_ref, o_ref):
    o_ref[...] = x_ref[...] + g_ref[...] * f_ref[...]


def gated_residual(x, g, f):
    n, L, d = x.shape
    tl = min(L, 512)
    return pl.pallas_call(
        _gated_residual_kernel,
        grid=(n, L // tl),
        in_specs=[pl.BlockSpec((1, tl, d), lambda i, j: (i, j, 0)),
                  pl.BlockSpec((1, 1, d), lambda i, j: (i, 0, 0)),
                  pl.BlockSpec((1, tl, d), lambda i, j: (i, j, 0))],
        out_specs=pl.BlockSpec((1, tl, d), lambda i, j: (i, j, 0)),
        out_shape=jax.ShapeDtypeStruct(x.shape, x.dtype),
    )(x, g, f)


def _split(t, sizes):
    return jnp.split(t, np.cumsum(sizes)[:-1].tolist(), axis=-1)


def rms_norm(x, w):
    xf = x.astype(jnp.float32)
    y = xf * lax.rsqrt(jnp.mean(xf * xf, axis=-1, keepdims=True) + NORM_EPS)
    return (y * w.astype(jnp.float32)).astype(x.dtype)


def head_rms(t, w):
    tf = t.astype(jnp.float32)
    return (tf * lax.rsqrt(jnp.mean(tf * tf, axis=-1, keepdims=True) + NORM_EPS) * w).astype(t.dtype)


def ada_modulation(c, w_ada, b_ada):
    m = jax.nn.silu(c) @ w_ada + b_ada
    return jnp.split(m[:, None, :], 6, axis=-1)


def swiglu(t, w_gate, w_up, w_down):
    return (jax.nn.silu(t @ w_gate) * (t @ w_up)) @ w_down


def moe_swiglu(u, router_w, router_b, w_gate, w_up, w_down):
    n, L, d = u.shape
    t = u.reshape(n * L, d)
    logits = (t @ router_w).astype(jnp.float32) + router_b.astype(jnp.float32)
    top_val, top_idx = lax.top_k(logits, TOP_K)
    weights = jax.nn.softmax(top_val, axis=-1)
    gates = jnp.sum(jax.nn.one_hot(top_idx, N_EXPERTS, dtype=jnp.float32) * weights[..., None], axis=1)
    out = jnp.zeros((n * L, d), jnp.float32)
    for e in range(N_EXPERTS):
        out = out + gates[:, e:e + 1] * swiglu(t, w_gate[e], w_up[e], w_down[e])
    return out.astype(u.dtype).reshape(n, L, d)


def rwkv7_recurrence(r, decay, k, v, kk, a, s0):
    def step(s, inp):
        r_t, w_t, k_t, v_t, kk_t, a_t = inp
        s_kk = jnp.einsum('nhvk,nhk->nhv', s, kk_t)
        s = (s * w_t[:, :, None, :]
             - s_kk[..., None] * (kk_t * a_t)[:, :, None, :]
             + v_t[..., None] * k_t[:, :, None, :])
        return s, jnp.einsum('nhvk,nhk->nhv', s, r_t)
    seq_first = tuple(jnp.moveaxis(t, 1, 0) for t in (r, decay, k, v, kk, a))
    s_last, ys = lax.scan(step, s0, seq_first)
    return jnp.moveaxis(ys, 0, 1), s_last


def rwkv7_mixer(cols, shift0, s0, mu, w0, w2, a0, a2, g2, k_k, k_a, r_k, ln_w, ln_b):
    n, L, _ = cols.shape
    f32 = jnp.float32
    prev = jnp.concatenate([shift0[:, None, :].astype(cols.dtype), cols[:, :-1]], axis=1)
    xs = cols + (prev - cols) * mu
    r, k, v, xw, xa, xg = _split(xs, [A_WIDTH, A_WIDTH, A_WIDTH, A_DECAY_LORA, A_ICL_LORA, A_GATE_LORA])
    w = -jax.nn.softplus(-(w0 + jnp.tanh(xw) @ w2)) - 0.5
    decay = jnp.exp(-jnp.exp(w.astype(f32)))
    a = jax.nn.sigmoid(a0 + xa @ a2)
    g = jax.nn.sigmoid(xg) @ g2
    heads = lambda t: t.astype(f32).reshape(n, L, A_HEADS, HEAD_DIM)
    kk = heads(k * k_k)
    kk = kk / jnp.maximum(jnp.sqrt(jnp.sum(kk * kk, axis=-1, keepdims=True)), 1e-12)
    k = k * (1.0 + (a - 1.0) * k_a)
    r_h, k_h, v_h = heads(r), heads(k), heads(v)
    y, s_last = rwkv7_recurrence(r_h, heads(decay), k_h, v_h, kk, heads(a), s0.astype(f32))
    mean = jnp.mean(y, axis=-1, keepdims=True)
    var = jnp.mean(jnp.square(y - mean), axis=-1, keepdims=True)
    y = ((y - mean) * lax.rsqrt(var + GN_EPS)).reshape(n, L, A_WIDTH) * ln_w + ln_b
    bonus = jnp.sum(r_h * k_h * r_k, axis=-1, keepdims=True) * v_h
    y = (y + bonus.reshape(n, L, A_WIDTH)) * g
    return y.astype(cols.dtype), s_last, cols[:, -1]


def ssd_chunked(x, dt, A, B, C, h0):
    N, L, H, P = x.shape
    q = min(SSD_CHUNK, L)
    pad = (-L) % q
    if pad:
        padl = lambda t: jnp.pad(t, [(0, 0), (0, pad)] + [(0, 0)] * (t.ndim - 2))
        x, dt, B, C = padl(x), padl(dt), padl(B), padl(C)
    nc = (L + pad) // q
    ch = lambda t: t.reshape((N, nc, q) + t.shape[2:])
    xc, dtc, Bc, Cc = ch(x), ch(dt), ch(B), ch(C)
    cs = jnp.cumsum(dtc * A, axis=2)
    causal = jnp.tril(jnp.ones((q, q), bool))[None, None, :, :, None]
    seg = cs[:, :, :, None, :] - cs[:, :, None, :, :]
    decay = jnp.exp(jnp.where(causal, seg, -jnp.inf))
    scores = jnp.einsum('nclhd,ncshd->nclsh', Cc, Bc) * decay * dtc[:, :, None, :, :]
    y_intra = jnp.einsum('nclsh,ncshp->nclhp', scores, xc)
    to_end = jnp.exp(cs[:, :, -1:, :] - cs) * dtc
    chunk_states = jnp.einsum('ncsh,ncshd,ncshp->nchpd', to_end, Bc, xc)
    chunk_decay = jnp.exp(cs[:, :, -1, :])
    def carry(h, inp):
        dec, st = inp
        return h * dec[:, :, None, None] + st, h
    h_last, h_in = lax.scan(carry, h0, (jnp.moveaxis(chunk_decay, 1, 0), jnp.moveaxis(chunk_states, 1, 0)))
    h_in = jnp.moveaxis(h_in, 0, 1)
    y_inter = jnp.einsum('nclhd,nchpd->nclhp', Cc, h_in) * jnp.exp(cs)[..., None]
    y = (y_intra + y_inter).reshape(N, nc * q, H, P)[:, :L]
    return y, h_last


def mamba2_mixer(z, xbc, dt_raw, conv0, ssm0, conv_w, conv_b, dt_bias, a_log, d_skip, norm_w):
    n, L, _ = xbc.shape
    f32 = jnp.float32
    xpad = jnp.concatenate([conv0.astype(xbc.dtype), xbc], axis=1)
    conv = lax.conv_general_dilated(xpad, conv_w[:, None, :], window_strides=(1,), padding='VALID',
                                    dimension_numbers=('NWC', 'WIO', 'NWC'),
                                    feature_group_count=B_CONV_DIM) + conv_b
    xs, Bm, Cm = _split(jax.nn.silu(conv), [B_WIDTH, B_GROUPS * D_STATE, B_GROUPS * D_STATE])
    xh = xs.astype(f32).reshape(n, L, B_HEADS, HEAD_DIM)
    rep = B_HEADS // B_GROUPS
    Bh = jnp.repeat(Bm.astype(f32).reshape(n, L, B_GROUPS, D_STATE), rep, axis=2)
    Ch = jnp.repeat(Cm.astype(f32).reshape(n, L, B_GROUPS, D_STATE), rep, axis=2)
    dt = jax.nn.softplus(dt_raw.astype(f32) + dt_bias)
    A = -jnp.exp(a_log.astype(f32))
    y, h_last = ssd_chunked(xh, dt, A, Bh, Ch, ssm0.astype(f32))
    y = (y + d_skip[:, None] * xh).reshape(n, L, B_WIDTH) * jax.nn.silu(z.astype(f32))
    yg = y.reshape(n, L, B_GROUPS, B_WIDTH // B_GROUPS)
    yg = yg * lax.rsqrt(jnp.mean(yg * yg, axis=-1, keepdims=True) + NORM_EPS)
    y = yg.reshape(n, L, B_WIDTH) * norm_w
    return y.astype(z.dtype), h_last, xpad[:, -(CONV_W - 1):]


def fox_attention(q, k, v, logf, past):
    n, L, H, hd = q.shape
    f32 = jnp.float32
    cum = jnp.cumsum(logf, axis=1)
    if past is None:
        P = 0
        keys_k, keys_v, key_c = k, v, cum
    else:
        pool_k, pool_v, pool_logf, page_table = past
        P = page_table.shape[1] * PAGE_SIZE
        gather = lambda pool: pool[page_table].reshape((n, P) + pool.shape[2:])
        plf = gather(pool_logf).astype(f32)
        suffix = jnp.cumsum(plf[:, ::-1], axis=1)[:, ::-1] - plf
        keys_k = jnp.concatenate([gather(pool_k).astype(k.dtype), k], axis=1)
        keys_v = jnp.concatenate([gather(pool_v).astype(v.dtype), v], axis=1)
        key_c = jnp.concatenate([-suffix, cum], axis=1)
    kc = jnp.moveaxis(key_c, 2, 1)
    kpos = jnp.arange(P + L)
    qb = Q_BLOCK if L % Q_BLOCK == 0 else L
    nb = L // qb
    scale = hd ** -0.5

    def block(args):
        q_blk, c_blk, t0 = args
        s = jnp.einsum('nqhd,nkhd->nhqk', q_blk, keys_k, preferred_element_type=f32) * scale
        s = s + jnp.moveaxis(c_blk, 2, 1)[..., None] - kc[:, :, None, :]
        mask = kpos[None, :] <= (t0 + jnp.arange(qb))[:, None]
        prob = jax.nn.softmax(jnp.where(mask, s, -jnp.inf), axis=-1)
        return jnp.einsum('nhqk,nkhd->nqhd', prob.astype(keys_v.dtype), keys_v)

    q_blocks = jnp.moveaxis(q.reshape(n, nb, qb, H, hd), 1, 0)
    c_blocks = jnp.moveaxis(cum.reshape(n, nb, qb, H), 1, 0)
    starts = P + jnp.arange(nb) * qb
    out = lax.map(block, (q_blocks, c_blocks, starts))
    return jnp.moveaxis(out, 0, 1).reshape(n, L, H, hd)


def fox_mixer(q, k, v, f_raw, f_bias, qn_w, kn_w, past):
    n, L, _ = q.shape
    qh = head_rms(q.reshape(n, L, C_HEADS, HEAD_DIM), qn_w)
    kh = head_rms(k.reshape(n, L, C_HEADS, HEAD_DIM), kn_w)
    vh = v.reshape(n, L, C_HEADS, HEAD_DIM)
    logf = jax.nn.log_sigmoid(f_raw.astype(jnp.float32) + f_bias)
    o = fox_attention(qh, kh, vh, logf, past)
    return o.reshape(n, L, C_WIDTH).astype(q.dtype), kh, vh, logf


def decoder_layer(x, c, l, p, s0, shift0, ssm0, conv0, past):
    sh1, sc1, g1, sh2, sc2, g2 = ada_modulation(c, p['w_ada'][l], p['b_ada'][l])
    u = rms_norm(x, p['norm1_w'][l]) * (1 + sc1) + sh1
    proj = u @ p['w_in'][l]
    a_cols, b_z, b_xbc, b_dt, c_q, c_k, c_v, c_f = _split(
        proj, [A_IN, B_WIDTH, B_CONV_DIM, B_HEADS, C_WIDTH, C_WIDTH, C_WIDTH, C_HEADS])
    y_a, s_new, shift_new = rwkv7_mixer(
        a_cols, shift0, s0, p['a_mu'][l], p['a_w0'][l], p['a_w2'][l], p['a_a0'][l], p['a_a2'][l],
        p['a_g2'][l], p['a_kk'][l], p['a_ka'][l], p['a_rk'][l], p['a_ln_w'][l], p['a_ln_b'][l])
    y_b, ssm_new, conv_new = mamba2_mixer(
        b_z, b_xbc, b_dt, conv0, ssm0, p['b_conv_w'][l], p['b_conv_b'][l], p['b_dt_bias'][l],
        p['b_a_log'][l], p['b_d'][l], p['b_norm_w'][l])
    fox_past = None if past is None else (past[0][l], past[1][l], past[2][l], past[3])
    y_c, k_rows, v_rows, logf_rows = fox_mixer(
        c_q, c_k, c_v, c_f, p['c_f_bias'][l], p['c_qnorm_w'][l], p['c_knorm_w'][l], fox_past)
    x = x + g1 * (jnp.concatenate([y_a, y_b, y_c], axis=-1) @ p['w_out'][l])
    u2 = rms_norm(x, p['norm2_w'][l]) * (1 + sc2) + sh2
    if l % 2 == 0:
        j = l // 2
        f = swiglu(u2, p['ffn_w_gate'][j], p['ffn_w_up'][j], p['ffn_w_down'][j])
    else:
        j = l // 2
        f = moe_swiglu(u2, p['moe_router_w'][j], p['moe_router_b'][j], p['moe_w_gate'][j],
                       p['moe_w_up'][j], p['moe_w_down'][j])
    x = gated_residual(x, g2, f)
    dt = x.dtype
    return x, (s_new.astype(dt), shift_new.astype(dt), ssm_new.astype(dt), conv_new.astype(dt),
               k_rows.astype(dt), v_rows.astype(dt), logf_rows.astype(dt))


def run_trunk(x, c, p, init_states, past):
    outs = []
    for l in range(DEPTH):
        x, st = decoder_layer(x, c, l, p, *init_states[l], past)
        outs.append(st)
    stacked = [jnp.stack([o[i] for o in outs]) for i in range(7)]
    return x, stacked


def kernel(x_prompt, x_sample, cache_k, cache_v, cache_logf, state_rwkv, state_shift, state_ssm,
           state_conv, page_table, c_prompt, c_sample, norm1_w, norm2_w, w_ada, b_ada, w_in, w_out,
           a_mu, a_w0, a_w2, a_a0, a_a2, a_g2, a_kk, a_ka, a_rk, a_ln_w, a_ln_b,
           b_conv_w, b_conv_b, b_dt_bias, b_a_log, b_d, b_norm_w,
           c_f_bias, c_qnorm_w, c_knorm_w, ffn_w_gate, ffn_w_up, ffn_w_down,
           moe_router_w, moe_router_b, moe_w_gate, moe_w_up, moe_w_down):
    p = dict(norm1_w=norm1_w, norm2_w=norm2_w, w_ada=w_ada, b_ada=b_ada, w_in=w_in, w_out=w_out,
             a_mu=a_mu, a_w0=a_w0, a_w2=a_w2, a_a0=a_a0, a_a2=a_a2, a_g2=a_g2, a_kk=a_kk, a_ka=a_ka,
             a_rk=a_rk, a_ln_w=a_ln_w, a_ln_b=a_ln_b, b_conv_w=b_conv_w, b_conv_b=b_conv_b,
             b_dt_bias=b_dt_bias, b_a_log=b_a_log, b_d=b_d, b_norm_w=b_norm_w, c_f_bias=c_f_bias,
             c_qnorm_w=c_qnorm_w, c_knorm_w=c_knorm_w, ffn_w_gate=ffn_w_gate, ffn_w_up=ffn_w_up,
             ffn_w_down=ffn_w_down, moe_router_w=moe_router_w, moe_router_b=moe_router_b,
             moe_w_gate=moe_w_gate, moe_w_up=moe_w_up, moe_w_down=moe_w_down)
    n_p = x_prompt.shape[0]
    zero_state = (jnp.zeros((n_p, A_HEADS, HEAD_DIM, HEAD_DIM), jnp.float32),
                  jnp.zeros((n_p, A_IN), x_prompt.dtype),
                  jnp.zeros((n_p, B_HEADS, HEAD_DIM, D_STATE), jnp.float32),
                  jnp.zeros((n_p, CONV_W - 1, B_CONV_DIM), x_prompt.dtype))
    y_prompt, st_p = run_trunk(x_prompt, c_prompt, p, [zero_state] * DEPTH, None)
    rwkv_prompt, shift_prompt, ssm_prompt, conv_prompt, k_prompt, v_prompt, logf_prompt = st_p
    init_s = [(state_rwkv[l], state_shift[l], state_ssm[l], state_conv[l]) for l in range(DEPTH)]
    y_sample, st_s = run_trunk(x_sample, c_sample, p, init_s, (cache_k, cache_v, cache_logf, page_table))
    rwkv_sample, shift_sample, ssm_sample, conv_sample, k_sample, v_sample, logf_sample = st_s
    return (y_prompt, y_sample,
            k_prompt, v_prompt, logf_prompt, rwkv_prompt, shift_prompt, ssm_prompt, conv_prompt,
            k_sample, v_sample, logf_sample, rwkv_sample, shift_sample, ssm_sample, conv_sample)
```

```python
import math
import numpy as np
import jax
import jax.numpy as jnp
from jax import lax
from jax.experimental import pallas as pl
from jax.experimental.pallas import tpu as pltpu

D_MODEL = 1024
BATCH = 4
SEQ = 4096
DEPTH = 2
DEC_BATCH = 128
DEC_SEQ = 4
PAST_LEN = 2048
PAGE_SIZE = 128

D_MIX = D_MODEL
HEAD_DIM = 64
A_WIDTH = D_MIX // 4
A_HEADS = A_WIDTH // HEAD_DIM
A_DECAY_LORA = 64
A_ICL_LORA = 64
A_GATE_LORA = 128
A_IN = 3 * A_WIDTH + A_DECAY_LORA + A_ICL_LORA + A_GATE_LORA
GN_EPS = HEAD_DIM * 1e-5
B_WIDTH = D_MIX // 2
B_HEADS = B_WIDTH // HEAD_DIM
B_GROUPS = 2
D_STATE = 128
CONV_W = 4
SSD_CHUNK = 128
B_CONV_DIM = B_WIDTH + 2 * B_GROUPS * D_STATE
B_IN = B_WIDTH + B_CONV_DIM + B_HEADS
C_WIDTH = D_MIX - A_WIDTH - B_WIDTH
C_HEADS = C_WIDTH // HEAD_DIM
C_IN = 3 * C_WIDTH + C_HEADS
Q_BLOCK = 128
IN_WIDTH = A_IN + B_IN + C_IN
D_FF = 2816
N_EXPERTS = 8
TOP_K = 2
D_FF_EXPERT = 3584
N_DENSE = (DEPTH + 1) // 2
N_MOE = DEPTH // 2
NORM_EPS = 1e-6


def _gated_residual_kernel(x_ref, g_ref, f_ref, o_ref):
    o_ref[...] = x_ref[...] + g_ref[...] * f_ref[...]


def gated_residual(x, g, f):
    n, L, d = x.shape
    tl = min(L, 512)
    return pl.pallas_call(
        _gated_residual_kernel,
        grid=(n, L // tl),
        in_specs=[pl.BlockSpec((1, tl, d), lambda i, j: (i, j, 0)),
                  pl.BlockSpec((1, 1, d), lambda i, j: (i, 0, 0)),
                  pl.BlockSpec((1, tl, d), lambda i, j: (i, j, 0))],
        out_specs=pl.BlockSpec((1, tl, d), lambda i, j: (i, j, 0)),
        out_shape=jax.ShapeDtypeStruct(x.shape, x.dtype),
    )(x, g, f)


def _split(t, sizes):
    return jnp.split(t, np.cumsum(sizes)[:-1].tolist(), axis=-1)


def rms_norm(x, w):
    xf = x.astype(jnp.float32)
    y = xf * lax.rsqrt(jnp.mean(xf * xf, axis=-1, keepdims=True) + NORM_EPS)
    return (y * w.astype(jnp.float32)).astype(x.dtype)


def head_rms(t, w):
    tf = t.astype(jnp.float32)
    return (tf * lax.rsqrt(jnp.mean(tf * tf, axis=-1, keepdims=True) + NORM_EPS) * w).astype(t.dtype)


def ada_modulation(c, w_ada, b_ada):
    m = jax.nn.silu(c) @ w_ada + b_ada
    return jnp.split(m[:, None, :], 6, axis=-1)


def swiglu(t, w_gate, w_up, w_down):
    return (jax.nn.silu(t @ w_gate) * (t @ w_up)) @ w_down


def moe_swiglu(u, router_w, router_b, w_gate, w_up, w_down):
    n, L, d = u.shape
    t = u.reshape(n * L, d)
    logits = (t @ router_w).astype(jnp.float32) + router_b.astype(jnp.float32)
    top_val, top_idx = lax.top_k(logits, TOP_K)
    weights = jax.nn.softmax(top_val, axis=-1)
    gates = jnp.sum(jax.nn.one_hot(top_idx, N_EXPERTS, dtype=jnp.float32) * weights[..., None], axis=1)
    out = jnp.zeros((n * L, d), jnp.float32)
    for e in range(N_EXPERTS):
        out = out + gates[:, e:e + 1] * swiglu(t, w_gate[e], w_up[e], w_down[e])
    return out.astype(u.dtype).reshape(n, L, d)


RWKV_CHUNK = 64
RWKV_BLOCK = 128
RWKV_INV_BLOCK = 16

_NN = (((1,), (0,)), ((), ()))
_NT = (((1,), (1,)), ((), ()))


def _dot(a, b, dims):
    return lax.dot_general(a, b, dims, preferred_element_type=jnp.float32)


def _split_bf16(x):
    hi = x.astype(jnp.bfloat16)
    lo = (x - hi.astype(jnp.float32)).astype(jnp.bfloat16)
    return hi, lo


def _mm3(a, b, dims=_NN):
    ah, al = _split_bf16(a)
    bh, bl = _split_bf16(b)
    return _dot(ah, bh, dims) + _dot(ah, bl, dims) + _dot(al, bh, dims)


def _mm_exact_lhs(a_bf16, x):
    h1, r1 = x.astype(jnp.bfloat16), None
    r1 = x - h1.astype(jnp.float32)
    h2 = r1.astype(jnp.bfloat16)
    h3 = (r1 - h2.astype(jnp.float32)).astype(jnp.bfloat16)
    return _dot(a_bf16, h1, _NN) + _dot(a_bf16, h2, _NN) + _dot(a_bf16, h3, _NN)


def _unit_lower_inverse(a, blk_mask, eye):
    ad = jnp.where(blk_mask, a, 0.0)
    e = a - ad
    p2 = _mm3(ad, ad)
    p4 = _mm3(p2, p2)
    p8 = _mm3(p4, p4)
    x = _mm3(eye - ad, eye + p2)
    x = _mm3(x, eye + p4)
    dinv = _mm3(x, eye + p8)
    n = _mm3(dinv, e)
    n2 = _mm3(n, n)
    return _mm3(_mm3(eye - n, eye + n2), dinv)


def _rwkv_subchunk(S, r, lw, k, v, vt, kk, a, consts):
    ltri, strict, incl, blk, eye = consts
    T = r.shape[0]
    c = _mm_exact_lhs(ltri, lw)
    cp = c - lw
    c_last = c[T - 1:T, :]
    e_c = jnp.exp(c)
    e_nc = jnp.exp(-c)
    e_end = jnp.exp(c_last - c)
    b = kk * a
    kt = kk * jnp.exp(cp)
    rt = r * e_c
    kh = k * e_nc
    bh = b * e_nc
    kbar = k * e_end
    bbar = b * e_end
    a_kb = jnp.where(strict, _mm3(kt, bh, _NT), 0.0)
    a_kk = jnp.where(strict, _mm3(kt, kh, _NT), 0.0)
    a_rk = jnp.where(incl, _mm3(rt, kh, _NT), 0.0)
    a_rb = jnp.where(incl, _mm3(rt, bh, _NT), 0.0)
    m = _unit_lower_inverse(a_kb, blk, eye)
    akkv = _mm3(a_kk, v)
    mk = _mm3(m, kt)
    g1 = _mm3(m, akkv)
    u = _mm3(mk, S, _NT) + g1
    y = _mm3(rt, S, _NT) + _mm3(a_rk, v) - _mm3(a_rb, u)
    g1t = _mm3(_mm3(vt, a_kk, _NT), m, _NT)
    ut = _mm3(S, mk, _NT) + g1t
    s_new = S * jnp.exp(c_last) + _mm3(vt, kbar) - _mm3(ut, bbar)
    return y, s_new


def _rwkv_kernel(r_ref, lw_ref, k_ref, v_ref, vt_ref, kk_ref, a_ref, s0_ref, y_ref, sT_ref, s_scr):
    H = r_ref.shape[1]
    T = RWKV_CHUNK

    @pl.when(pl.program_id(1) == 0)
    def _():
        s_scr[...] = s0_ref[0]

    row = lax.broadcasted_iota(jnp.int32, (T, T), 0)
    col = lax.broadcasted_iota(jnp.int32, (T, T), 1)
    consts = ((row >= col).astype(jnp.bfloat16), row > col, row >= col,
              (row // RWKV_INV_BLOCK) == (col // RWKV_INV_BLOCK), (row == col).astype(jnp.float32))
    for h in range(H):
        S = s_scr[h]
        for j in range(RWKV_BLOCK // T):
            sl = slice(j * T, (j + 1) * T)
            y, S = _rwkv_subchunk(S, r_ref[0, h, sl, :], lw_ref[0, h, sl, :], k_ref[0, h, sl, :],
                                  v_ref[0, h, sl, :], vt_ref[0, h, :, sl], kk_ref[0, h, sl, :],
                                  a_ref[0, h, sl, :], consts)
            y_ref[0, h, sl, :] = y
        s_scr[h] = S
    sT_ref[0] = s_scr[...]


def rwkv7_recurrence_pallas(r, lw, k, v, kk, a, s0):
    N, L, H, D = r.shape
    hm = lambda t: jnp.transpose(t, (0, 2, 1, 3))
    vt = jnp.transpose(v, (0, 2, 3, 1))
    row_spec = pl.BlockSpec((1, H, RWKV_BLOCK, D), lambda n, c: (n, 0, c, 0))
    st_spec = pl.BlockSpec((1, H, D, D), lambda n, c: (n, 0, 0, 0))
    y, s_last = pl.pallas_call(
        _rwkv_kernel,
        grid=(N, L // RWKV_BLOCK),
        in_specs=[row_spec, row_spec, row_spec, row_spec,
                  pl.BlockSpec((1, H, D, RWKV_BLOCK), lambda n, c: (n, 0, 0, c)),
                  row_spec, row_spec, st_spec],
        out_specs=[row_spec, st_spec],
        out_shape=[jax.ShapeDtypeStruct((N, H, L, D), jnp.float32),
                   jax.ShapeDtypeStruct((N, H, D, D), jnp.float32)],
        scratch_shapes=[pltpu.VMEM((H, D, D), jnp.float32)],
        compiler_params=pltpu.CompilerParams(dimension_semantics=("parallel", "arbitrary")),
        name="rwkv7_chunked",
    )(hm(r), hm(lw), hm(k), hm(v), vt, hm(kk), hm(a), s0)
    return jnp.transpose(y, (0, 2, 1, 3)), s_last


def rwkv7_recurrence(r, decay, k, v, kk, a, s0):
    def step(s, inp):
        r_t, w_t, k_t, v_t, kk_t, a_t = inp
        s_kk = jnp.einsum('nhvk,nhk->nhv', s, kk_t)
        s = (s * w_t[:, :, None, :]
             - s_kk[..., None] * (kk_t * a_t)[:, :, None, :]
             + v_t[..., None] * k_t[:, :, None, :])
        return s, jnp.einsum('nhvk,nhk->nhv', s, r_t)
    seq_first = tuple(jnp.moveaxis(t, 1, 0) for t in (r, decay, k, v, kk, a))
    s_last, ys = lax.scan(step, s0, seq_first)
    return jnp.moveaxis(ys, 0, 1), s_last


def rwkv7_mixer(cols, shift0, s0, mu, w0, w2, a0, a2, g2, k_k, k_a, r_k, ln_w, ln_b):
    n, L, _ = cols.shape
    f32 = jnp.float32
    prev = jnp.concatenate([shift0[:, None, :].astype(cols.dtype), cols[:, :-1]], axis=1)
    xs = cols + (prev - cols) * mu
    r, k, v, xw, xa, xg = _split(xs, [A_WIDTH, A_WIDTH, A_WIDTH, A_DECAY_LORA, A_ICL_LORA, A_GATE_LORA])
    w = -jax.nn.softplus(-(w0 + jnp.tanh(xw) @ w2)) - 0.5
    decay = jnp.exp(-jnp.exp(w.astype(f32)))
    a = jax.nn.sigmoid(a0 + xa @ a2)
    g = jax.nn.sigmoid(xg) @ g2
    heads = lambda t: t.astype(f32).reshape(n, L, A_HEADS, HEAD_DIM)
    kk = heads(k * k_k)
    kk = kk / jnp.maximum(jnp.sqrt(jnp.sum(kk * kk, axis=-1, keepdims=True)), 1e-12)
    k = k * (1.0 + (a - 1.0) * k_a)
    r_h, k_h, v_h = heads(r), heads(k), heads(v)
    if L % RWKV_BLOCK == 0:
        y, s_last = rwkv7_recurrence_pallas(r_h, heads(-jnp.exp(w.astype(f32))), k_h, v_h, kk, heads(a),
                                            s0.astype(f32))
    else:
        y, s_last = rwkv7_recurrence(r_h, heads(decay), k_h, v_h, kk, heads(a), s0.astype(f32))
    mean = jnp.mean(y, axis=-1, keepdims=True)
    var = jnp.mean(jnp.square(y - mean), axis=-1, keepdims=True)
    y = ((y - mean) * lax.rsqrt(var + GN_EPS)).reshape(n, L, A_WIDTH) * ln_w + ln_b
    bonus = jnp.sum(r_h * k_h * r_k, axis=-1, keepdims=True) * v_h
    y = (y + bonus.reshape(n, L, A_WIDTH)) * g
    return y.astype(cols.dtype), s_last, cols[:, -1]


def ssd_chunked(x, dt, A, B, C, h0):
    N, L, H, P = x.shape
    q = min(SSD_CHUNK, L)
    pad = (-L) % q
    if pad:
        padl = lambda t: jnp.pad(t, [(0, 0), (0, pad)] + [(0, 0)] * (t.ndim - 2))
        x, dt, B, C = padl(x), padl(dt), padl(B), padl(C)
    nc = (L + pad) // q
    ch = lambda t: t.reshape((N, nc, q) + t.shape[2:])
    xc, dtc, Bc, Cc = ch(x), ch(dt), ch(B), ch(C)
    cs = jnp.cumsum(dtc * A, axis=2)
    causal = jnp.tril(jnp.ones((q, q), bool))[None, None, :, :, None]
    seg = cs[:, :, :, None, :] - cs[:, :, None, :, :]
    decay = jnp.exp(jnp.where(causal, seg, -jnp.inf))
    scores = jnp.einsum('nclhd,ncshd->nclsh', Cc, Bc) * decay * dtc[:, :, None, :, :]
    y_intra = jnp.einsum('nclsh,ncshp->nclhp', scores, xc)
    to_end = jnp.exp(cs[:, :, -1:, :] - cs) * dtc
    chunk_states = jnp.einsum('ncsh,ncshd,ncshp->nchpd', to_end, Bc, xc)
    chunk_decay = jnp.exp(cs[:, :, -1, :])
    def carry(h, inp):
        dec, st = inp
        return h * dec[:, :, None, None] + st, h
    h_last, h_in = lax.scan(carry, h0, (jnp.moveaxis(chunk_decay, 1, 0), jnp.moveaxis(chunk_states, 1, 0)))
    h_in = jnp.moveaxis(h_in, 0, 1)
    y_inter = jnp.einsum('nclhd,nchpd->nclhp', Cc, h_in) * jnp.exp(cs)[..., None]
    y = (y_intra + y_inter).reshape(N, nc * q, H, P)[:, :L]
    return y, h_last


def mamba2_mixer(z, xbc, dt_raw, conv0, ssm0, conv_w, conv_b, dt_bias, a_log, d_skip, norm_w):
    n, L, _ = xbc.shape
    f32 = jnp.float32
    xpad = jnp.concatenate([conv0.astype(xbc.dtype), xbc], axis=1)
    conv = lax.conv_general_dilated(xpad, conv_w[:, None, :], window_strides=(1,), padding='VALID',
                                    dimension_numbers=('NWC', 'WIO', 'NWC'),
                                    feature_group_count=B_CONV_DIM) + conv_b
    xs, Bm, Cm = _split(jax.nn.silu(conv), [B_WIDTH, B_GROUPS * D_STATE, B_GROUPS * D_STATE])
    xh = xs.astype(f32).reshape(n, L, B_HEADS, HEAD_DIM)
    rep = B_HEADS // B_GROUPS
    Bh = jnp.repeat(Bm.astype(f32).reshape(n, L, B_GROUPS, D_STATE), rep, axis=2)
    Ch = jnp.repeat(Cm.astype(f32).reshape(n, L, B_GROUPS, D_STATE), rep, axis=2)
    dt = jax.nn.softplus(dt_raw.astype(f32) + dt_bias)
    A = -jnp.exp(a_log.astype(f32))
    y, h_last = ssd_chunked(xh, dt, A, Bh, Ch, ssm0.astype(f32))
    y = (y + d_skip[:, None] * xh).reshape(n, L, B_WIDTH) * jax.nn.silu(z.astype(f32))
    yg = y.reshape(n, L, B_GROUPS, B_WIDTH // B_GROUPS)
    yg = yg * lax.rsqrt(jnp.mean(yg * yg, axis=-1, keepdims=True) + NORM_EPS)
    y = yg.reshape(n, L, B_WIDTH) * norm_w
    return y.astype(z.dtype), h_last, xpad[:, -(CONV_W - 1):]


def fox_attention(q, k, v, logf, past):
    n, L, H, hd = q.shape
    f32 = jnp.float32
    cum = jnp.cumsum(logf, axis=1)
    if past is None:
        P = 0
        keys_k, keys_v, key_c = k, v, cum
    else:
        pool_k, pool_v, pool_logf, page_table = past
        P = page_table.shape[1] * PAGE_SIZE
        gather = lambda pool: pool[page_table].reshape((n, P) + pool.shape[2:])
        plf = gather(pool_logf).astype(f32)
        suffix = jnp.cumsum(plf[:, ::-1], axis=1)[:, ::-1] - plf
        keys_k = jnp.concatenate([gather(pool_k).astype(k.dtype), k], axis=1)
        keys_v = jnp.concatenate([gather(pool_v).astype(v.dtype), v], axis=1)
        key_c = jnp.concatenate([-suffix, cum], axis=1)
    kc = jnp.moveaxis(key_c, 2, 1)
    kpos = jnp.arange(P + L)
    qb = Q_BLOCK if L % Q_BLOCK == 0 else L
    nb = L // qb
    scale = hd ** -0.5

    def block(args):
        q_blk, c_blk, t0 = args
        s = jnp.einsum('nqhd,nkhd->nhqk', q_blk, keys_k, preferred_element_type=f32) * scale
        s = s + jnp.moveaxis(c_blk, 2, 1)[..., None] - kc[:, :, None, :]
        mask = kpos[None, :] <= (t0 + jnp.arange(qb))[:, None]
        prob = jax.nn.softmax(jnp.where(mask, s, -jnp.inf), axis=-1)
        return jnp.einsum('nhqk,nkhd->nqhd', prob.astype(keys_v.dtype), keys_v)

    q_blocks = jnp.moveaxis(q.reshape(n, nb, qb, H, hd), 1, 0)
    c_blocks = jnp.moveaxis(cum.reshape(n, nb, qb, H), 1, 0)
    starts = P + jnp.arange(nb) * qb
    out = lax.map(block, (q_blocks, c_blocks, starts))
    return jnp.moveaxis(out, 0, 1).reshape(n, L, H, hd)


def fox_mixer(q, k, v, f_raw, f_bias, qn_w, kn_w, past):
    n, L, _ = q.shape
    qh = head_rms(q.reshape(n, L, C_HEADS, HEAD_DIM), qn_w)
    kh = head_rms(k.reshape(n, L, C_HEADS, HEAD_DIM), kn_w)
    vh = v.reshape(n, L, C_HEADS, HEAD_DIM)
    logf = jax.nn.log_sigmoid(f_raw.astype(jnp.float32) + f_bias)
    o = fox_attention(qh, kh, vh, logf, past)
    return o.reshape(n, L, C_WIDTH).astype(q.dtype), kh, vh, logf


def decoder_layer(x, c, l, p, s0, shift0, ssm0, conv0, past):
    sh1, sc1, g1, sh2, sc2, g2 = ada_modulation(c, p['w_ada'][l], p['b_ada'][l])
    u = rms_norm(x, p['norm1_w'][l]) * (1 + sc1) + sh1
    proj = u @ p['w_in'][l]
    a_cols, b_z, b_xbc, b_dt, c_q, c_k, c_v, c_f = _split(
        proj, [A_IN, B_WIDTH, B_CONV_DIM, B_HEADS, C_WIDTH, C_WIDTH, C_WIDTH, C_HEADS])
    y_a, s_new, shift_new = rwkv7_mixer(
        a_cols, shift0, s0, p['a_mu'][l], p['a_w0'][l], p['a_w2'][l], p['a_a0'][l], p['a_a2'][l],
        p['a_g2'][l], p['a_kk'][l], p['a_ka'][l], p['a_rk'][l], p['a_ln_w'][l], p['a_ln_b'][l])
    y_b, ssm_new, conv_new = mamba2_mixer(
        b_z, b_xbc, b_dt, conv0, ssm0, p['b_conv_w'][l], p['b_conv_b'][l], p['b_dt_bias'][l],
        p['b_a_log'][l], p['b_d'][l], p['b_norm_w'][l])
    fox_past = None if past is None else (past[0][l], past[1][l], past[2][l], past[3])
    y_c, k_rows, v_rows, logf_rows = fox_mixer(
        c_q, c_k, c_v, c_f, p['c_f_bias'][l], p['c_qnorm_w'][l], p['c_knorm_w'][l], fox_past)
    x = x + g1 * (jnp.concatenate([y_a, y_b, y_c], axis=-1) @ p['w_out'][l])
    u2 = rms_norm(x, p['norm2_w'][l]) * (1 + sc2) + sh2
    if l % 2 == 0:
        j = l // 2
        f = swiglu(u2, p['ffn_w_gate'][j], p['ffn_w_up'][j], p['ffn_w_down'][j])
    else:
        j = l // 2
        f = moe_swiglu(u2, p['moe_router_w'][j], p['moe_router_b'][j], p['moe_w_gate'][j],
                       p['moe_w_up'][j], p['moe_w_down'][j])
    x = gated_residual(x, g2, f)
    dt = x.dtype
    return x, (s_new.astype(dt), shift_new.astype(dt), ssm_new.astype(dt), conv_new.astype(dt),
               k_rows.astype(dt), v_rows.astype(dt), logf_rows.astype(dt))


def run_trunk(x, c, p, init_states, past):
    outs = []
    for l in range(DEPTH):
        x, st = decoder_layer(x, c, l, p, *init_states[l], past)
        outs.append(st)
    stacked = [jnp.stack([o[i] for o in outs]) for i in range(7)]
    return x, stacked


def kernel(x_prompt, x_sample, cache_k, cache_v, cache_logf, state_rwkv, state_shift, state_ssm,
           state_conv, page_table, c_prompt, c_sample, norm1_w, norm2_w, w_ada, b_ada, w_in, w_out,
           a_mu, a_w0, a_w2, a_a0, a_a2, a_g2, a_kk, a_ka, a_rk, a_ln_w, a_ln_b,
           b_conv_w, b_conv_b, b_dt_bias, b_a_log, b_d, b_norm_w,
           c_f_bias, c_qnorm_w, c_knorm_w, ffn_w_gate, ffn_w_up, ffn_w_down,
           moe_router_w, moe_router_b, moe_w_gate, moe_w_up, moe_w_down):
    p = dict(norm1_w=norm1_w, norm2_w=norm2_w, w_ada=w_ada, b_ada=b_ada, w_in=w_in, w_out=w_out,
             a_mu=a_mu, a_w0=a_w0, a_w2=a_w2, a_a0=a_a0, a_a2=a_a2, a_g2=a_g2, a_kk=a_kk, a_ka=a_ka,
             a_rk=a_rk, a_ln_w=a_ln_w, a_ln_b=a_ln_b, b_conv_w=b_conv_w, b_conv_b=b_conv_b,
             b_dt_bias=b_dt_bias, b_a_log=b_a_log, b_d=b_d, b_norm_w=b_norm_w, c_f_bias=c_f_bias,
             c_qnorm_w=c_qnorm_w, c_knorm_w=c_knorm_w, ffn_w_gate=ffn_w_gate, ffn_w_up=ffn_w_up,
             ffn_w_down=ffn_w_down, moe_router_w=moe_router_w, moe_router_b=moe_router_b,
             moe_w_gate=moe_w_gate, moe_w_up=moe_w_up, moe_w_down=moe_w_down)
    n_p = x_prompt.shape[0]
    zero_state = (jnp.zeros((n_p, A_HEADS, HEAD_DIM, HEAD_DIM), jnp.float32),
                  jnp.zeros((n_p, A_IN), x_prompt.dtype),
                  jnp.zeros((n_p, B_HEADS, HEAD_DIM, D_STATE), jnp.float32),
                  jnp.zeros((n_p, CONV_W - 1, B_CONV_DIM), x_prompt.dtype))
    y_prompt, st_p = run_trunk(x_prompt, c_prompt, p, [zero_state] * DEPTH, None)
    rwkv_prompt, shift_prompt, ssm_prompt, conv_prompt, k_prompt, v_prompt, logf_prompt = st_p
    init_s = [(state_rwkv[l], state_shift[l], state_ssm[l], state_conv[l]) for l in range(DEPTH)]
    y_sample, st_s = run_trunk(x_sample, c_sample, p, init_s, (cache_k, cache_v, cache_logf, page_table))
    rwkv_sample, shift_sample, ssm_sample, conv_sample, k_sample, v_sample, logf_sample = st_s
    return (y_prompt, y_sample,
            k_prompt, v_prompt, logf_prompt, rwkv_prompt, shift_prompt, ssm_prompt, conv_prompt,
            k_sample, v_sample, logf_sample, rwkv_sample, shift_sample, ssm_sample, conv_sample)
```

```python
import math
import numpy as np
import jax
import jax.numpy as jnp
from jax import lax
from jax.experimental import pallas as pl
from jax.experimental.pallas import tpu as pltpu

D_MODEL = 1024
BATCH = 4
SEQ = 4096
DEPTH = 2
DEC_BATCH = 128
DEC_SEQ = 4
PAST_LEN = 2048
PAGE_SIZE = 128

D_MIX = D_MODEL
HEAD_DIM = 64
A_WIDTH = D_MIX // 4
A_HEADS = A_WIDTH // HEAD_DIM
A_DECAY_LORA = 64
A_ICL_LORA = 64
A_GATE_LORA = 128
A_IN = 3 * A_WIDTH + A_DECAY_LORA + A_ICL_LORA + A_GATE_LORA
GN_EPS = HEAD_DIM * 1e-5
B_WIDTH = D_MIX // 2
B_HEADS = B_WIDTH // HEAD_DIM
B_GROUPS = 2
D_STATE = 128
CONV_W = 4
SSD_CHUNK = 128
B_CONV_DIM = B_WIDTH + 2 * B_GROUPS * D_STATE
B_IN = B_WIDTH + B_CONV_DIM + B_HEADS
C_WIDTH = D_MIX - A_WIDTH - B_WIDTH
C_HEADS = C_WIDTH // HEAD_DIM
C_IN = 3 * C_WIDTH + C_HEADS
Q_BLOCK = 128
IN_WIDTH = A_IN + B_IN + C_IN
D_FF = 2816
N_EXPERTS = 8
TOP_K = 2
D_FF_EXPERT = 3584
N_DENSE = (DEPTH + 1) // 2
N_MOE = DEPTH // 2
NORM_EPS = 1e-6


BF16 = jnp.bfloat16
F32 = jnp.float32
LANE = 128
VMEM_LIMIT = 48 * 1024 * 1024
ROW_TILE = 256
FFN_ROW_TILE = 512
FFN_TILE_DENSE = D_FF // 2
FFN_TILE_EXPERT = D_FF_EXPERT // 4
ATT_TILE = 1024
IN_MAIN = A_IN + B_WIDTH + B_CONV_DIM + 3 * C_WIDTH
IN_PAD = IN_MAIN + LANE


def _mod_spec(mod, tile):
    if mod.shape[1] == 1:
        return pl.BlockSpec((1, 1, mod.shape[2]), lambda g, i: (g, 0, 0))
    return pl.BlockSpec((1, tile, mod.shape[2]), lambda g, i: (g, i, 0))


def _modulated_rms(x, nw, sc, sh):
    y = x * lax.rsqrt(jnp.mean(x * x, axis=-1, keepdims=True) + NORM_EPS) * nw
    return y * (1.0 + sc) + sh


def _gated_residual_kernel(x_ref, g_ref, f_ref, o_ref):
    o_ref[0] = x_ref[0] + g_ref[0] * f_ref[...]


def gated_residual(x, g, f, row0):
    G, R, d = x.shape
    t = min(R, 512)
    nt = R // t
    t0 = row0 // t
    return pl.pallas_call(
        _gated_residual_kernel,
        grid=(G, nt),
        in_specs=[pl.BlockSpec((1, t, d), lambda g, i: (g, i, 0)),
                  _mod_spec(g, t),
                  pl.BlockSpec((t, d), lambda g, i: (t0 + g * nt + i, 0))],
        out_specs=pl.BlockSpec((1, t, d), lambda g, i: (g, i, 0)),
        out_shape=jax.ShapeDtypeStruct(x.shape, x.dtype),
        name="gated_residual",
    )(x, g, f)


def _inproj_kernel(x_ref, nw_ref, sc_ref, sh_ref, w_ref, a_ref, z_ref, xbc_ref, qkv_ref, sm_ref):
    u = _modulated_rms(x_ref[0], nw_ref[...], sc_ref[0], sh_ref[0])
    p = jnp.dot(u.astype(BF16), w_ref[...], preferred_element_type=F32)
    o = 0
    for ref in (a_ref, z_ref, xbc_ref, qkv_ref, sm_ref):
        n = ref.shape[1]
        ref[...] = p[:, o:o + n]
        o += n


def in_projection(x, norm_w, sc, sh, w_perm):
    G, R, d = x.shape
    t = min(R, ROW_TILE)
    nt = R // t
    widths = (A_IN, B_WIDTH, B_CONV_DIM, 3 * C_WIDTH, LANE)
    return pl.pallas_call(
        _inproj_kernel,
        grid=(G, nt),
        in_specs=[pl.BlockSpec((1, t, d), lambda g, i: (g, i, 0)),
                  pl.BlockSpec((1, d), lambda g, i: (0, 0)),
                  _mod_spec(sc, t), _mod_spec(sh, t),
                  pl.BlockSpec((d, IN_PAD), lambda g, i: (0, 0))],
        out_specs=[pl.BlockSpec((t, n), lambda g, i: (g * nt + i, 0)) for n in widths],
        out_shape=[jax.ShapeDtypeStruct((G * R, n), F32) for n in widths],
        compiler_params=pltpu.CompilerParams(dimension_semantics=("parallel", "parallel"),
                                             vmem_limit_bytes=VMEM_LIMIT),
        name="in_projection",
    )(x, norm_w.reshape(1, d), sc, sh, w_perm)


def _outproj_kernel(ya_ref, yb_ref, yc_ref, w_ref, x_ref, g_ref, nw_ref, sc_ref, sh_ref, rw_ref, rb_ref,
                    x1_ref, u2_ref, lg_ref):
    acc = jnp.dot(ya_ref[...].astype(BF16), w_ref[0:A_WIDTH, :], preferred_element_type=F32)
    acc += jnp.dot(yb_ref[...].astype(BF16), w_ref[A_WIDTH:A_WIDTH + B_WIDTH, :], preferred_element_type=F32)
    acc += jnp.dot(yc_ref[...].astype(BF16), w_ref[A_WIDTH + B_WIDTH:, :], preferred_element_type=F32)
    x1 = x_ref[0] + g_ref[0] * acc
    x1_ref[...] = x1
    u2 = _modulated_rms(x1, nw_ref[...], sc_ref[0], sh_ref[0]).astype(BF16)
    u2_ref[...] = u2
    lg_ref[...] = jnp.dot(u2, rw_ref[...], preferred_element_type=F32) + rb_ref[...]


def out_projection(ya, yb, yc, w_out, x, g1, norm_w, sc, sh, router_w, router_b):
    G, R, d = x.shape
    t = min(R, ROW_TILE)
    nt = R // t
    rows = lambda n: pl.BlockSpec((t, n), lambda g, i: (g * nt + i, 0))
    full = lambda a: pl.BlockSpec(a.shape, lambda g, i: (0,) * a.ndim)
    nw = norm_w.reshape(1, d)
    return pl.pallas_call(
        _outproj_kernel,
        grid=(G, nt),
        in_specs=[rows(A_WIDTH), rows(B_WIDTH), rows(C_WIDTH), full(w_out),
                  pl.BlockSpec((1, t, d), lambda g, i: (g, i, 0)), _mod_spec(g1, t), full(nw),
                  _mod_spec(sc, t), _mod_spec(sh, t), full(router_w), full(router_b)],
        out_specs=[rows(d), rows(d), rows(LANE)],
        out_shape=[jax.ShapeDtypeStruct((G * R, d), F32), jax.ShapeDtypeStruct((G * R, d), BF16),
                   jax.ShapeDtypeStruct((G * R, LANE), F32)],
        compiler_params=pltpu.CompilerParams(dimension_semantics=("parallel", "parallel"),
                                             vmem_limit_bytes=VMEM_LIMIT),
        name="out_projection",
    )(ya, yb, yc, w_out, x, g1, nw, sc, sh, router_w, router_b)


def _swiglu_kernel(te_ref, tv_ref, u_ref, wg_ref, wu_ref, wd_ref, o_ref, acc_ref):
    i, j = pl.program_id(0), pl.program_id(1)
    last = pl.num_programs(1) - 1

    @pl.when(tv_ref[i] == 1)
    def _():
        u = u_ref[...]
        g = jnp.dot(u, wg_ref[0], preferred_element_type=F32)
        up = jnp.dot(u, wu_ref[0], preferred_element_type=F32)
        h = (g * jax.nn.sigmoid(g) * up).astype(BF16)
        part = jnp.dot(h, wd_ref[0], preferred_element_type=F32)

        @pl.when(j == 0)
        def _():
            acc_ref[...] = part

        @pl.when(j > 0)
        def _():
            acc_ref[...] += part

    @pl.when(j == last)
    def _():
        o_ref[...] = jnp.where(tv_ref[i] == 1, acc_ref[...], 0.0)


def grouped_swiglu(u, w_gate, w_up, w_down, tile_expert, tile_valid, ff_tile):
    P, d = u.shape
    F = w_gate.shape[2]
    t = FFN_ROW_TILE
    grid_spec = pltpu.PrefetchScalarGridSpec(
        num_scalar_prefetch=2,
        grid=(P // t, F // ff_tile),
        in_specs=[pl.BlockSpec((t, d), lambda i, j, te, tv: (i, 0)),
                  pl.BlockSpec((1, d, ff_tile), lambda i, j, te, tv: (te[i], 0, j)),
                  pl.BlockSpec((1, d, ff_tile), lambda i, j, te, tv: (te[i], 0, j)),
                  pl.BlockSpec((1, ff_tile, d), lambda i, j, te, tv: (te[i], j, 0))],
        out_specs=pl.BlockSpec((t, d), lambda i, j, te, tv: (i, 0)),
        scratch_shapes=[pltpu.VMEM((t, d), F32)])
    return pl.pallas_call(
        _swiglu_kernel,
        grid_spec=grid_spec,
        out_shape=jax.ShapeDtypeStruct((P, d), F32),
        compiler_params=pltpu.CompilerParams(dimension_semantics=("parallel", "arbitrary"),
                                             vmem_limit_bytes=VMEM_LIMIT),
        name="grouped_swiglu",
    )(tile_expert, tile_valid, u, w_gate, w_up, w_down)


def _fox_flash_kernel(q_ref, k_ref, v_ref, cq_ref, ck_ref, o_ref, m_scr, l_scr, acc_scr):
    qi, ki = pl.program_id(2), pl.program_id(3)
    tq, tk = q_ref.shape[2], k_ref.shape[2]

    @pl.when(ki == 0)
    def _():
        m_scr[...] = jnp.full(m_scr.shape, -jnp.inf, F32)
        l_scr[...] = jnp.zeros(l_scr.shape, F32)
        acc_scr[...] = jnp.zeros(acc_scr.shape, F32)

    @pl.when(ki * tk <= qi * tq + (tq - 1))
    def _():
        s = lax.dot_general(q_ref[0, 0], k_ref[0, 0], _NT, preferred_element_type=F32)
        s = s + cq_ref[0, 0] - ck_ref[0, 0]
        qpos = qi * tq + lax.broadcasted_iota(jnp.int32, (tq, tk), 0)
        kpos = ki * tk + lax.broadcasted_iota(jnp.int32, (tq, tk), 1)
        s = jnp.where(kpos <= qpos, s, -jnp.inf)
        m_prev = m_scr[...]
        m_new = jnp.maximum(m_prev, jnp.max(s, axis=-1, keepdims=True))
        alpha = jnp.exp(m_prev - m_new)
        p = jnp.exp(s - m_new)
        l_scr[...] = alpha * l_scr[...] + jnp.sum(p, axis=-1, keepdims=True)
        acc_scr[...] = alpha * acc_scr[...] + jnp.dot(p.astype(BF16), v_ref[0, 0], preferred_element_type=F32)
        m_scr[...] = m_new

    @pl.when(ki == pl.num_programs(3) - 1)
    def _():
        o_ref[0, 0] = acc_scr[...] / l_scr[...]


def fox_flash_attention(q, k, v, cq, ck):
    N, H, L, hd = q.shape
    t = min(L, ATT_TILE)
    nt = L // t
    kv_spec = pl.BlockSpec((1, 1, t, hd), lambda n, h, qi, ki: (n, h, jnp.minimum(ki, qi), 0))
    return pl.pallas_call(
        _fox_flash_kernel,
        grid=(N, H, nt, nt),
        in_specs=[pl.BlockSpec((1, 1, t, hd), lambda n, h, qi, ki: (n, h, qi, 0)), kv_spec, kv_spec,
                  pl.BlockSpec((1, 1, t, 1), lambda n, h, qi, ki: (n, h, qi, 0)),
                  pl.BlockSpec((1, 1, 1, t), lambda n, h, qi, ki: (n, h, 0, jnp.minimum(ki, qi)))],
        out_specs=pl.BlockSpec((1, 1, t, hd), lambda n, h, qi, ki: (n, h, qi, 0)),
        out_shape=jax.ShapeDtypeStruct((N, H, L, hd), F32),
        scratch_shapes=[pltpu.VMEM((t, 1), F32), pltpu.VMEM((t, 1), F32), pltpu.VMEM((t, hd), F32)],
        compiler_params=pltpu.CompilerParams(
            dimension_semantics=("parallel", "parallel", "parallel", "arbitrary"), vmem_limit_bytes=VMEM_LIMIT),
        name="fox_flash_attention",
    )(q, k, v, cq, ck)


def _split(t, sizes):
    return jnp.split(t, np.cumsum(sizes)[:-1].tolist(), axis=-1)


def rms_norm(x, w):
    xf = x.astype(jnp.float32)
    y = xf * lax.rsqrt(jnp.mean(xf * xf, axis=-1, keepdims=True) + NORM_EPS)
    return (y * w.astype(jnp.float32)).astype(x.dtype)


def head_rms(t, w):
    tf = t.astype(jnp.float32)
    return (tf * lax.rsqrt(jnp.mean(tf * tf, axis=-1, keepdims=True) + NORM_EPS) * w).astype(t.dtype)


def ada_modulation(c, w_ada, b_ada):
    m = jax.nn.silu(c) @ w_ada + b_ada
    return jnp.split(m[:, None, :], 6, axis=-1)


def swiglu(t, w_gate, w_up, w_down):
    return (jax.nn.silu(t @ w_gate) * (t @ w_up)) @ w_down


def dense_swiglu(u, w_gate, w_up, w_down, ff_tile):
    nt = u.shape[0] // FFN_ROW_TILE
    return grouped_swiglu(u, w_gate[None], w_up[None], w_down[None],
                          jnp.zeros((nt,), jnp.int32), jnp.ones((nt,), jnp.int32), ff_tile)


def moe_swiglu(u, logits, w_gate, w_up, w_down, ff_tile):
    T, d = u.shape
    E = logits.shape[1]
    t = FFN_ROW_TILE
    top_val, top_idx = lax.top_k(logits, TOP_K)
    weights = jax.nn.softmax(top_val, axis=-1)
    eid = top_idx.reshape(-1).astype(jnp.int32)
    n_pairs = T * TOP_K
    order = jnp.argsort(eid, stable=True).astype(jnp.int32)
    counts = jnp.sum(jax.nn.one_hot(eid, E, dtype=jnp.int32), axis=0)
    padded = ((counts + t - 1) // t) * t
    ends_p = jnp.cumsum(padded)
    start_p = ends_p - padded
    start = jnp.cumsum(counts) - counts
    e_sorted = eid[order]
    pos_sorted = start_p[e_sorted] + (jnp.arange(n_pairs, dtype=jnp.int32) - start[e_sorted])
    P = n_pairs + E * t
    src_tok = jnp.zeros((P,), jnp.int32).at[pos_sorted].set(order // TOP_K)
    pos_of_pair = jnp.zeros((n_pairs,), jnp.int32).at[order].set(pos_sorted).reshape(T, TOP_K)
    tile_start = jnp.arange(P // t, dtype=jnp.int32) * t
    tile_expert = jnp.minimum(jnp.searchsorted(ends_p, tile_start, side='right'), E - 1).astype(jnp.int32)
    tile_valid = (tile_start < ends_p[-1]).astype(jnp.int32)
    ys = grouped_swiglu(u[src_tok], w_gate, w_up, w_down, tile_expert, tile_valid, ff_tile)
    out = weights[:, 0:1] * ys[pos_of_pair[:, 0]]
    for s in range(1, TOP_K):
        out = out + weights[:, s:s + 1] * ys[pos_of_pair[:, s]]
    return out


RWKV_CHUNK = 64
RWKV_BLOCK = 128
RWKV_INV_BLOCK = 16

_NN = (((1,), (0,)), ((), ()))
_NT = (((1,), (1,)), ((), ()))


def _dot(a, b, dims):
    return lax.dot_general(a, b, dims, preferred_element_type=jnp.float32)


def _split_bf16(x):
    hi = x.astype(jnp.bfloat16)
    lo = (x - hi.astype(jnp.float32)).astype(jnp.bfloat16)
    return hi, lo


def _mm3(a, b, dims=_NN):
    ah, al = _split_bf16(a)
    bh, bl = _split_bf16(b)
    return _dot(ah, bh, dims) + _dot(ah, bl, dims) + _dot(al, bh, dims)


def _mm_exact_lhs(a_bf16, x):
    h1, r1 = x.astype(jnp.bfloat16), None
    r1 = x - h1.astype(jnp.float32)
    h2 = r1.astype(jnp.bfloat16)
    h3 = (r1 - h2.astype(jnp.float32)).astype(jnp.bfloat16)
    return _dot(a_bf16, h1, _NN) + _dot(a_bf16, h2, _NN) + _dot(a_bf16, h3, _NN)


def _unit_lower_inverse(a, blk_mask, eye):
    ad = jnp.where(blk_mask, a, 0.0)
    e = a - ad
    p2 = _mm3(ad, ad)
    p4 = _mm3(p2, p2)
    p8 = _mm3(p4, p4)
    x = _mm3(eye - ad, eye + p2)
    x = _mm3(x, eye + p4)
    dinv = _mm3(x, eye + p8)
    n = _mm3(dinv, e)
    n2 = _mm3(n, n)
    return _mm3(_mm3(eye - n, eye + n2), dinv)


def _rwkv_subchunk(S, r, lw, k, v, vt, kk, a, consts):
    ltri, strict, incl, blk, eye = consts
    T = r.shape[0]
    c = _mm_exact_lhs(ltri, lw)
    cp = c - lw
    c_last = c[T - 1:T, :]
    e_c = jnp.exp(c)
    e_nc = jnp.exp(-c)
    e_end = jnp.exp(c_last - c)
    b = kk * a
    kt = kk * jnp.exp(cp)
    rt = r * e_c
    kh = k * e_nc
    bh = b * e_nc
    kbar = k * e_end
    bbar = b * e_end
    a_kb = jnp.where(strict, _mm3(kt, bh, _NT), 0.0)
    a_kk = jnp.where(strict, _mm3(kt, kh, _NT), 0.0)
    a_rk = jnp.where(incl, _mm3(rt, kh, _NT), 0.0)
    a_rb = jnp.where(incl, _mm3(rt, bh, _NT), 0.0)
    m = _unit_lower_inverse(a_kb, blk, eye)
    akkv = _mm3(a_kk, v)
    mk = _mm3(m, kt)
    g1 = _mm3(m, akkv)
    u = _mm3(mk, S, _NT) + g1
    y = _mm3(rt, S, _NT) + _mm3(a_rk, v) - _mm3(a_rb, u)
    g1t = _mm3(_mm3(vt, a_kk, _NT), m, _NT)
    ut = _mm3(S, mk, _NT) + g1t
    s_new = S * jnp.exp(c_last) + _mm3(vt, kbar) - _mm3(ut, bbar)
    return y, s_new


def _rwkv_kernel(r_ref, lw_ref, k_ref, v_ref, vt_ref, kk_ref, a_ref, s0_ref, y_ref, sT_ref, s_scr):
    H = r_ref.shape[1]
    T = RWKV_CHUNK

    @pl.when(pl.program_id(1) == 0)
    def _():
        s_scr[...] = s0_ref[0]

    row = lax.broadcasted_iota(jnp.int32, (T, T), 0)
    col = lax.broadcasted_iota(jnp.int32, (T, T), 1)
    consts = ((row >= col).astype(jnp.bfloat16), row > col, row >= col,
              (row // RWKV_INV_BLOCK) == (col // RWKV_INV_BLOCK), (row == col).astype(jnp.float32))
    for h in range(H):
        S = s_scr[h]
        for j in range(RWKV_BLOCK // T):
            sl = slice(j * T, (j + 1) * T)
            y, S = _rwkv_subchunk(S, r_ref[0, h, sl, :], lw_ref[0, h, sl, :], k_ref[0, h, sl, :],
                                  v_ref[0, h, sl, :], vt_ref[0, h, :, sl], kk_ref[0, h, sl, :],
                                  a_ref[0, h, sl, :], consts)
            y_ref[0, h, sl, :] = y
        s_scr[h] = S
    sT_ref[0] = s_scr[...]


def rwkv7_recurrence_pallas(r, lw, k, v, kk, a, s0):
    N, L, H, D = r.shape
    hm = lambda t: jnp.transpose(t, (0, 2, 1, 3))
    vt = jnp.transpose(v, (0, 2, 3, 1))
    row_spec = pl.BlockSpec((1, H, RWKV_BLOCK, D), lambda n, c: (n, 0, c, 0))
    st_spec = pl.BlockSpec((1, H, D, D), lambda n, c: (n, 0, 0, 0))
    y, s_last = pl.pallas_call(
        _rwkv_kernel,
        grid=(N, L // RWKV_BLOCK),
        in_specs=[row_spec, row_spec, row_spec, row_spec,
                  pl.BlockSpec((1, H, D, RWKV_BLOCK), lambda n, c: (n, 0, 0, c)),
                  row_spec, row_spec, st_spec],
        out_specs=[row_spec, st_spec],
        out_shape=[jax.ShapeDtypeStruct((N, H, L, D), jnp.float32),
                   jax.ShapeDtypeStruct((N, H, D, D), jnp.float32)],
        scratch_shapes=[pltpu.VMEM((H, D, D), jnp.float32)],
        compiler_params=pltpu.CompilerParams(dimension_semantics=("parallel", "arbitrary")),
        name="rwkv7_chunked",
    )(hm(r), hm(lw), hm(k), hm(v), vt, hm(kk), hm(a), s0)
    return jnp.transpose(y, (0, 2, 1, 3)), s_last


def rwkv7_recurrence(r, decay, k, v, kk, a, s0):
    def step(s, inp):
        r_t, w_t, k_t, v_t, kk_t, a_t = inp
        s_kk = jnp.einsum('nhvk,nhk->nhv', s, kk_t)
        s = (s * w_t[:, :, None, :]
             - s_kk[..., None] * (kk_t * a_t)[:, :, None, :]
             + v_t[..., None] * k_t[:, :, None, :])
        return s, jnp.einsum('nhvk,nhk->nhv', s, r_t)
    seq_first = tuple(jnp.moveaxis(t, 1, 0) for t in (r, decay, k, v, kk, a))
    s_last, ys = lax.scan(step, s0, seq_first)
    return jnp.moveaxis(ys, 0, 1), s_last


def rwkv7_mixer(cols, shift0, s0, mu, w0, w2, a0, a2, g2, k_k, k_a, r_k, ln_w, ln_b):
    n, L, _ = cols.shape
    f32 = jnp.float32
    prev = jnp.concatenate([shift0[:, None, :].astype(cols.dtype), cols[:, :-1]], axis=1)
    xs = cols + (prev - cols) * mu
    r, k, v, xw, xa, xg = _split(xs, [A_WIDTH, A_WIDTH, A_WIDTH, A_DECAY_LORA, A_ICL_LORA, A_GATE_LORA])
    w = -jax.nn.softplus(-(w0 + jnp.tanh(xw) @ w2)) - 0.5
    decay = jnp.exp(-jnp.exp(w.astype(f32)))
    a = jax.nn.sigmoid(a0 + xa @ a2)
    g = jax.nn.sigmoid(xg) @ g2
    heads = lambda t: t.astype(f32).reshape(n, L, A_HEADS, HEAD_DIM)
    kk = heads(k * k_k)
    kk = kk / jnp.maximum(jnp.sqrt(jnp.sum(kk * kk, axis=-1, keepdims=True)), 1e-12)
    k = k * (1.0 + (a - 1.0) * k_a)
    r_h, k_h, v_h = heads(r), heads(k), heads(v)
    if L % RWKV_BLOCK == 0:
        y, s_last = rwkv7_recurrence_pallas(r_h, heads(-jnp.exp(w.astype(f32))), k_h, v_h, kk, heads(a),
                                            s0.astype(f32))
    else:
        y, s_last = rwkv7_recurrence(r_h, heads(decay), k_h, v_h, kk, heads(a), s0.astype(f32))
    mean = jnp.mean(y, axis=-1, keepdims=True)
    var = jnp.mean(jnp.square(y - mean), axis=-1, keepdims=True)
    y = ((y - mean) * lax.rsqrt(var + GN_EPS)).reshape(n, L, A_WIDTH) * ln_w + ln_b
    bonus = jnp.sum(r_h * k_h * r_k, axis=-1, keepdims=True) * v_h
    y = (y + bonus.reshape(n, L, A_WIDTH)) * g
    return y.astype(cols.dtype), s_last, cols[:, -1]


def ssd_chunked(x, dt, A, B, C, h0):
    N, L, H, P = x.shape
    q = min(SSD_CHUNK, L)
    pad = (-L) % q
    if pad:
        padl = lambda t: jnp.pad(t, [(0, 0), (0, pad)] + [(0, 0)] * (t.ndim - 2))
        x, dt, B, C = padl(x), padl(dt), padl(B), padl(C)
    nc = (L + pad) // q
    ch = lambda t: t.reshape((N, nc, q) + t.shape[2:])
    xc, dtc, Bc, Cc = ch(x), ch(dt), ch(B), ch(C)
    cs = jnp.cumsum(dtc * A, axis=2)
    causal = jnp.tril(jnp.ones((q, q), bool))[None, None, :, :, None]
    seg = cs[:, :, :, None, :] - cs[:, :, None, :, :]
    decay = jnp.exp(jnp.where(causal, seg, -jnp.inf))
    scores = jnp.einsum('nclhd,ncshd->nclsh', Cc, Bc) * decay * dtc[:, :, None, :, :]
    y_intra = jnp.einsum('nclsh,ncshp->nclhp', scores, xc)
    to_end = jnp.exp(cs[:, :, -1:, :] - cs) * dtc
    chunk_states = jnp.einsum('ncsh,ncshd,ncshp->nchpd', to_end, Bc, xc)
    chunk_decay = jnp.exp(cs[:, :, -1, :])
    def carry(h, inp):
        dec, st = inp
        return h * dec[:, :, None, None] + st, h
    h_last, h_in = lax.scan(carry, h0, (jnp.moveaxis(chunk_decay, 1, 0), jnp.moveaxis(chunk_states, 1, 0)))
    h_in = jnp.moveaxis(h_in, 0, 1)
    y_inter = jnp.einsum('nclhd,nchpd->nclhp', Cc, h_in) * jnp.exp(cs)[..., None]
    y = (y_intra + y_inter).reshape(N, nc * q, H, P)[:, :L]
    return y, h_last


def mamba2_mixer(z, xbc, dt_raw, conv0, ssm0, conv_w, conv_b, dt_bias, a_log, d_skip, norm_w):
    n, L, _ = xbc.shape
    f32 = jnp.float32
    xpad = jnp.concatenate([conv0.astype(xbc.dtype), xbc], axis=1)
    conv = lax.conv_general_dilated(xpad, conv_w[:, None, :], window_strides=(1,), padding='VALID',
                                    dimension_numbers=('NWC', 'WIO', 'NWC'),
                                    feature_group_count=B_CONV_DIM) + conv_b
    xs, Bm, Cm = _split(jax.nn.silu(conv), [B_WIDTH, B_GROUPS * D_STATE, B_GROUPS * D_STATE])
    xh = xs.astype(f32).reshape(n, L, B_HEADS, HEAD_DIM)
    rep = B_HEADS // B_GROUPS
    Bh = jnp.repeat(Bm.astype(f32).reshape(n, L, B_GROUPS, D_STATE), rep, axis=2)
    Ch = jnp.repeat(Cm.astype(f32).reshape(n, L, B_GROUPS, D_STATE), rep, axis=2)
    dt = jax.nn.softplus(dt_raw.astype(f32) + dt_bias)
    A = -jnp.exp(a_log.astype(f32))
    y, h_last = ssd_chunked(xh, dt, A, Bh, Ch, ssm0.astype(f32))
    y = (y + d_skip[:, None] * xh).reshape(n, L, B_WIDTH) * jax.nn.silu(z.astype(f32))
    yg = y.reshape(n, L, B_GROUPS, B_WIDTH // B_GROUPS)
    yg = yg * lax.rsqrt(jnp.mean(yg * yg, axis=-1, keepdims=True) + NORM_EPS)
    y = yg.reshape(n, L, B_WIDTH) * norm_w
    return y.astype(z.dtype), h_last, xpad[:, -(CONV_W - 1):]


def fox_attention(q, k, v, logf, past):
    n, L, H, hd = q.shape
    f32 = jnp.float32
    cum = jnp.cumsum(logf, axis=1)
    if past is None:
        P = 0
        keys_k, keys_v, key_c = k, v, cum
    else:
        pool_k, pool_v, pool_logf, page_table = past
        P = page_table.shape[1] * PAGE_SIZE
        gather = lambda pool: pool[page_table].reshape((n, P) + pool.shape[2:])
        plf = gather(pool_logf).astype(f32)
        suffix = jnp.cumsum(plf[:, ::-1], axis=1)[:, ::-1] - plf
        keys_k = jnp.concatenate([gather(pool_k).astype(k.dtype), k], axis=1)
        keys_v = jnp.concatenate([gather(pool_v).astype(v.dtype), v], axis=1)
        key_c = jnp.concatenate([-suffix, cum], axis=1)
    kc = jnp.moveaxis(key_c, 2, 1)
    kpos = jnp.arange(P + L)
    qb = Q_BLOCK if L % Q_BLOCK == 0 else L
    nb = L // qb
    scale = hd ** -0.5

    def block(args):
        q_blk, c_blk, t0 = args
        s = jnp.einsum('nqhd,nkhd->nhqk', q_blk, keys_k, preferred_element_type=f32) * scale
        s = s + jnp.moveaxis(c_blk, 2, 1)[..., None] - kc[:, :, None, :]
        mask = kpos[None, :] <= (t0 + jnp.arange(qb))[:, None]
        prob = jax.nn.softmax(jnp.where(mask, s, -jnp.inf), axis=-1)
        return jnp.einsum('nhqk,nkhd->nqhd', prob.astype(keys_v.dtype), keys_v)

    q_blocks = jnp.moveaxis(q.reshape(n, nb, qb, H, hd), 1, 0)
    c_blocks = jnp.moveaxis(cum.reshape(n, nb, qb, H), 1, 0)
    starts = P + jnp.arange(nb) * qb
    out = lax.map(block, (q_blocks, c_blocks, starts))
    return jnp.moveaxis(out, 0, 1).reshape(n, L, H, hd)


def fox_mixer(q, k, v, f_raw, f_bias, qn_w, kn_w, past):
    n, L, _ = q.shape
    qh = head_rms(q.reshape(n, L, C_HEADS, HEAD_DIM), qn_w)
    kh = head_rms(k.reshape(n, L, C_HEADS, HEAD_DIM), kn_w)
    vh = v.reshape(n, L, C_HEADS, HEAD_DIM)
    logf = jax.nn.log_sigmoid(f_raw.astype(jnp.float32) + f_bias)
    if past is None and L % ATT_TILE == 0:
        cum = jnp.transpose(jnp.cumsum(logf, axis=1), (0, 2, 1))
        hm = lambda t: jnp.transpose(t, (0, 2, 1, 3)).astype(BF16)
        o = fox_flash_attention(hm(qh * (HEAD_DIM ** -0.5)), hm(kh), hm(vh), cum[..., None], cum[:, :, None, :])
        o = jnp.transpose(o, (0, 2, 1, 3))
    else:
        o = fox_attention(qh, kh, vh, logf, past)
    return o.reshape(n, L, C_WIDTH).astype(q.dtype), kh, vh, logf


def mix_group(x, c, l, p, s0, shift0, ssm0, conv0, past):
    n, L, d = x.shape
    mods = ada_modulation(c, p['w_ada'][l], p['b_ada'][l])
    if L >= ROW_TILE:
        xg = x
    else:
        xg = x.reshape(1, n * L, d)
        mods = [jnp.broadcast_to(m, (n, L, d)).reshape(1, n * L, d) for m in mods]
    sh1, sc1, g1, sh2, sc2, g2 = mods
    a_cols, b_z, b_xbc, c_qkv, small = in_projection(xg, p['norm1_w'][l], sc1, sh1, p['w_in_perm'][l])
    rs = lambda t: t.reshape(n, L, t.shape[-1])
    a_cols, b_z, b_xbc = rs(a_cols), rs(b_z), rs(b_xbc)
    b_dt, c_f = rs(small[:, :B_HEADS]), rs(small[:, B_HEADS:B_HEADS + C_HEADS])
    c_q, c_k, c_v = (rs(c_qkv[:, i * C_WIDTH:(i + 1) * C_WIDTH]) for i in range(3))
    y_a, s_new, shift_new = rwkv7_mixer(
        a_cols, shift0, s0, p['a_mu'][l], p['a_w0'][l], p['a_w2'][l], p['a_a0'][l], p['a_a2'][l],
        p['a_g2'][l], p['a_kk'][l], p['a_ka'][l], p['a_rk'][l], p['a_ln_w'][l], p['a_ln_b'][l])
    y_b, ssm_new, conv_new = mamba2_mixer(
        b_z, b_xbc, b_dt, conv0, ssm0, p['b_conv_w'][l], p['b_conv_b'][l], p['b_dt_bias'][l],
        p['b_a_log'][l], p['b_d'][l], p['b_norm_w'][l])
    fox_past = None if past is None else (past[0][l], past[1][l], past[2][l], past[3])
    y_c, k_rows, v_rows, logf_rows = fox_mixer(
        c_q, c_k, c_v, c_f, p['c_f_bias'][l], p['c_qnorm_w'][l], p['c_knorm_w'][l], fox_past)
    fl = lambda t: t.reshape(n * L, t.shape[-1])
    x1, u2, logits = out_projection(fl(y_a), fl(y_b), fl(y_c), p['w_out_bf16'][l], xg, g1, p['norm2_w'][l],
                                    sc2, sh2, p['router_w_pad'][l // 2], p['router_b_pad'][l // 2])
    dt = x.dtype
    states = (s_new.astype(dt), shift_new.astype(dt), ssm_new.astype(dt), conv_new.astype(dt),
              k_rows.astype(dt), v_rows.astype(dt), logf_rows.astype(dt))
    return x1.reshape(xg.shape), u2, logits, g2, states


def channel_mix(l, p, x1s, u2s, logits, g2s):
    u2 = jnp.concatenate(u2s, axis=0)
    j = l // 2
    if l % 2 == 0:
        f = dense_swiglu(u2, p['ffn_w_gate'][j], p['ffn_w_up'][j], p['ffn_w_down'][j], FFN_TILE_DENSE)
    else:
        lg = jnp.concatenate(logits, axis=0)[:, :N_EXPERTS]
        f = moe_swiglu(u2, lg, p['moe_w_gate'][j], p['moe_w_up'][j], p['moe_w_down'][j], FFN_TILE_EXPERT)
    outs, row0 = [], 0
    for x1, g2 in zip(x1s, g2s):
        outs.append(gated_residual(x1, g2, f, row0))
        row0 += x1.shape[0] * x1.shape[1]
    return outs


def run_trunk(xs, cs, p, init_states, pasts):
    outs = [[] for _ in xs]
    shapes = [x.shape for x in xs]
    for l in range(DEPTH):
        halves = [mix_group(x.reshape(s), c, l, p, *st[l], past)
                  for x, s, c, st, past in zip(xs, shapes, cs, init_states, pasts)]
        for o, h in zip(outs, halves):
            o.append(h[4])
        xs = channel_mix(l, p, [h[0] for h in halves], [h[1] for h in halves], [h[2] for h in halves],
                         [h[3] for h in halves])
    stacked = [[jnp.stack([o[i] for o in og]) for i in range(7)] for og in outs]
    return [x.reshape(s) for x, s in zip(xs, shapes)], stacked


def kernel(x_prompt, x_sample, cache_k, cache_v, cache_logf, state_rwkv, state_shift, state_ssm,
           state_conv, page_table, c_prompt, c_sample, norm1_w, norm2_w, w_ada, b_ada, w_in, w_out,
           a_mu, a_w0, a_w2, a_a0, a_a2, a_g2, a_kk, a_ka, a_rk, a_ln_w, a_ln_b,
           b_conv_w, b_conv_b, b_dt_bias, b_a_log, b_d, b_norm_w,
           c_f_bias, c_qnorm_w, c_knorm_w, ffn_w_gate, ffn_w_up, ffn_w_down,
           moe_router_w, moe_router_b, moe_w_gate, moe_w_up, moe_w_down):
    p = dict(norm1_w=norm1_w, norm2_w=norm2_w, w_ada=w_ada, b_ada=b_ada, w_in=w_in, w_out=w_out,
             a_mu=a_mu, a_w0=a_w0, a_w2=a_w2, a_a0=a_a0, a_a2=a_a2, a_g2=a_g2, a_kk=a_kk, a_ka=a_ka,
             a_rk=a_rk, a_ln_w=a_ln_w, a_ln_b=a_ln_b, b_conv_w=b_conv_w, b_conv_b=b_conv_b,
             b_dt_bias=b_dt_bias, b_a_log=b_a_log, b_d=b_d, b_norm_w=b_norm_w, c_f_bias=c_f_bias,
             c_qnorm_w=c_qnorm_w, c_knorm_w=c_knorm_w, ffn_w_gate=ffn_w_gate, ffn_w_up=ffn_w_up,
             ffn_w_down=ffn_w_down, moe_router_w=moe_router_w, moe_router_b=moe_router_b,
             moe_w_gate=moe_w_gate, moe_w_up=moe_w_up, moe_w_down=moe_w_down)
    o_dt, o_q, o_f = A_IN + B_WIDTH + B_CONV_DIM, A_IN + B_IN, A_IN + B_IN + 3 * C_WIDTH
    p['w_in_perm'] = jnp.concatenate(
        [w_in[:, :, :o_dt], w_in[:, :, o_q:o_f], w_in[:, :, o_dt:o_q], w_in[:, :, o_f:],
         jnp.zeros((DEPTH, D_MODEL, LANE - B_HEADS - C_HEADS), w_in.dtype)], axis=-1).astype(BF16)
    p['w_out_bf16'] = w_out.astype(BF16)
    p['router_w_pad'] = jnp.pad(moe_router_w, ((0, 0), (0, 0), (0, LANE - N_EXPERTS))).astype(BF16)
    p['router_b_pad'] = jnp.pad(moe_router_b.astype(F32), ((0, 0), (0, LANE - N_EXPERTS)))[:, None, :]
    for name in ('ffn_w_gate', 'ffn_w_up', 'ffn_w_down', 'moe_w_gate', 'moe_w_up', 'moe_w_down'):
        p[name] = p[name].astype(BF16)
    n_p = x_prompt.shape[0]
    zero_state = (jnp.zeros((n_p, A_HEADS, HEAD_DIM, HEAD_DIM), jnp.float32),
                  jnp.zeros((n_p, A_IN), x_prompt.dtype),
                  jnp.zeros((n_p, B_HEADS, HEAD_DIM, D_STATE), jnp.float32),
                  jnp.zeros((n_p, CONV_W - 1, B_CONV_DIM), x_prompt.dtype))
    init_s = [(state_rwkv[l], state_shift[l], state_ssm[l], state_conv[l]) for l in range(DEPTH)]
    (y_prompt, y_sample), (st_p, st_s) = run_trunk(
        [x_prompt, x_sample], [c_prompt, c_sample], p, [[zero_state] * DEPTH, init_s],
        [None, (cache_k, cache_v, cache_logf, page_table)])
    rwkv_prompt, shift_prompt, ssm_prompt, conv_prompt, k_prompt, v_prompt, logf_prompt = st_p
    rwkv_sample, shift_sample, ssm_sample, conv_sample, k_sample, v_sample, logf_sample = st_s
    return (y_prompt, y_sample,
            k_prompt, v_prompt, logf_prompt, rwkv_prompt, shift_prompt, ssm_prompt, conv_prompt,
            k_sample, v_sample, logf_sample, rwkv_sample, shift_sample, ssm_sample, conv_sample)
```

```python
import math
import numpy as np
import jax
import jax.numpy as jnp
from jax import lax
from jax.experimental import pallas as pl
from jax.experimental.pallas import tpu as pltpu

D_MODEL = 1024
BATCH = 4
SEQ = 4096
DEPTH = 2
DEC_BATCH = 128
DEC_SEQ = 4
PAST_LEN = 2048
PAGE_SIZE = 128

D_MIX = D_MODEL
HEAD_DIM = 64
A_WIDTH = D_MIX // 4
A_HEADS = A_WIDTH // HEAD_DIM
A_DECAY_LORA = 64
A_ICL_LORA = 64
A_GATE_LORA = 128
A_IN = 3 * A_WIDTH + A_DECAY_LORA + A_ICL_LORA + A_GATE_LORA
GN_EPS = HEAD_DIM * 1e-5
B_WIDTH = D_MIX // 2
B_HEADS = B_WIDTH // HEAD_DIM
B_GROUPS = 2
D_STATE = 128
CONV_W = 4
SSD_CHUNK = 128
B_CONV_DIM = B_WIDTH + 2 * B_GROUPS * D_STATE
B_IN = B_WIDTH + B_CONV_DIM + B_HEADS
C_WIDTH = D_MIX - A_WIDTH - B_WIDTH
C_HEADS = C_WIDTH // HEAD_DIM
C_IN = 3 * C_WIDTH + C_HEADS
Q_BLOCK = 128
IN_WIDTH = A_IN + B_IN + C_IN
D_FF = 2816
N_EXPERTS = 8
TOP_K = 2
D_FF_EXPERT = 3584
N_DENSE = (DEPTH + 1) // 2
N_MOE = DEPTH // 2
NORM_EPS = 1e-6


BF16 = jnp.bfloat16
F32 = jnp.float32
LANE = 128
VMEM_LIMIT = 48 * 1024 * 1024
ROW_TILE = 256
FFN_ROW_TILE = 512
FFN_TILE_DENSE = D_FF // 2
FFN_TILE_EXPERT = D_FF_EXPERT // 4
ATT_TILE = 1024
IN_MAIN = A_IN + B_WIDTH + B_CONV_DIM + 3 * C_WIDTH
IN_PAD = IN_MAIN + LANE


def _mod_spec(mod, tile):
    if mod.shape[1] == 1:
        return pl.BlockSpec((1, 1, mod.shape[2]), lambda g, i: (g, 0, 0))
    return pl.BlockSpec((1, tile, mod.shape[2]), lambda g, i: (g, i, 0))


def _modulated_rms(x, nw, sc, sh):
    y = x * lax.rsqrt(jnp.mean(x * x, axis=-1, keepdims=True) + NORM_EPS) * nw
    return y * (1.0 + sc) + sh


def _gated_residual_kernel(x_ref, g_ref, f_ref, o_ref):
    o_ref[0] = x_ref[0] + g_ref[0] * f_ref[...]


def gated_residual(x, g, f, row0):
    G, R, d = x.shape
    t = min(R, 512)
    nt = R // t
    t0 = row0 // t
    return pl.pallas_call(
        _gated_residual_kernel,
        grid=(G, nt),
        in_specs=[pl.BlockSpec((1, t, d), lambda g, i: (g, i, 0)),
                  _mod_spec(g, t),
                  pl.BlockSpec((t, d), lambda g, i: (t0 + g * nt + i, 0))],
        out_specs=pl.BlockSpec((1, t, d), lambda g, i: (g, i, 0)),
        out_shape=jax.ShapeDtypeStruct(x.shape, x.dtype),
        name="gated_residual",
    )(x, g, f)


def _inproj_kernel(x_ref, nw_ref, sc_ref, sh_ref, w_ref, a_ref, z_ref, xbc_ref, qkv_ref, sm_ref):
    u = _modulated_rms(x_ref[0], nw_ref[...], sc_ref[0], sh_ref[0])
    p = jnp.dot(u.astype(BF16), w_ref[...], preferred_element_type=F32)
    o = 0
    for ref in (a_ref, z_ref, xbc_ref, qkv_ref, sm_ref):
        n = ref.shape[1]
        ref[...] = p[:, o:o + n]
        o += n


def in_projection(x, norm_w, sc, sh, w_perm):
    G, R, d = x.shape
    t = min(R, ROW_TILE)
    nt = R // t
    widths = (A_IN, B_WIDTH, B_CONV_DIM, 3 * C_WIDTH, LANE)
    return pl.pallas_call(
        _inproj_kernel,
        grid=(G, nt),
        in_specs=[pl.BlockSpec((1, t, d), lambda g, i: (g, i, 0)),
                  pl.BlockSpec((1, d), lambda g, i: (0, 0)),
                  _mod_spec(sc, t), _mod_spec(sh, t),
                  pl.BlockSpec((d, IN_PAD), lambda g, i: (0, 0))],
        out_specs=[pl.BlockSpec((t, n), lambda g, i: (g * nt + i, 0)) for n in widths],
        out_shape=[jax.ShapeDtypeStruct((G * R, n), F32) for n in widths],
        compiler_params=pltpu.CompilerParams(dimension_semantics=("parallel", "parallel"),
                                             vmem_limit_bytes=VMEM_LIMIT),
        name="in_projection",
    )(x, norm_w.reshape(1, d), sc, sh, w_perm)


def _outproj_kernel(ya_ref, yb_ref, yc_ref, w_ref, x_ref, g_ref, nw_ref, sc_ref, sh_ref, rw_ref, rb_ref,
                    x1_ref, u2_ref, lg_ref):
    acc = jnp.dot(ya_ref[...].astype(BF16), w_ref[0:A_WIDTH, :], preferred_element_type=F32)
    acc += jnp.dot(yb_ref[...].astype(BF16), w_ref[A_WIDTH:A_WIDTH + B_WIDTH, :], preferred_element_type=F32)
    acc += jnp.dot(yc_ref[...].astype(BF16), w_ref[A_WIDTH + B_WIDTH:, :], preferred_element_type=F32)
    x1 = x_ref[0] + g_ref[0] * acc
    x1_ref[...] = x1
    u2 = _modulated_rms(x1, nw_ref[...], sc_ref[0], sh_ref[0])
    u2_ref[...] = u2
    lg_ref[...] = jnp.dot(u2.astype(BF16), rw_ref[...], preferred_element_type=F32) + rb_ref[...]


def out_projection(ya, yb, yc, w_out, x, g1, norm_w, sc, sh, router_w, router_b):
    G, R, d = x.shape
    t = min(R, ROW_TILE)
    nt = R // t
    rows = lambda n: pl.BlockSpec((t, n), lambda g, i: (g * nt + i, 0))
    full = lambda a: pl.BlockSpec(a.shape, lambda g, i: (0,) * a.ndim)
    nw = norm_w.reshape(1, d)
    return pl.pallas_call(
        _outproj_kernel,
        grid=(G, nt),
        in_specs=[rows(A_WIDTH), rows(B_WIDTH), rows(C_WIDTH), full(w_out),
                  pl.BlockSpec((1, t, d), lambda g, i: (g, i, 0)), _mod_spec(g1, t), full(nw),
                  _mod_spec(sc, t), _mod_spec(sh, t), full(router_w), full(router_b)],
        out_specs=[rows(d), rows(d), rows(LANE)],
        out_shape=[jax.ShapeDtypeStruct((G * R, d), F32), jax.ShapeDtypeStruct((G * R, d), F32),
                   jax.ShapeDtypeStruct((G * R, LANE), F32)],
        compiler_params=pltpu.CompilerParams(dimension_semantics=("parallel", "parallel"),
                                             vmem_limit_bytes=VMEM_LIMIT),
        name="out_projection",
    )(ya, yb, yc, w_out, x, g1, nw, sc, sh, router_w, router_b)


def _swiglu_kernel(te_ref, tv_ref, u_ref, wg_ref, wu_ref, wd_ref, o_ref, acc_ref):
    i, j = pl.program_id(0), pl.program_id(1)
    last = pl.num_programs(1) - 1

    @pl.when(tv_ref[i] == 1)
    def _():
        u = u_ref[...].astype(BF16)
        g = jnp.dot(u, wg_ref[0], preferred_element_type=F32)
        up = jnp.dot(u, wu_ref[0], preferred_element_type=F32)
        h = (g * jax.nn.sigmoid(g) * up).astype(BF16)
        part = jnp.dot(h, wd_ref[0], preferred_element_type=F32)

        @pl.when(j == 0)
        def _():
            acc_ref[...] = part

        @pl.when(j > 0)
        def _():
            acc_ref[...] += part

    @pl.when(j == last)
    def _():
        o_ref[...] = jnp.where(tv_ref[i] == 1, acc_ref[...], 0.0)


def grouped_swiglu(u, w_gate, w_up, w_down, tile_expert, tile_valid, ff_tile):
    P, d = u.shape
    F = w_gate.shape[2]
    t = FFN_ROW_TILE
    grid_spec = pltpu.PrefetchScalarGridSpec(
        num_scalar_prefetch=2,
        grid=(P // t, F // ff_tile),
        in_specs=[pl.BlockSpec((t, d), lambda i, j, te, tv: (i, 0)),
                  pl.BlockSpec((1, d, ff_tile), lambda i, j, te, tv: (te[i], 0, j)),
                  pl.BlockSpec((1, d, ff_tile), lambda i, j, te, tv: (te[i], 0, j)),
                  pl.BlockSpec((1, ff_tile, d), lambda i, j, te, tv: (te[i], j, 0))],
        out_specs=pl.BlockSpec((t, d), lambda i, j, te, tv: (i, 0)),
        scratch_shapes=[pltpu.VMEM((t, d), F32)])
    return pl.pallas_call(
        _swiglu_kernel,
        grid_spec=grid_spec,
        out_shape=jax.ShapeDtypeStruct((P, d), F32),
        compiler_params=pltpu.CompilerParams(dimension_semantics=("parallel", "arbitrary"),
                                             vmem_limit_bytes=VMEM_LIMIT),
        name="grouped_swiglu",
    )(tile_expert, tile_valid, u, w_gate, w_up, w_down)


def _fox_flash_kernel(q_ref, k_ref, v_ref, cq_ref, ck_ref, o_ref, m_scr, l_scr, acc_scr):
    qi, ki = pl.program_id(2), pl.program_id(3)
    tq, tk = q_ref.shape[2], k_ref.shape[2]

    @pl.when(ki == 0)
    def _():
        m_scr[...] = jnp.full(m_scr.shape, -jnp.inf, F32)
        l_scr[...] = jnp.zeros(l_scr.shape, F32)
        acc_scr[...] = jnp.zeros(acc_scr.shape, F32)

    @pl.when(ki * tk <= qi * tq + (tq - 1))
    def _():
        s = lax.dot_general(q_ref[0, 0], k_ref[0, 0], _NT, preferred_element_type=F32)
        s = s + cq_ref[0, 0] - ck_ref[0, 0]
        qpos = qi * tq + lax.broadcasted_iota(jnp.int32, (tq, tk), 0)
        kpos = ki * tk + lax.broadcasted_iota(jnp.int32, (tq, tk), 1)
        s = jnp.where(kpos <= qpos, s, -jnp.inf)
        m_prev = m_scr[...]
        m_new = jnp.maximum(m_prev, jnp.max(s, axis=-1, keepdims=True))
        alpha = jnp.exp(m_prev - m_new)
        p = jnp.exp(s - m_new)
        l_scr[...] = alpha * l_scr[...] + jnp.sum(p, axis=-1, keepdims=True)
        acc_scr[...] = alpha * acc_scr[...] + jnp.dot(p.astype(BF16), v_ref[0, 0], preferred_element_type=F32)
        m_scr[...] = m_new

    @pl.when(ki == pl.num_programs(3) - 1)
    def _():
        o_ref[0, 0] = acc_scr[...] / l_scr[...]


def fox_flash_attention(q, k, v, cq, ck):
    N, H, L, hd = q.shape
    t = min(L, ATT_TILE)
    nt = L // t
    kv_spec = pl.BlockSpec((1, 1, t, hd), lambda n, h, qi, ki: (n, h, jnp.minimum(ki, qi), 0))
    return pl.pallas_call(
        _fox_flash_kernel,
        grid=(N, H, nt, nt),
        in_specs=[pl.BlockSpec((1, 1, t, hd), lambda n, h, qi, ki: (n, h, qi, 0)), kv_spec, kv_spec,
                  pl.BlockSpec((1, 1, t, 1), lambda n, h, qi, ki: (n, h, qi, 0)),
                  pl.BlockSpec((1, 1, 1, t), lambda n, h, qi, ki: (n, h, 0, jnp.minimum(ki, qi)))],
        out_specs=pl.BlockSpec((1, 1, t, hd), lambda n, h, qi, ki: (n, h, qi, 0)),
        out_shape=jax.ShapeDtypeStruct((N, H, L, hd), F32),
        scratch_shapes=[pltpu.VMEM((t, 1), F32), pltpu.VMEM((t, 1), F32), pltpu.VMEM((t, hd), F32)],
        compiler_params=pltpu.CompilerParams(
            dimension_semantics=("parallel", "parallel", "parallel", "arbitrary"), vmem_limit_bytes=VMEM_LIMIT),
        name="fox_flash_attention",
    )(q, k, v, cq, ck)


def _split(t, sizes):
    return jnp.split(t, np.cumsum(sizes)[:-1].tolist(), axis=-1)


def rms_norm(x, w):
    xf = x.astype(jnp.float32)
    y = xf * lax.rsqrt(jnp.mean(xf * xf, axis=-1, keepdims=True) + NORM_EPS)
    return (y * w.astype(jnp.float32)).astype(x.dtype)


def head_rms(t, w):
    tf = t.astype(jnp.float32)
    return (tf * lax.rsqrt(jnp.mean(tf * tf, axis=-1, keepdims=True) + NORM_EPS) * w).astype(t.dtype)


def ada_modulation(c, w_ada, b_ada):
    m = jax.nn.silu(c) @ w_ada + b_ada
    return jnp.split(m[:, None, :], 6, axis=-1)


def swiglu(t, w_gate, w_up, w_down):
    return (jax.nn.silu(t @ w_gate) * (t @ w_up)) @ w_down


def dense_swiglu(u, w_gate, w_up, w_down, ff_tile):
    nt = u.shape[0] // FFN_ROW_TILE
    return grouped_swiglu(u, w_gate[None], w_up[None], w_down[None],
                          jnp.zeros((nt,), jnp.int32), jnp.ones((nt,), jnp.int32), ff_tile)


def moe_swiglu(u, logits, w_gate, w_up, w_down, ff_tile):
    T, d = u.shape
    E = logits.shape[1]
    t = FFN_ROW_TILE
    top_val, top_idx = lax.top_k(logits, TOP_K)
    weights = jax.nn.softmax(top_val, axis=-1)
    eid = top_idx.reshape(-1).astype(jnp.int32)
    n_pairs = T * TOP_K
    order = jnp.argsort(eid, stable=True).astype(jnp.int32)
    counts = jnp.sum(jax.nn.one_hot(eid, E, dtype=jnp.int32), axis=0)
    padded = ((counts + t - 1) // t) * t
    ends_p = jnp.cumsum(padded)
    start_p = ends_p - padded
    start = jnp.cumsum(counts) - counts
    P = n_pairs + E * t
    tile_start = jnp.arange(P // t, dtype=jnp.int32) * t
    tile_expert = jnp.minimum(jnp.searchsorted(ends_p, tile_start, side='right'), E - 1).astype(jnp.int32)
    tile_valid = (tile_start < ends_p[-1]).astype(jnp.int32)
    per_row = lambda table: jnp.repeat(table[tile_expert], t)
    row_off = jnp.arange(P, dtype=jnp.int32) - per_row(start_p)
    row_ok = row_off < per_row(counts)
    src_pair = order[jnp.clip(per_row(start) + row_off, 0, n_pairs - 1)]
    src_tok = jnp.where(row_ok, src_pair // TOP_K, 0)
    rank = jnp.argsort(order).astype(jnp.int32)
    shift_e = jnp.sum(jax.nn.one_hot(eid, E, dtype=jnp.int32) * (start_p - start)[None, :], axis=1)
    pos_of_pair = (rank + shift_e).reshape(T, TOP_K)
    ys = grouped_swiglu(u[src_tok], w_gate, w_up, w_down, tile_expert, tile_valid, ff_tile)
    out = weights[:, 0:1] * ys[pos_of_pair[:, 0]]
    for s in range(1, TOP_K):
        out = out + weights[:, s:s + 1] * ys[pos_of_pair[:, s]]
    return out


RWKV_CHUNK = 64
RWKV_BLOCK = 256
RWKV_INV_BLOCK = 16

_NN = (((1,), (0,)), ((), ()))
_NT = (((1,), (1,)), ((), ()))


def _dot(a, b, dims):
    return lax.dot_general(a, b, dims, preferred_element_type=jnp.float32)


_BNN = (((2,), (1,)), ((0,), (0,)))
_BNT = (((2,), (2,)), ((0,), (0,)))


def _sp(x):
    hi = x.astype(BF16)
    lo = (x - hi.astype(F32)).astype(BF16)
    return hi, lo


def _mmp(a, b, dims=_BNN):
    (ah, al), (bh, bl) = a, b
    return _dot(ah, bh, dims) + _dot(ah, bl, dims) + _dot(al, bh, dims)


def _mm_exact_lhs(a_bf16, x):
    h1 = x.astype(BF16)
    r1 = x - h1.astype(F32)
    h2 = r1.astype(BF16)
    h3 = (r1 - h2.astype(F32)).astype(BF16)
    return _dot(a_bf16, h1, _BNN) + _dot(a_bf16, h2, _BNN) + _dot(a_bf16, h3, _BNN)


def _unit_lower_inverse(a, blk_mask, eye):
    ad = jnp.where(blk_mask, a, 0.0)
    e = a - ad
    ad_p = _sp(ad)
    p2 = _mmp(ad_p, ad_p)
    p2_p = _sp(p2)
    p4 = _mmp(p2_p, p2_p)
    p4_p = _sp(p4)
    p8 = _mmp(p4_p, p4_p)
    x = _mmp(_sp(eye - ad), _sp(eye + p2))
    x = _mmp(_sp(x), _sp(eye + p4))
    dinv_p = _sp(_mmp(_sp(x), _sp(eye + p8)))
    n = _mmp(dinv_p, _sp(e))
    n_p = _sp(n)
    n2 = _mmp(n_p, n_p)
    return _mmp(_sp(_mmp(_sp(eye - n), _sp(eye + n2))), dinv_p)


def _rwkv_chunk_prep(r, lw, k, v, vt, kk, a):
    B, T, _ = r.shape
    row = lax.broadcasted_iota(jnp.int32, (T, T), 0)
    col = lax.broadcasted_iota(jnp.int32, (T, T), 1)
    strict, incl = row > col, row >= col
    blk = (row // RWKV_INV_BLOCK) == (col // RWKV_INV_BLOCK)
    eye = (row == col).astype(F32)
    c = _mm_exact_lhs(jnp.broadcast_to(incl.astype(BF16), (B, T, T)), lw)
    c_last = c[:, T - 1:T, :]
    e_nc = jnp.exp(-c)
    e_end = jnp.exp(c_last - c)
    b = kk * a
    kt = _sp(kk * jnp.exp(c - lw))
    rt = _sp(r * jnp.exp(c))
    kh = _sp(k * e_nc)
    bh = _sp(b * e_nc)
    v_p, vt_p = _sp(v), _sp(vt)
    a_kb = jnp.where(strict, _mmp(kt, bh, _BNT), 0.0)
    a_kk = _sp(jnp.where(strict, _mmp(kt, kh, _BNT), 0.0))
    a_rk = _sp(jnp.where(incl, _mmp(rt, kh, _BNT), 0.0))
    a_rb = _sp(jnp.where(incl, _mmp(rt, bh, _BNT), 0.0))
    m = _sp(_unit_lower_inverse(a_kb, blk, eye))
    mk = _sp(_mmp(m, kt))
    g1 = _mmp(m, _sp(_mmp(a_kk, v_p)))
    g1t = _mmp(_sp(_mmp(vt_p, a_kk, _BNT)), m, _BNT)
    y0 = _mmp(a_rk, v_p)
    s_add = _mmp(vt_p, _sp(k * e_end))
    return dict(mk=mk, g1=g1, g1t=g1t, rt=rt, a_rb=a_rb, y0=y0, s_add=s_add, bbar=_sp(b * e_end),
                decay=jnp.exp(c_last))


def _rwkv_chunk_step(S, q, lo, hi):
    pick = lambda x: tuple(t[lo:hi] for t in x) if isinstance(x, tuple) else x[lo:hi]
    S_p, mk = _sp(S), pick(q['mk'])
    u = _mmp(mk, S_p, _BNT) + pick(q['g1'])
    y = _mmp(pick(q['rt']), S_p, _BNT) + pick(q['y0']) - _mmp(pick(q['a_rb']), _sp(u))
    ut = _mmp(S_p, mk, _BNT) + pick(q['g1t'])
    s_new = S * pick(q['decay']) + pick(q['s_add']) - _mmp(_sp(ut), pick(q['bbar']))
    return y, s_new


def _rwkv_kernel(r_ref, lw_ref, k_ref, v_ref, vt_ref, kk_ref, a_ref, s0_ref, y_ref, sT_ref, s_scr):
    H = r_ref.shape[1]
    T = RWKV_CHUNK

    @pl.when(pl.program_id(1) == 0)
    def _():
        s_scr[...] = s0_ref[0]

    J = RWKV_BLOCK // T
    rows = lambda ref: jnp.concatenate([ref[0, :, j * T:(j + 1) * T, :] for j in range(J)], axis=0)
    vt = jnp.concatenate([vt_ref[0, :, :, j * T:(j + 1) * T] for j in range(J)], axis=0)
    q = _rwkv_chunk_prep(rows(r_ref), rows(lw_ref), rows(k_ref), rows(v_ref), vt, rows(kk_ref), rows(a_ref))
    S = s_scr[...]
    for j in range(J):
        y, S = _rwkv_chunk_step(S, q, j * H, (j + 1) * H)
        y_ref[0, :, j * T:(j + 1) * T, :] = y
    s_scr[...] = S
    sT_ref[0] = S


def rwkv7_recurrence_pallas(r, lw, k, v, kk, a, s0):
    N, L, H, D = r.shape
    hm = lambda t: jnp.transpose(t, (0, 2, 1, 3))
    vt = jnp.transpose(v, (0, 2, 3, 1))
    row_spec = pl.BlockSpec((1, H, RWKV_BLOCK, D), lambda n, c: (n, 0, c, 0))
    st_spec = pl.BlockSpec((1, H, D, D), lambda n, c: (n, 0, 0, 0))
    y, s_last = pl.pallas_call(
        _rwkv_kernel,
        grid=(N, L // RWKV_BLOCK),
        in_specs=[row_spec, row_spec, row_spec, row_spec,
                  pl.BlockSpec((1, H, D, RWKV_BLOCK), lambda n, c: (n, 0, 0, c)),
                  row_spec, row_spec, st_spec],
        out_specs=[row_spec, st_spec],
        out_shape=[jax.ShapeDtypeStruct((N, H, L, D), jnp.float32),
                   jax.ShapeDtypeStruct((N, H, D, D), jnp.float32)],
        scratch_shapes=[pltpu.VMEM((H, D, D), jnp.float32)],
        compiler_params=pltpu.CompilerParams(dimension_semantics=("parallel", "arbitrary")),
        name="rwkv7_chunked",
    )(hm(r), hm(lw), hm(k), hm(v), vt, hm(kk), hm(a), s0)
    return jnp.transpose(y, (0, 2, 1, 3)), s_last


def rwkv7_recurrence(r, decay, k, v, kk, a, s0):
    def step(s, inp):
        r_t, w_t, k_t, v_t, kk_t, a_t = inp
        s_kk = jnp.einsum('nhvk,nhk->nhv', s, kk_t)
        s = (s * w_t[:, :, None, :]
             - s_kk[..., None] * (kk_t * a_t)[:, :, None, :]
             + v_t[..., None] * k_t[:, :, None, :])
        return s, jnp.einsum('nhvk,nhk->nhv', s, r_t)
    seq_first = tuple(jnp.moveaxis(t, 1, 0) for t in (r, decay, k, v, kk, a))
    s_last, ys = lax.scan(step, s0, seq_first)
    return jnp.moveaxis(ys, 0, 1), s_last


def rwkv7_mixer(cols, shift0, s0, mu, w0, w2, a0, a2, g2, k_k, k_a, r_k, ln_w, ln_b):
    n, L, _ = cols.shape
    f32 = jnp.float32
    prev = jnp.concatenate([shift0[:, None, :].astype(cols.dtype), cols[:, :-1]], axis=1)
    xs = cols + (prev - cols) * mu
    r, k, v, xw, xa, xg = _split(xs, [A_WIDTH, A_WIDTH, A_WIDTH, A_DECAY_LORA, A_ICL_LORA, A_GATE_LORA])
    w = -jax.nn.softplus(-(w0 + jnp.tanh(xw) @ w2)) - 0.5
    decay = jnp.exp(-jnp.exp(w.astype(f32)))
    a = jax.nn.sigmoid(a0 + xa @ a2)
    g = jax.nn.sigmoid(xg) @ g2
    heads = lambda t: t.astype(f32).reshape(n, L, A_HEADS, HEAD_DIM)
    kk = heads(k * k_k)
    kk = kk / jnp.maximum(jnp.sqrt(jnp.sum(kk * kk, axis=-1, keepdims=True)), 1e-12)
    k = k * (1.0 + (a - 1.0) * k_a)
    r_h, k_h, v_h = heads(r), heads(k), heads(v)
    if L % RWKV_BLOCK == 0:
        y, s_last = rwkv7_recurrence_pallas(r_h, heads(-jnp.exp(w.astype(f32))), k_h, v_h, kk, heads(a),
                                            s0.astype(f32))
    else:
        y, s_last = rwkv7_recurrence(r_h, heads(decay), k_h, v_h, kk, heads(a), s0.astype(f32))
    mean = jnp.mean(y, axis=-1, keepdims=True)
    var = jnp.mean(jnp.square(y - mean), axis=-1, keepdims=True)
    y = ((y - mean) * lax.rsqrt(var + GN_EPS)).reshape(n, L, A_WIDTH) * ln_w + ln_b
    bonus = jnp.sum(r_h * k_h * r_k, axis=-1, keepdims=True) * v_h
    y = (y + bonus.reshape(n, L, A_WIDTH)) * g
    return y.astype(cols.dtype), s_last, cols[:, -1]


def ssd_chunked(x, dt, A, B, C, h0):
    N, L, H, P = x.shape
    q = min(SSD_CHUNK, L)
    pad = (-L) % q
    if pad:
        padl = lambda t: jnp.pad(t, [(0, 0), (0, pad)] + [(0, 0)] * (t.ndim - 2))
        x, dt, B, C = padl(x), padl(dt), padl(B), padl(C)
    nc = (L + pad) // q
    ch = lambda t: t.reshape((N, nc, q) + t.shape[2:])
    xc, dtc, Bc, Cc = ch(x), ch(dt), ch(B), ch(C)
    cs = jnp.cumsum(dtc * A, axis=2)
    causal = jnp.tril(jnp.ones((q, q), bool))[None, None, :, :, None]
    seg = cs[:, :, :, None, :] - cs[:, :, None, :, :]
    decay = jnp.exp(jnp.where(causal, seg, -jnp.inf))
    scores = jnp.einsum('nclhd,ncshd->nclsh', Cc, Bc) * decay * dtc[:, :, None, :, :]
    y_intra = jnp.einsum('nclsh,ncshp->nclhp', scores, xc)
    to_end = jnp.exp(cs[:, :, -1:, :] - cs) * dtc
    chunk_states = jnp.einsum('ncsh,ncshd,ncshp->nchpd', to_end, Bc, xc)
    chunk_decay = jnp.exp(cs[:, :, -1, :])
    def carry(h, inp):
        dec, st = inp
        return h * dec[:, :, None, None] + st, h
    h_last, h_in = lax.scan(carry, h0, (jnp.moveaxis(chunk_decay, 1, 0), jnp.moveaxis(chunk_states, 1, 0)))
    h_in = jnp.moveaxis(h_in, 0, 1)
    y_inter = jnp.einsum('nclhd,nchpd->nclhp', Cc, h_in) * jnp.exp(cs)[..., None]
    y = (y_intra + y_inter).reshape(N, nc * q, H, P)[:, :L]
    return y, h_last


def mamba2_mixer(z, xbc, dt_raw, conv0, ssm0, conv_w, conv_b, dt_bias, a_log, d_skip, norm_w):
    n, L, _ = xbc.shape
    f32 = jnp.float32
    xpad = jnp.concatenate([conv0.astype(xbc.dtype), xbc], axis=1)
    conv = lax.conv_general_dilated(xpad, conv_w[:, None, :], window_strides=(1,), padding='VALID',
                                    dimension_numbers=('NWC', 'WIO', 'NWC'),
                                    feature_group_count=B_CONV_DIM) + conv_b
    xs, Bm, Cm = _split(jax.nn.silu(conv), [B_WIDTH, B_GROUPS * D_STATE, B_GROUPS * D_STATE])
    xh = xs.astype(f32).reshape(n, L, B_HEADS, HEAD_DIM)
    rep = B_HEADS // B_GROUPS
    Bh = jnp.repeat(Bm.astype(f32).reshape(n, L, B_GROUPS, D_STATE), rep, axis=2)
    Ch = jnp.repeat(Cm.astype(f32).reshape(n, L, B_GROUPS, D_STATE), rep, axis=2)
    dt = jax.nn.softplus(dt_raw.astype(f32) + dt_bias)
    A = -jnp.exp(a_log.astype(f32))
    y, h_last = ssd_chunked(xh, dt, A, Bh, Ch, ssm0.astype(f32))
    y = (y + d_skip[:, None] * xh).reshape(n, L, B_WIDTH) * jax.nn.silu(z.astype(f32))
    yg = y.reshape(n, L, B_GROUPS, B_WIDTH // B_GROUPS)
    yg = yg * lax.rsqrt(jnp.mean(yg * yg, axis=-1, keepdims=True) + NORM_EPS)
    y = yg.reshape(n, L, B_WIDTH) * norm_w
    return y.astype(z.dtype), h_last, xpad[:, -(CONV_W - 1):]


SSD_PAIRS = B_HEADS // 2
CONV_TAIL = 8


def _exact_rhs_mm(x, sel_bf16):
    h1 = x.astype(BF16)
    r1 = x - h1.astype(F32)
    h2 = r1.astype(BF16)
    h3 = (r1 - h2.astype(F32)).astype(BF16)
    d = lambda a: jnp.dot(a, sel_bf16, preferred_element_type=F32)
    return d(h1) + d(h2) + d(h3)


def _ssd_kernel(xbc_ref, z_ref, sm_ref, conv0_ref, h0_ref, cw_ref, cb_ref, dtb_ref, a_ref, dskip_ref, nw_ref,
                y_ref, hT_ref, xbuf, h_scr):
    Q = SSD_CHUNK
    c = pl.program_id(1)

    @pl.when(c == 0)
    def _():
        xbuf[0:CONV_TAIL, :] = conv0_ref[0]
        h_scr[...] = h0_ref[0]

    @pl.when(c > 0)
    def _():
        xbuf[0:CONV_TAIL, :] = xbuf[Q:Q + CONV_TAIL, :]

    xbuf[CONV_TAIL:CONV_TAIL + Q, :] = xbc_ref[0]
    conv = cb_ref[...]
    for j in range(CONV_W):
        conv = conv + cw_ref[j:j + 1, :] * xbuf[pl.ds(CONV_TAIL - (CONV_W - 1) + j, Q), :]
    act = conv * jax.nn.sigmoid(conv)
    xs = act[:, :B_WIDTH]
    n_bc = B_GROUPS * D_STATE
    Bm, Cm = act[:, B_WIDTH:B_WIDTH + n_bc], act[:, B_WIDTH + n_bc:]

    pre = sm_ref[0] + dtb_ref[...]
    dt = jnp.maximum(pre, 0.0) + jnp.log(1.0 + jnp.exp(-jnp.abs(pre)))
    dA = dt * a_ref[...]
    row = lax.broadcasted_iota(jnp.int32, (Q, Q), 0)
    col = lax.broadcasted_iota(jnp.int32, (Q, Q), 1)
    causal = row >= col
    tri = causal.astype(BF16)
    h1 = dA.astype(BF16)
    r1 = dA - h1.astype(F32)
    h2 = r1.astype(BF16)
    h3 = (r1 - h2.astype(F32)).astype(BF16)
    cs = (jnp.dot(tri, h1, preferred_element_type=F32) + jnp.dot(tri, h2, preferred_element_type=F32)
          + jnp.dot(tri, h3, preferred_element_type=F32))
    csT, dtT = cs.T, dt.T
    sel_b = (lax.broadcasted_iota(jnp.int32, (LANE, B_WIDTH), 0)
             == lax.broadcasted_iota(jnp.int32, (LANE, B_WIDTH), 1) // HEAD_DIM).astype(BF16)
    sel_h = (lax.broadcasted_iota(jnp.int32, (LANE, B_HEADS * Q), 0)
             == lax.broadcasted_iota(jnp.int32, (LANE, B_HEADS * Q), 1) // Q).astype(BF16)
    cs_b = _exact_rhs_mm(cs, sel_b)
    dt_b = _exact_rhs_mm(dt, sel_b)
    cs_full = _exact_rhs_mm(cs, sel_h)
    cs_last_b = cs_b[Q - 1:Q, :]
    e_b = jnp.exp(cs_b)
    xs_bf = xs.astype(BF16)
    xs_te = (xs * (jnp.exp(cs_last_b - cs_b) * dt_b)).astype(BF16)
    first_half = lax.broadcasted_iota(jnp.int32, (Q, LANE), 1) < HEAD_DIM
    ys = []
    for g in range(B_GROUPS):
        gs = slice(g * D_STATE, (g + 1) * D_STATE)
        Cg, Bg = Cm[:, gs].astype(BF16), Bm[:, gs]
        cb = lax.dot_general(Cg, Bg.astype(BF16), _NT, preferred_element_type=F32)
        BTg = Bg.T.astype(BF16)
        for pp in range(SSD_PAIRS // B_GROUPS):
            pair = g * (SSD_PAIRS // B_GROUPS) + pp
            lanes = slice(pair * LANE, (pair + 1) * LANE)
            hT = h_scr[pair]
            y_pair = jnp.dot(Cg, hT.astype(BF16), preferred_element_type=F32) * e_b[:, lanes]
            for j in range(2):
                h = 2 * pair + j
                seg = cs_full[:, h * Q:(h + 1) * Q] - csT[h:h + 1, :]
                dec = jnp.exp(jnp.where(causal, seg, -jnp.inf))
                sc = (cb * dec * dtT[h:h + 1, :]).astype(BF16)
                xm = jnp.where(first_half if j == 0 else jnp.logical_not(first_half), xs_bf[:, lanes], 0.0)
                y_pair = y_pair + jnp.dot(sc, xm.astype(BF16), preferred_element_type=F32)
            h_scr[pair] = hT * jnp.exp(cs_last_b[:, lanes]) + jnp.dot(BTg, xs_te[:, lanes],
                                                                      preferred_element_type=F32)
            ys.append(y_pair)
    y = jnp.concatenate(ys, axis=-1)
    z = z_ref[0]
    y = (y + dskip_ref[...] * xs) * (z * jax.nn.sigmoid(z))
    gw = B_WIDTH // B_GROUPS
    outs = []
    for g in range(B_GROUPS):
        yg = y[:, g * gw:(g + 1) * gw]
        outs.append(yg * lax.rsqrt(jnp.mean(yg * yg, axis=-1, keepdims=True) + NORM_EPS))
    y_ref[0] = jnp.concatenate(outs, axis=-1) * nw_ref[...]
    hT_ref[0] = h_scr[...]


def mamba2_mixer_pallas(z, xbc, small, conv0, ssm0, conv_w, conv_b, dt_bias, a_log, d_skip, norm_w):
    n, L, _ = xbc.shape
    Q = SSD_CHUNK
    pad_l = lambda v: jnp.pad(v.astype(F32), (0, LANE - v.shape[0]))[None, :]
    tail0 = jnp.pad(conv0.astype(F32), ((0, 0), (CONV_TAIL - (CONV_W - 1), 0), (0, 0)))
    h0 = ssm0.astype(F32).reshape(n, SSD_PAIRS, 2, HEAD_DIM, D_STATE)
    h0 = jnp.transpose(h0, (0, 1, 4, 2, 3)).reshape(n, SSD_PAIRS, D_STATE, 2 * HEAD_DIM)
    full = lambda a: pl.BlockSpec(a.shape, lambda i, c: (0,) * a.ndim)
    seq = lambda w: pl.BlockSpec((1, Q, w), lambda i, c: (i, c, 0))
    args = (xbc, z, small, tail0, h0, conv_w.astype(F32), conv_b.astype(F32)[None, :], pad_l(dt_bias),
            pad_l(-jnp.exp(a_log.astype(F32))), jnp.repeat(d_skip.astype(F32), HEAD_DIM)[None, :],
            norm_w.astype(F32)[None, :])
    st_spec = pl.BlockSpec((1, SSD_PAIRS, D_STATE, 2 * HEAD_DIM), lambda i, c: (i, 0, 0, 0))
    y, hT = pl.pallas_call(
        _ssd_kernel,
        grid=(n, L // Q),
        in_specs=[seq(B_CONV_DIM), seq(B_WIDTH), seq(LANE),
                  pl.BlockSpec((1, CONV_TAIL, B_CONV_DIM), lambda i, c: (i, 0, 0)), st_spec]
                 + [full(a) for a in args[5:]],
        out_specs=[seq(B_WIDTH), st_spec],
        out_shape=[jax.ShapeDtypeStruct((n, L, B_WIDTH), F32),
                   jax.ShapeDtypeStruct((n, SSD_PAIRS, D_STATE, 2 * HEAD_DIM), F32)],
        scratch_shapes=[pltpu.VMEM((Q + CONV_TAIL, B_CONV_DIM), F32),
                        pltpu.VMEM((SSD_PAIRS, D_STATE, 2 * HEAD_DIM), F32)],
        compiler_params=pltpu.CompilerParams(dimension_semantics=("parallel", "arbitrary"),
                                             vmem_limit_bytes=VMEM_LIMIT),
        name="mamba2_ssd",
    )(*args)
    h_last = jnp.transpose(hT.reshape(n, SSD_PAIRS, D_STATE, 2, HEAD_DIM), (0, 1, 3, 4, 2))
    h_last = h_last.reshape(n, B_HEADS, HEAD_DIM, D_STATE)
    tail = jnp.concatenate([conv0.astype(xbc.dtype), xbc[:, -(CONV_W - 1):]], axis=1)[:, -(CONV_W - 1):]
    return y, h_last, tail


def fox_attention(q, k, v, logf, past):
    n, L, H, hd = q.shape
    f32 = jnp.float32
    cum = jnp.cumsum(logf, axis=1)
    if past is None:
        P = 0
        keys_k, keys_v, key_c = k, v, cum
    else:
        pool_k, pool_v, pool_logf, page_table = past
        P = page_table.shape[1] * PAGE_SIZE
        gather = lambda pool: pool[page_table].reshape((n, P) + pool.shape[2:])
        plf = gather(pool_logf).astype(f32)
        suffix = jnp.cumsum(plf[:, ::-1], axis=1)[:, ::-1] - plf
        keys_k = jnp.concatenate([gather(pool_k).astype(k.dtype), k], axis=1)
        keys_v = jnp.concatenate([gather(pool_v).astype(v.dtype), v], axis=1)
        key_c = jnp.concatenate([-suffix, cum], axis=1)
    kc = jnp.moveaxis(key_c, 2, 1)
    kpos = jnp.arange(P + L)
    qb = Q_BLOCK if L % Q_BLOCK == 0 else L
    nb = L // qb
    scale = hd ** -0.5

    def block(args):
        q_blk, c_blk, t0 = args
        s = jnp.einsum('nqhd,nkhd->nhqk', q_blk, keys_k, preferred_element_type=f32) * scale
        s = s + jnp.moveaxis(c_blk, 2, 1)[..., None] - kc[:, :, None, :]
        mask = kpos[None, :] <= (t0 + jnp.arange(qb))[:, None]
        prob = jax.nn.softmax(jnp.where(mask, s, -jnp.inf), axis=-1)
        return jnp.einsum('nhqk,nkhd->nqhd', prob.astype(keys_v.dtype), keys_v)

    q_blocks = jnp.moveaxis(q.reshape(n, nb, qb, H, hd), 1, 0)
    c_blocks = jnp.moveaxis(cum.reshape(n, nb, qb, H), 1, 0)
    starts = P + jnp.arange(nb) * qb
    out = lax.map(block, (q_blocks, c_blocks, starts))
    return jnp.moveaxis(out, 0, 1).reshape(n, L, H, hd)


def fox_mixer(q, k, v, f_raw, f_bias, qn_w, kn_w, past):
    n, L, _ = q.shape
    qh = head_rms(q.reshape(n, L, C_HEADS, HEAD_DIM), qn_w)
    kh = head_rms(k.reshape(n, L, C_HEADS, HEAD_DIM), kn_w)
    vh = v.reshape(n, L, C_HEADS, HEAD_DIM)
    logf = jax.nn.log_sigmoid(f_raw.astype(jnp.float32) + f_bias)
    if past is None and L % ATT_TILE == 0:
        cum = jnp.transpose(jnp.cumsum(logf, axis=1), (0, 2, 1))
        hm = lambda t: jnp.transpose(t, (0, 2, 1, 3)).astype(BF16)
        o = fox_flash_attention(hm(qh * (HEAD_DIM ** -0.5)), hm(kh), hm(vh), cum[..., None], cum[:, :, None, :])
        o = jnp.transpose(o, (0, 2, 1, 3))
    else:
        o = fox_attention(qh, kh, vh, logf, past)
    return o.reshape(n, L, C_WIDTH).astype(q.dtype), kh, vh, logf


def mix_group(x, c, l, p, s0, shift0, ssm0, conv0, past):
    n, L, d = x.shape
    mods = ada_modulation(c, p['w_ada'][l], p['b_ada'][l])
    if L >= ROW_TILE:
        xg = x
    else:
        xg = x.reshape(1, n * L, d)
        mods = [jnp.broadcast_to(m, (n, L, d)).reshape(1, n * L, d) for m in mods]
    sh1, sc1, g1, sh2, sc2, g2 = mods
    a_cols, b_z, b_xbc, c_qkv, small = in_projection(xg, p['norm1_w'][l], sc1, sh1, p['w_in_perm'][l])
    rs = lambda t: t.reshape(n, L, t.shape[-1])
    a_cols, b_z, b_xbc = rs(a_cols), rs(b_z), rs(b_xbc)
    b_dt, c_f = rs(small[:, :B_HEADS]), rs(small[:, B_HEADS:B_HEADS + C_HEADS])
    c_q, c_k, c_v = (rs(c_qkv[:, i * C_WIDTH:(i + 1) * C_WIDTH]) for i in range(3))
    y_a, s_new, shift_new = rwkv7_mixer(
        a_cols, shift0, s0, p['a_mu'][l], p['a_w0'][l], p['a_w2'][l], p['a_a0'][l], p['a_a2'][l],
        p['a_g2'][l], p['a_kk'][l], p['a_ka'][l], p['a_rk'][l], p['a_ln_w'][l], p['a_ln_b'][l])
    mamba = mamba2_mixer_pallas if L % SSD_CHUNK == 0 else mamba2_mixer
    y_b, ssm_new, conv_new = mamba(
        b_z, b_xbc, rs(small) if L % SSD_CHUNK == 0 else b_dt, conv0, ssm0, p['b_conv_w'][l], p['b_conv_b'][l],
        p['b_dt_bias'][l], p['b_a_log'][l], p['b_d'][l], p['b_norm_w'][l])
    fox_past = None if past is None else (past[0][l], past[1][l], past[2][l], past[3])
    y_c, k_rows, v_rows, logf_rows = fox_mixer(
        c_q, c_k, c_v, c_f, p['c_f_bias'][l], p['c_qnorm_w'][l], p['c_knorm_w'][l], fox_past)
    fl = lambda t: t.reshape(n * L, t.shape[-1])
    x1, u2, logits = out_projection(fl(y_a), fl(y_b), fl(y_c), p['w_out_bf16'][l], xg, g1, p['norm2_w'][l],
                                    sc2, sh2, p['router_w_pad'][l // 2], p['router_b_pad'][l // 2])
    dt = x.dtype
    states = (s_new.astype(dt), shift_new.astype(dt), ssm_new.astype(dt), conv_new.astype(dt),
              k_rows.astype(dt), v_rows.astype(dt), logf_rows.astype(dt))
    return x1.reshape(xg.shape), u2, logits, g2, states


def channel_mix(l, p, x1s, u2s, logits, g2s):
    u2 = jnp.concatenate(u2s, axis=0)
    j = l // 2
    if l % 2 == 0:
        f = dense_swiglu(u2, p['ffn_w_gate'][j], p['ffn_w_up'][j], p['ffn_w_down'][j], FFN_TILE_DENSE)
    else:
        lg = jnp.concatenate(logits, axis=0)[:, :N_EXPERTS]
        f = moe_swiglu(u2, lg, p['moe_w_gate'][j], p['moe_w_up'][j], p['moe_w_down'][j], FFN_TILE_EXPERT)
    outs, row0 = [], 0
    for x1, g2 in zip(x1s, g2s):
        outs.append(gated_residual(x1, g2, f, row0))
        row0 += x1.shape[0] * x1.shape[1]
    return outs


def run_trunk(xs, cs, p, init_states, pasts):
    outs = [[] for _ in xs]
    shapes = [x.shape for x in xs]
    for l in range(DEPTH):
        halves = [mix_group(x.reshape(s), c, l, p, *st[l], past)
                  for x, s, c, st, past in zip(xs, shapes, cs, init_states, pasts)]
        for o, h in zip(outs, halves):
            o.append(h[4])
        xs = channel_mix(l, p, [h[0] for h in halves], [h[1] for h in halves], [h[2] for h in halves],
                         [h[3] for h in halves])
    stacked = [[jnp.stack([o[i] for o in og]) for i in range(7)] for og in outs]
    return [x.reshape(s) for x, s in zip(xs, shapes)], stacked


def kernel(x_prompt, x_sample, cache_k, cache_v, cache_logf, state_rwkv, state_shift, state_ssm,
           state_conv, page_table, c_prompt, c_sample, norm1_w, norm2_w, w_ada, b_ada, w_in, w_out,
           a_mu, a_w0, a_w2, a_a0, a_a2, a_g2, a_kk, a_ka, a_rk, a_ln_w, a_ln_b,
           b_conv_w, b_conv_b, b_dt_bias, b_a_log, b_d, b_norm_w,
           c_f_bias, c_qnorm_w, c_knorm_w, ffn_w_gate, ffn_w_up, ffn_w_down,
           moe_router_w, moe_router_b, moe_w_gate, moe_w_up, moe_w_down):
    p = dict(norm1_w=norm1_w, norm2_w=norm2_w, w_ada=w_ada, b_ada=b_ada, w_in=w_in, w_out=w_out,
             a_mu=a_mu, a_w0=a_w0, a_w2=a_w2, a_a0=a_a0, a_a2=a_a2, a_g2=a_g2, a_kk=a_kk, a_ka=a_ka,
             a_rk=a_rk, a_ln_w=a_ln_w, a_ln_b=a_ln_b, b_conv_w=b_conv_w, b_conv_b=b_conv_b,
             b_dt_bias=b_dt_bias, b_a_log=b_a_log, b_d=b_d, b_norm_w=b_norm_w, c_f_bias=c_f_bias,
             c_qnorm_w=c_qnorm_w, c_knorm_w=c_knorm_w, ffn_w_gate=ffn_w_gate, ffn_w_up=ffn_w_up,
             ffn_w_down=ffn_w_down, moe_router_w=moe_router_w, moe_router_b=moe_router_b,
             moe_w_gate=moe_w_gate, moe_w_up=moe_w_up, moe_w_down=moe_w_down)
    o_dt, o_q, o_f = A_IN + B_WIDTH + B_CONV_DIM, A_IN + B_IN, A_IN + B_IN + 3 * C_WIDTH
    p['w_in_perm'] = jnp.concatenate(
        [w_in[:, :, :o_dt], w_in[:, :, o_q:o_f], w_in[:, :, o_dt:o_q], w_in[:, :, o_f:],
         jnp.zeros((DEPTH, D_MODEL, LANE - B_HEADS - C_HEADS), w_in.dtype)], axis=-1).astype(BF16)
    p['w_out_bf16'] = w_out.astype(BF16)
    p['router_w_pad'] = jnp.pad(moe_router_w, ((0, 0), (0, 0), (0, LANE - N_EXPERTS))).astype(BF16)
    p['router_b_pad'] = jnp.pad(moe_router_b.astype(F32), ((0, 0), (0, LANE - N_EXPERTS)))[:, None, :]
    for name in ('ffn_w_gate', 'ffn_w_up', 'ffn_w_down', 'moe_w_gate', 'moe_w_up', 'moe_w_down'):
        p[name] = p[name].astype(BF16)
    n_p = x_prompt.shape[0]
    zero_state = (jnp.zeros((n_p, A_HEADS, HEAD_DIM, HEAD_DIM), jnp.float32),
                  jnp.zeros((n_p, A_IN), x_prompt.dtype),
                  jnp.zeros((n_p, B_HEADS, HEAD_DIM, D_STATE), jnp.float32),
                  jnp.zeros((n_p, CONV_W - 1, B_CONV_DIM), x_prompt.dtype))
    init_s = [(state_rwkv[l], state_shift[l], state_ssm[l], state_conv[l]) for l in range(DEPTH)]
    (y_prompt, y_sample), (st_p, st_s) = run_trunk(
        [x_prompt, x_sample], [c_prompt, c_sample], p, [[zero_state] * DEPTH, init_s],
        [None, (cache_k, cache_v, cache_logf, page_table)])
    rwkv_prompt, shift_prompt, ssm_prompt, conv_prompt, k_prompt, v_prompt, logf_prompt = st_p
    rwkv_sample, shift_sample, ssm_sample, conv_sample, k_sample, v_sample, logf_sample = st_s
    return (y_prompt, y_sample,
            k_prompt, v_prompt, logf_prompt, rwkv_prompt, shift_prompt, ssm_prompt, conv_prompt,
            k_sample, v_sample, logf_sample, rwkv_sample, shift_sample, ssm_sample, conv_sample)
```

```python
import math
import numpy as np
import jax
import jax.numpy as jnp
import functools
from jax import lax
from jax.experimental import pallas as pl
from jax.experimental.pallas import tpu as pltpu

D_MODEL = 1024
BATCH = 4
SEQ = 4096
DEPTH = 2
DEC_BATCH = 128
DEC_SEQ = 4
PAST_LEN = 2048
PAGE_SIZE = 128

D_MIX = D_MODEL
HEAD_DIM = 64
A_WIDTH = D_MIX // 4
A_HEADS = A_WIDTH // HEAD_DIM
A_DECAY_LORA = 64
A_ICL_LORA = 64
A_GATE_LORA = 128
A_IN = 3 * A_WIDTH + A_DECAY_LORA + A_ICL_LORA + A_GATE_LORA
GN_EPS = HEAD_DIM * 1e-5
B_WIDTH = D_MIX // 2
B_HEADS = B_WIDTH // HEAD_DIM
B_GROUPS = 2
D_STATE = 128
CONV_W = 4
SSD_CHUNK = 128
B_CONV_DIM = B_WIDTH + 2 * B_GROUPS * D_STATE
B_IN = B_WIDTH + B_CONV_DIM + B_HEADS
C_WIDTH = D_MIX - A_WIDTH - B_WIDTH
C_HEADS = C_WIDTH // HEAD_DIM
C_IN = 3 * C_WIDTH + C_HEADS
Q_BLOCK = 128
IN_WIDTH = A_IN + B_IN + C_IN
D_FF = 2816
N_EXPERTS = 8
TOP_K = 2
D_FF_EXPERT = 3584
N_DENSE = (DEPTH + 1) // 2
N_MOE = DEPTH // 2
NORM_EPS = 1e-6


BF16 = jnp.bfloat16
F32 = jnp.float32
LANE = 128
VMEM_LIMIT = 48 * 1024 * 1024
ROW_TILE = 256
FFN_ROW_TILE = 512
FFN_TILE_DENSE = D_FF // 2
FFN_TILE_EXPERT = D_FF_EXPERT // 4
ATT_TILE = 1024
CAST_BLOCK_ELEMS = 1024 * 1024
IN_MAIN = A_IN + B_WIDTH + B_CONV_DIM + 3 * C_WIDTH
IN_PAD = IN_MAIN + LANE


def _mod_spec(mod, tile):
    if mod.shape[1] == 1:
        return pl.BlockSpec((1, 1, mod.shape[2]), lambda g, i: (g, 0, 0))
    return pl.BlockSpec((1, tile, mod.shape[2]), lambda g, i: (g, i, 0))


def _modulated_rms(x, nw, sc, sh):
    y = x * lax.rsqrt(jnp.mean(x * x, axis=-1, keepdims=True) + NORM_EPS) * nw
    return y * (1.0 + sc) + sh


def _cast_kernel(x_ref, o_ref):
    o_ref[...] = x_ref[...].astype(o_ref.dtype)


def cast_bf16(w):
    E, A, B = w.shape
    ta = next(t for t in (1024, 512, 256, 128, 64, 32, 16) if t * B <= CAST_BLOCK_ELEMS and A % t == 0)
    spec = pl.BlockSpec((1, ta, B), lambda e, i: (e, i, 0))
    return pl.pallas_call(
        _cast_kernel, grid=(E, A // ta), in_specs=[spec], out_specs=spec,
        out_shape=jax.ShapeDtypeStruct(w.shape, BF16),
        compiler_params=pltpu.CompilerParams(dimension_semantics=("parallel", "parallel")),
        name="cast_bf16",
    )(w)


def _gated_residual_kernel(x_ref, g_ref, f_ref, o_ref):
    o_ref[0] = x_ref[0] + g_ref[0] * f_ref[...]


def gated_residual(x, g, f, row0):
    G, R, d = x.shape
    t = min(R, 512)
    nt = R // t
    t0 = row0 // t
    return pl.pallas_call(
        _gated_residual_kernel,
        grid=(G, nt),
        in_specs=[pl.BlockSpec((1, t, d), lambda g, i: (g, i, 0)),
                  _mod_spec(g, t),
                  pl.BlockSpec((t, d), lambda g, i: (t0 + g * nt + i, 0))],
        out_specs=pl.BlockSpec((1, t, d), lambda g, i: (g, i, 0)),
        out_shape=jax.ShapeDtypeStruct(x.shape, x.dtype),
        name="gated_residual",
    )(x, g, f)


def _inproj_kernel(x_ref, nw_ref, sc_ref, sh_ref, w_ref, *out_refs):
    u = _modulated_rms(x_ref[0], nw_ref[...], sc_ref[0], sh_ref[0])
    p = jnp.dot(u.astype(BF16), w_ref[...], preferred_element_type=F32)
    o = 0
    for ref in out_refs:
        n = ref.shape[1]
        ref[...] = p[:, o:o + n]
        o += n


def in_projection(x, norm_w, sc, sh, w_perm):
    G, R, d = x.shape
    t = min(R, ROW_TILE)
    nt = R // t
    widths = (A_IN, B_WIDTH, B_CONV_DIM, C_WIDTH, C_WIDTH, C_WIDTH, LANE)
    return pl.pallas_call(
        _inproj_kernel,
        grid=(G, nt),
        in_specs=[pl.BlockSpec((1, t, d), lambda g, i: (g, i, 0)),
                  pl.BlockSpec((1, d), lambda g, i: (0, 0)),
                  _mod_spec(sc, t), _mod_spec(sh, t),
                  pl.BlockSpec((d, IN_PAD), lambda g, i: (0, 0))],
        out_specs=[pl.BlockSpec((t, n), lambda g, i: (g * nt + i, 0)) for n in widths],
        out_shape=[jax.ShapeDtypeStruct((G * R, n), F32) for n in widths],
        compiler_params=pltpu.CompilerParams(dimension_semantics=("parallel", "parallel"),
                                             vmem_limit_bytes=VMEM_LIMIT),
        name="in_projection",
    )(x, norm_w.reshape(1, d), sc, sh, w_perm)


def _outproj_kernel(ya_ref, yb_ref, yc_ref, w_ref, x_ref, g_ref, nw_ref, sc_ref, sh_ref, rw_ref, rb_ref,
                    x1_ref, u2_ref, lg_ref):
    acc = jnp.dot(ya_ref[...].astype(BF16), w_ref[0:A_WIDTH, :], preferred_element_type=F32)
    acc += jnp.dot(yb_ref[...].astype(BF16), w_ref[A_WIDTH:A_WIDTH + B_WIDTH, :], preferred_element_type=F32)
    acc += jnp.dot(yc_ref[...].astype(BF16), w_ref[A_WIDTH + B_WIDTH:, :], preferred_element_type=F32)
    x1 = x_ref[0] + g_ref[0] * acc
    x1_ref[...] = x1
    u2 = _modulated_rms(x1, nw_ref[...], sc_ref[0], sh_ref[0])
    u2_ref[...] = u2
    lg_ref[...] = jnp.dot(u2.astype(BF16), rw_ref[...], preferred_element_type=F32) + rb_ref[...]


def out_projection(ya, yb, yc, w_out, x, g1, norm_w, sc, sh, router_w, router_b):
    G, R, d = x.shape
    t = min(R, ROW_TILE)
    nt = R // t
    rows = lambda n: pl.BlockSpec((t, n), lambda g, i: (g * nt + i, 0))
    full = lambda a: pl.BlockSpec(a.shape, lambda g, i: (0,) * a.ndim)
    nw = norm_w.reshape(1, d)
    return pl.pallas_call(
        _outproj_kernel,
        grid=(G, nt),
        in_specs=[rows(A_WIDTH), rows(B_WIDTH), rows(C_WIDTH), full(w_out),
                  pl.BlockSpec((1, t, d), lambda g, i: (g, i, 0)), _mod_spec(g1, t), full(nw),
                  _mod_spec(sc, t), _mod_spec(sh, t), full(router_w), full(router_b)],
        out_specs=[rows(d), rows(d), rows(LANE)],
        out_shape=[jax.ShapeDtypeStruct((G * R, d), F32), jax.ShapeDtypeStruct((G * R, d), F32),
                   jax.ShapeDtypeStruct((G * R, LANE), F32)],
        compiler_params=pltpu.CompilerParams(dimension_semantics=("parallel", "parallel"),
                                             vmem_limit_bytes=VMEM_LIMIT),
        name="out_projection",
    )(ya, yb, yc, w_out, x, g1, nw, sc, sh, router_w, router_b)


def _swiglu_kernel(te_ref, tv_ref, u_ref, wg_ref, wu_ref, wd_ref, o_ref, acc_ref):
    i, j = pl.program_id(0), pl.program_id(1)
    last = pl.num_programs(1) - 1

    @pl.when(tv_ref[i] == 1)
    def _():
        u = u_ref[...].astype(BF16)
        g = jnp.dot(u, wg_ref[0], preferred_element_type=F32)
        up = jnp.dot(u, wu_ref[0], preferred_element_type=F32)
        h = (g * jax.nn.sigmoid(g) * up).astype(BF16)
        part = jnp.dot(h, wd_ref[0], preferred_element_type=F32)

        @pl.when(j == 0)
        def _():
            acc_ref[...] = part

        @pl.when(j > 0)
        def _():
            acc_ref[...] += part

    @pl.when(j == last)
    def _():
        o_ref[...] = jnp.where(tv_ref[i] == 1, acc_ref[...], 0.0)


def grouped_swiglu(u, w_gate, w_up, w_down, tile_expert, tile_valid, ff_tile):
    P, d = u.shape
    F = w_gate.shape[2]
    t = FFN_ROW_TILE
    grid_spec = pltpu.PrefetchScalarGridSpec(
        num_scalar_prefetch=2,
        grid=(P // t, F // ff_tile),
        in_specs=[pl.BlockSpec((t, d), lambda i, j, te, tv: (i, 0)),
                  pl.BlockSpec((1, d, ff_tile), lambda i, j, te, tv: (te[i], 0, j)),
                  pl.BlockSpec((1, d, ff_tile), lambda i, j, te, tv: (te[i], 0, j)),
                  pl.BlockSpec((1, ff_tile, d), lambda i, j, te, tv: (te[i], j, 0))],
        out_specs=pl.BlockSpec((t, d), lambda i, j, te, tv: (i, 0)),
        scratch_shapes=[pltpu.VMEM((t, d), F32)])
    return pl.pallas_call(
        _swiglu_kernel,
        grid_spec=grid_spec,
        out_shape=jax.ShapeDtypeStruct((P, d), F32),
        compiler_params=pltpu.CompilerParams(dimension_semantics=("parallel", "arbitrary"),
                                             vmem_limit_bytes=VMEM_LIMIT),
        name="grouped_swiglu",
    )(tile_expert, tile_valid, u, w_gate, w_up, w_down)


def _fox_flash_kernel(q_ref, k_ref, v_ref, cq_ref, ck_ref, o_ref, m_scr, l_scr, acc_scr):
    qi, ki = pl.program_id(2), pl.program_id(3)
    tq, tk = q_ref.shape[2], k_ref.shape[2]

    @pl.when(ki == 0)
    def _():
        m_scr[...] = jnp.full(m_scr.shape, -jnp.inf, F32)
        l_scr[...] = jnp.zeros(l_scr.shape, F32)
        acc_scr[...] = jnp.zeros(acc_scr.shape, F32)

    def tile(on_diagonal):
        s = lax.dot_general(q_ref[0, 0], k_ref[0, 0], _NT, preferred_element_type=F32)
        s = s + cq_ref[0, 0] - ck_ref[0, 0]
        if on_diagonal:
            s = jnp.where(lax.broadcasted_iota(jnp.int32, (tq, tk), 1)
                          <= lax.broadcasted_iota(jnp.int32, (tq, tk), 0), s, -jnp.inf)
        m_prev = m_scr[...]
        m_new = jnp.maximum(m_prev, jnp.max(s, axis=-1, keepdims=True))
        alpha = jnp.exp(m_prev - m_new)
        p = jnp.exp(s - m_new)
        l_scr[...] = alpha * l_scr[...] + jnp.sum(p, axis=-1, keepdims=True)
        acc_scr[...] = alpha * acc_scr[...] + jnp.dot(p.astype(BF16), v_ref[0, 0], preferred_element_type=F32)
        m_scr[...] = m_new

    pl.when(ki < qi)(functools.partial(tile, False))
    pl.when(ki == qi)(functools.partial(tile, True))

    @pl.when(ki == pl.num_programs(3) - 1)
    def _():
        o_ref[0, 0] = acc_scr[...] / l_scr[...]


def fox_flash_attention(q, k, v, cq, ck):
    N, H, L, hd = q.shape
    t = min(L, ATT_TILE)
    nt = L // t
    kv_spec = pl.BlockSpec((1, 1, t, hd), lambda n, h, qi, ki: (n, h, jnp.minimum(ki, qi), 0))
    return pl.pallas_call(
        _fox_flash_kernel,
        grid=(N, H, nt, nt),
        in_specs=[pl.BlockSpec((1, 1, t, hd), lambda n, h, qi, ki: (n, h, qi, 0)), kv_spec, kv_spec,
                  pl.BlockSpec((1, 1, t, 1), lambda n, h, qi, ki: (n, h, qi, 0)),
                  pl.BlockSpec((1, 1, 1, t), lambda n, h, qi, ki: (n, h, 0, jnp.minimum(ki, qi)))],
        out_specs=pl.BlockSpec((1, 1, t, hd), lambda n, h, qi, ki: (n, h, qi, 0)),
        out_shape=jax.ShapeDtypeStruct((N, H, L, hd), F32),
        scratch_shapes=[pltpu.VMEM((t, 1), F32), pltpu.VMEM((t, 1), F32), pltpu.VMEM((t, hd), F32)],
        compiler_params=pltpu.CompilerParams(
            dimension_semantics=("parallel", "parallel", "parallel", "arbitrary"), vmem_limit_bytes=VMEM_LIMIT),
        name="fox_flash_attention",
    )(q, k, v, cq, ck)


def _split(t, sizes):
    return jnp.split(t, np.cumsum(sizes)[:-1].tolist(), axis=-1)


def rms_norm(x, w):
    xf = x.astype(jnp.float32)
    y = xf * lax.rsqrt(jnp.mean(xf * xf, axis=-1, keepdims=True) + NORM_EPS)
    return (y * w.astype(jnp.float32)).astype(x.dtype)


def head_rms(t, w):
    tf = t.astype(jnp.float32)
    return (tf * lax.rsqrt(jnp.mean(tf * tf, axis=-1, keepdims=True) + NORM_EPS) * w).astype(t.dtype)


def ada_modulation(c, w_ada, b_ada):
    m = jax.nn.silu(c) @ w_ada + b_ada
    return jnp.split(m[:, None, :], 6, axis=-1)


def swiglu(t, w_gate, w_up, w_down):
    return (jax.nn.silu(t @ w_gate) * (t @ w_up)) @ w_down


def dense_swiglu(u, w_gate, w_up, w_down, ff_tile):
    nt = u.shape[0] // FFN_ROW_TILE
    return grouped_swiglu(u, w_gate[None], w_up[None], w_down[None],
                          jnp.zeros((nt,), jnp.int32), jnp.ones((nt,), jnp.int32), ff_tile)


def moe_swiglu(u, logits, w_gate, w_up, w_down, ff_tile):
    T, d = u.shape
    E = logits.shape[1]
    t = FFN_ROW_TILE
    top_val, top_idx = lax.top_k(logits, TOP_K)
    weights = jax.nn.softmax(top_val, axis=-1)
    eid = top_idx.reshape(-1).astype(jnp.int32)
    n_pairs = T * TOP_K
    order = jnp.argsort(eid, stable=True).astype(jnp.int32)
    counts = jnp.sum(jax.nn.one_hot(eid, E, dtype=jnp.int32), axis=0)
    padded = ((counts + t - 1) // t) * t
    ends_p = jnp.cumsum(padded)
    start_p = ends_p - padded
    start = jnp.cumsum(counts) - counts
    P = n_pairs + E * t
    tile_start = jnp.arange(P // t, dtype=jnp.int32) * t
    tile_expert = jnp.minimum(jnp.searchsorted(ends_p, tile_start, side='right'), E - 1).astype(jnp.int32)
    tile_valid = (tile_start < ends_p[-1]).astype(jnp.int32)
    per_row = lambda table: jnp.repeat(table[tile_expert], t)
    row_off = jnp.arange(P, dtype=jnp.int32) - per_row(start_p)
    row_ok = row_off < per_row(counts)
    src_pair = order[jnp.clip(per_row(start) + row_off, 0, n_pairs - 1)]
    src_tok = jnp.where(row_ok, src_pair // TOP_K, 0)
    rank = jnp.argsort(order).astype(jnp.int32)
    shift_e = jnp.sum(jax.nn.one_hot(eid, E, dtype=jnp.int32) * (start_p - start)[None, :], axis=1)
    pos_of_pair = (rank + shift_e).reshape(T, TOP_K)
    ys = grouped_swiglu(u[src_tok], w_gate, w_up, w_down, tile_expert, tile_valid, ff_tile)
    out = weights[:, 0:1] * ys[pos_of_pair[:, 0]]
    for s in range(1, TOP_K):
        out = out + weights[:, s:s + 1] * ys[pos_of_pair[:, s]]
    return out


RWKV_CHUNK = 64
RWKV_BLOCK = 256
RWKV_INV_BLOCK = 16

_NN = (((1,), (0,)), ((), ()))
_NT = (((1,), (1,)), ((), ()))


def _dot(a, b, dims):
    return lax.dot_general(a, b, dims, preferred_element_type=jnp.float32)


_BNN = (((2,), (1,)), ((0,), (0,)))
_BNT = (((2,), (2,)), ((0,), (0,)))


def _sp(x):
    hi = x.astype(BF16)
    lo = (x - hi.astype(F32)).astype(BF16)
    return hi, lo


def _mmp(a, b, dims=_BNN):
    (ah, al), (bh, bl) = a, b
    return _dot(ah, bh, dims) + _dot(ah, bl, dims) + _dot(al, bh, dims)


def _mm_exact_lhs(a_bf16, x):
    h1 = x.astype(BF16)
    r1 = x - h1.astype(F32)
    h2 = r1.astype(BF16)
    h3 = (r1 - h2.astype(F32)).astype(BF16)
    return _dot(a_bf16, h1, _BNN) + _dot(a_bf16, h2, _BNN) + _dot(a_bf16, h3, _BNN)


def _unit_lower_inverse(a, blk_mask, eye):
    ad = jnp.where(blk_mask, a, 0.0)
    e = a - ad
    ad_p = _sp(ad)
    p2 = _mmp(ad_p, ad_p)
    p2_p = _sp(p2)
    p4 = _mmp(p2_p, p2_p)
    p4_p = _sp(p4)
    p8 = _mmp(p4_p, p4_p)
    x = _mmp(_sp(eye - ad), _sp(eye + p2))
    x = _mmp(_sp(x), _sp(eye + p4))
    dinv_p = _sp(_mmp(_sp(x), _sp(eye + p8)))
    n = _mmp(dinv_p, _sp(e))
    n_p = _sp(n)
    n2 = _mmp(n_p, n_p)
    return _mmp(_sp(_mmp(_sp(eye - n), _sp(eye + n2))), dinv_p)


def _rwkv_chunk_prep(r, lw, k, v, vt, kk, a):
    B, T, _ = r.shape
    row = lax.broadcasted_iota(jnp.int32, (T, T), 0)
    col = lax.broadcasted_iota(jnp.int32, (T, T), 1)
    strict, incl = row > col, row >= col
    blk = (row // RWKV_INV_BLOCK) == (col // RWKV_INV_BLOCK)
    eye = (row == col).astype(F32)
    c = _mm_exact_lhs(jnp.broadcast_to(incl.astype(BF16), (B, T, T)), lw)
    c_last = c[:, T - 1:T, :]
    e_nc = jnp.exp(-c)
    e_end = jnp.exp(c_last - c)
    b = kk * a
    kt = _sp(kk * jnp.exp(c - lw))
    rt = _sp(r * jnp.exp(c))
    kh = _sp(k * e_nc)
    bh = _sp(b * e_nc)
    v_p, vt_p = _sp(v), _sp(vt)
    a_kb = jnp.where(strict, _mmp(kt, bh, _BNT), 0.0)
    a_kk = _sp(jnp.where(strict, _mmp(kt, kh, _BNT), 0.0))
    a_rk = _sp(jnp.where(incl, _mmp(rt, kh, _BNT), 0.0))
    a_rb = _sp(jnp.where(incl, _mmp(rt, bh, _BNT), 0.0))
    m = _sp(_unit_lower_inverse(a_kb, blk, eye))
    mk = _sp(_mmp(m, kt))
    g1 = _mmp(m, _sp(_mmp(a_kk, v_p)))
    g1t = _mmp(_sp(_mmp(vt_p, a_kk, _BNT)), m, _BNT)
    y0 = _mmp(a_rk, v_p)
    s_add = _mmp(vt_p, _sp(k * e_end))
    return dict(mk=mk, g1=g1, g1t=g1t, rt=rt, a_rb=a_rb, y0=y0, s_add=s_add, bbar=_sp(b * e_end),
                decay=jnp.exp(c_last))


def _rwkv_chunk_step(S, q, lo, hi):
    pick = lambda x: tuple(t[lo:hi] for t in x) if isinstance(x, tuple) else x[lo:hi]
    S_p, mk = _sp(S), pick(q['mk'])
    u = _mmp(mk, S_p, _BNT) + pick(q['g1'])
    y = _mmp(pick(q['rt']), S_p, _BNT) + pick(q['y0']) - _mmp(pick(q['a_rb']), _sp(u))
    ut = _mmp(S_p, mk, _BNT) + pick(q['g1t'])
    s_new = S * pick(q['decay']) + pick(q['s_add']) - _mmp(_sp(ut), pick(q['bbar']))
    return y, s_new


def _rwkv_kernel(r_ref, lw_ref, k_ref, v_ref, vt_ref, kk_ref, a_ref, s0_ref, y_ref, sT_ref, s_scr):
    H = r_ref.shape[1]
    T = RWKV_CHUNK

    @pl.when(pl.program_id(1) == 0)
    def _():
        s_scr[...] = s0_ref[0]

    J = RWKV_BLOCK // T
    rows = lambda ref: jnp.concatenate([ref[0, :, j * T:(j + 1) * T, :] for j in range(J)], axis=0)
    vt = jnp.concatenate([vt_ref[0, :, :, j * T:(j + 1) * T] for j in range(J)], axis=0)
    q = _rwkv_chunk_prep(rows(r_ref), rows(lw_ref), rows(k_ref), rows(v_ref), vt, rows(kk_ref), rows(a_ref))
    S = s_scr[...]
    for j in range(J):
        y, S = _rwkv_chunk_step(S, q, j * H, (j + 1) * H)
        y_ref[0, :, j * T:(j + 1) * T, :] = y
    s_scr[...] = S
    sT_ref[0] = S


def rwkv7_recurrence_pallas(r, lw, k, v, kk, a, s0):
    N, L, H, D = r.shape
    hm = lambda t: jnp.transpose(t, (0, 2, 1, 3))
    vt = jnp.transpose(v, (0, 2, 3, 1))
    row_spec = pl.BlockSpec((1, H, RWKV_BLOCK, D), lambda n, c: (n, 0, c, 0))
    st_spec = pl.BlockSpec((1, H, D, D), lambda n, c: (n, 0, 0, 0))
    y, s_last = pl.pallas_call(
        _rwkv_kernel,
        grid=(N, L // RWKV_BLOCK),
        in_specs=[row_spec, row_spec, row_spec, row_spec,
                  pl.BlockSpec((1, H, D, RWKV_BLOCK), lambda n, c: (n, 0, 0, c)),
                  row_spec, row_spec, st_spec],
        out_specs=[row_spec, st_spec],
        out_shape=[jax.ShapeDtypeStruct((N, H, L, D), jnp.float32),
                   jax.ShapeDtypeStruct((N, H, D, D), jnp.float32)],
        scratch_shapes=[pltpu.VMEM((H, D, D), jnp.float32)],
        compiler_params=pltpu.CompilerParams(dimension_semantics=("parallel", "arbitrary")),
        name="rwkv7_chunked",
    )(hm(r), hm(lw), hm(k), hm(v), vt, hm(kk), hm(a), s0)
    return jnp.transpose(y, (0, 2, 1, 3)), s_last


RWKV_SHORT_SEQS = 8


def _rwkv_short_kernel(r_ref, w_ref, k_ref, vt_ref, kk_ref, a_ref, s0_ref, yt_ref, sT_ref):
    nb, H, L, D = r_ref.shape
    B = nb * H
    lane_t = lax.broadcasted_iota(jnp.int32, (B, D, L), 2)
    S = s0_ref[...].reshape(B, D, D)
    yt = jnp.zeros((B, D, L), F32)
    for t in range(L):
        row = lambda ref: ref[:, :, t:t + 1, :].reshape(B, 1, D)
        kk_t = row(kk_ref)
        s_kk = jnp.sum(S * kk_t, axis=-1, keepdims=True)
        S = S * row(w_ref) - s_kk * (kk_t * row(a_ref)) + vt_ref[:, :, :, t:t + 1].reshape(B, D, 1) * row(k_ref)
        yt = jnp.where(lane_t == t, jnp.sum(S * row(r_ref), axis=-1, keepdims=True), yt)
    yt_ref[...] = yt.reshape(nb, H, D, L)
    sT_ref[...] = S.reshape(nb, H, D, D)


def rwkv7_recurrence_short(r, decay, k, v, kk, a, s0):
    N, L, H, D = r.shape
    nb = RWKV_SHORT_SEQS
    hm = lambda t: jnp.transpose(t, (0, 2, 1, 3))
    row_spec = pl.BlockSpec((nb, H, L, D), lambda n: (n, 0, 0, 0))
    col_spec = pl.BlockSpec((nb, H, D, L), lambda n: (n, 0, 0, 0))
    st_spec = pl.BlockSpec((nb, H, D, D), lambda n: (n, 0, 0, 0))
    yt, s_last = pl.pallas_call(
        _rwkv_short_kernel,
        grid=(N // nb,),
        in_specs=[row_spec, row_spec, row_spec, col_spec, row_spec, row_spec, st_spec],
        out_specs=[col_spec, st_spec],
        out_shape=[jax.ShapeDtypeStruct((N, H, D, L), F32), jax.ShapeDtypeStruct((N, H, D, D), F32)],
        compiler_params=pltpu.CompilerParams(dimension_semantics=("parallel",)),
        name="rwkv7_short",
    )(hm(r), hm(decay), hm(k), jnp.transpose(v, (0, 2, 3, 1)), hm(kk), hm(a), s0)
    return jnp.transpose(yt, (0, 3, 1, 2)), s_last


def rwkv7_recurrence(r, decay, k, v, kk, a, s0):
    def step(s, inp):
        r_t, w_t, k_t, v_t, kk_t, a_t = inp
        s_kk = jnp.einsum('nhvk,nhk->nhv', s, kk_t)
        s = (s * w_t[:, :, None, :]
             - s_kk[..., None] * (kk_t * a_t)[:, :, None, :]
             + v_t[..., None] * k_t[:, :, None, :])
        return s, jnp.einsum('nhvk,nhk->nhv', s, r_t)
    seq_first = tuple(jnp.moveaxis(t, 1, 0) for t in (r, decay, k, v, kk, a))
    s_last, ys = lax.scan(step, s0, seq_first)
    return jnp.moveaxis(ys, 0, 1), s_last


def rwkv7_mixer(cols, shift0, s0, mu, w0, w2, a0, a2, g2, k_k, k_a, r_k, ln_w, ln_b):
    n, L, _ = cols.shape
    f32 = jnp.float32
    prev = jnp.concatenate([shift0[:, None, :].astype(cols.dtype), cols[:, :-1]], axis=1)
    xs = cols + (prev - cols) * mu
    r, k, v, xw, xa, xg = _split(xs, [A_WIDTH, A_WIDTH, A_WIDTH, A_DECAY_LORA, A_ICL_LORA, A_GATE_LORA])
    w = -jax.nn.softplus(-(w0 + jnp.tanh(xw) @ w2)) - 0.5
    decay = jnp.exp(-jnp.exp(w.astype(f32)))
    a = jax.nn.sigmoid(a0 + xa @ a2)
    g = jax.nn.sigmoid(xg) @ g2
    heads = lambda t: t.astype(f32).reshape(n, L, A_HEADS, HEAD_DIM)
    kk = heads(k * k_k)
    kk = kk / jnp.maximum(jnp.sqrt(jnp.sum(kk * kk, axis=-1, keepdims=True)), 1e-12)
    k = k * (1.0 + (a - 1.0) * k_a)
    r_h, k_h, v_h = heads(r), heads(k), heads(v)
    if L % RWKV_BLOCK == 0:
        y, s_last = rwkv7_recurrence_pallas(r_h, heads(-jnp.exp(w.astype(f32))), k_h, v_h, kk, heads(a),
                                            s0.astype(f32))
    elif L <= SUBLANE and n % RWKV_SHORT_SEQS == 0:
        y, s_last = rwkv7_recurrence_short(r_h, heads(decay), k_h, v_h, kk, heads(a), s0.astype(f32))
    else:
        y, s_last = rwkv7_recurrence(r_h, heads(decay), k_h, v_h, kk, heads(a), s0.astype(f32))
    mean = jnp.mean(y, axis=-1, keepdims=True)
    var = jnp.mean(jnp.square(y - mean), axis=-1, keepdims=True)
    y = ((y - mean) * lax.rsqrt(var + GN_EPS)).reshape(n, L, A_WIDTH) * ln_w + ln_b
    bonus = jnp.sum(r_h * k_h * r_k, axis=-1, keepdims=True) * v_h
    y = (y + bonus.reshape(n, L, A_WIDTH)) * g
    return y.astype(cols.dtype), s_last, cols[:, -1]


def ssd_chunked(x, dt, A, B, C, h0):
    N, L, H, P = x.shape
    q = min(SSD_CHUNK, L)
    pad = (-L) % q
    if pad:
        padl = lambda t: jnp.pad(t, [(0, 0), (0, pad)] + [(0, 0)] * (t.ndim - 2))
        x, dt, B, C = padl(x), padl(dt), padl(B), padl(C)
    nc = (L + pad) // q
    ch = lambda t: t.reshape((N, nc, q) + t.shape[2:])
    xc, dtc, Bc, Cc = ch(x), ch(dt), ch(B), ch(C)
    cs = jnp.cumsum(dtc * A, axis=2)
    causal = jnp.tril(jnp.ones((q, q), bool))[None, None, :, :, None]
    seg = cs[:, :, :, None, :] - cs[:, :, None, :, :]
    decay = jnp.exp(jnp.where(causal, seg, -jnp.inf))
    scores = jnp.einsum('nclhd,ncshd->nclsh', Cc, Bc) * decay * dtc[:, :, None, :, :]
    y_intra = jnp.einsum('nclsh,ncshp->nclhp', scores, xc)
    to_end = jnp.exp(cs[:, :, -1:, :] - cs) * dtc
    chunk_states = jnp.einsum('ncsh,ncshd,ncshp->nchpd', to_end, Bc, xc)
    chunk_decay = jnp.exp(cs[:, :, -1, :])
    def carry(h, inp):
        dec, st = inp
        return h * dec[:, :, None, None] + st, h
    h_last, h_in = lax.scan(carry, h0, (jnp.moveaxis(chunk_decay, 1, 0), jnp.moveaxis(chunk_states, 1, 0)))
    h_in = jnp.moveaxis(h_in, 0, 1)
    y_inter = jnp.einsum('nclhd,nchpd->nclhp', Cc, h_in) * jnp.exp(cs)[..., None]
    y = (y_intra + y_inter).reshape(N, nc * q, H, P)[:, :L]
    return y, h_last


def mamba2_mixer(z, xbc, dt_raw, conv0, ssm0, conv_w, conv_b, dt_bias, a_log, d_skip, norm_w):
    n, L, _ = xbc.shape
    f32 = jnp.float32
    xpad = jnp.concatenate([conv0.astype(xbc.dtype), xbc], axis=1)
    conv = lax.conv_general_dilated(xpad, conv_w[:, None, :], window_strides=(1,), padding='VALID',
                                    dimension_numbers=('NWC', 'WIO', 'NWC'),
                                    feature_group_count=B_CONV_DIM) + conv_b
    xs, Bm, Cm = _split(jax.nn.silu(conv), [B_WIDTH, B_GROUPS * D_STATE, B_GROUPS * D_STATE])
    xh = xs.astype(f32).reshape(n, L, B_HEADS, HEAD_DIM)
    rep = B_HEADS // B_GROUPS
    Bh = jnp.repeat(Bm.astype(f32).reshape(n, L, B_GROUPS, D_STATE), rep, axis=2)
    Ch = jnp.repeat(Cm.astype(f32).reshape(n, L, B_GROUPS, D_STATE), rep, axis=2)
    dt = jax.nn.softplus(dt_raw.astype(f32) + dt_bias)
    A = -jnp.exp(a_log.astype(f32))
    y, h_last = ssd_chunked(xh, dt, A, Bh, Ch, ssm0.astype(f32))
    y = (y + d_skip[:, None] * xh).reshape(n, L, B_WIDTH) * jax.nn.silu(z.astype(f32))
    yg = y.reshape(n, L, B_GROUPS, B_WIDTH // B_GROUPS)
    yg = yg * lax.rsqrt(jnp.mean(yg * yg, axis=-1, keepdims=True) + NORM_EPS)
    y = yg.reshape(n, L, B_WIDTH) * norm_w
    return y.astype(z.dtype), h_last, xpad[:, -(CONV_W - 1):]


SSD_PAIRS = B_HEADS // 2
CONV_TAIL = 8


def _exact_rhs_mm(x, sel_bf16):
    h1 = x.astype(BF16)
    r1 = x - h1.astype(F32)
    h2 = r1.astype(BF16)
    h3 = (r1 - h2.astype(F32)).astype(BF16)
    d = lambda a: jnp.dot(a, sel_bf16, preferred_element_type=F32)
    return d(h1) + d(h2) + d(h3)


def _ssd_kernel(xbc_ref, z_ref, sm_ref, conv0_ref, h0_ref, cw_ref, cb_ref, dtb_ref, a_ref, dskip_ref, nw_ref,
                y_ref, hT_ref, xbuf, h_scr):
    Q = SSD_CHUNK
    c = pl.program_id(1)

    @pl.when(c == 0)
    def _():
        xbuf[0:CONV_TAIL, :] = conv0_ref[0]
        h_scr[...] = h0_ref[0]

    @pl.when(c > 0)
    def _():
        xbuf[0:CONV_TAIL, :] = xbuf[Q:Q + CONV_TAIL, :]

    xbuf[CONV_TAIL:CONV_TAIL + Q, :] = xbc_ref[0]
    conv = cb_ref[...]
    for j in range(CONV_W):
        conv = conv + cw_ref[j:j + 1, :] * xbuf[pl.ds(CONV_TAIL - (CONV_W - 1) + j, Q), :]
    act = conv * jax.nn.sigmoid(conv)
    xs = act[:, :B_WIDTH]
    n_bc = B_GROUPS * D_STATE
    Bm, Cm = act[:, B_WIDTH:B_WIDTH + n_bc], act[:, B_WIDTH + n_bc:]

    pre = sm_ref[0] + dtb_ref[...]
    dt = jnp.maximum(pre, 0.0) + jnp.log(1.0 + jnp.exp(-jnp.abs(pre)))
    dA = dt * a_ref[...]
    row = lax.broadcasted_iota(jnp.int32, (Q, Q), 0)
    col = lax.broadcasted_iota(jnp.int32, (Q, Q), 1)
    causal = row >= col
    tri = causal.astype(BF16)
    h1 = dA.astype(BF16)
    r1 = dA - h1.astype(F32)
    h2 = r1.astype(BF16)
    h3 = (r1 - h2.astype(F32)).astype(BF16)
    cs = (jnp.dot(tri, h1, preferred_element_type=F32) + jnp.dot(tri, h2, preferred_element_type=F32)
          + jnp.dot(tri, h3, preferred_element_type=F32))
    csT, dtT = cs.T, dt.T
    sel_b = (lax.broadcasted_iota(jnp.int32, (LANE, B_WIDTH), 0)
             == lax.broadcasted_iota(jnp.int32, (LANE, B_WIDTH), 1) // HEAD_DIM).astype(BF16)
    sel_h = (lax.broadcasted_iota(jnp.int32, (LANE, B_HEADS * Q), 0)
             == lax.broadcasted_iota(jnp.int32, (LANE, B_HEADS * Q), 1) // Q).astype(BF16)
    cs_b = _exact_rhs_mm(cs, sel_b)
    dt_b = _exact_rhs_mm(dt, sel_b)
    cs_full = _exact_rhs_mm(cs, sel_h)
    cs_last_b = cs_b[Q - 1:Q, :]
    e_b = jnp.exp(cs_b)
    xs_bf = xs.astype(BF16)
    xs_te = (xs * (jnp.exp(cs_last_b - cs_b) * dt_b)).astype(BF16)
    first_half = lax.broadcasted_iota(jnp.int32, (Q, LANE), 1) < HEAD_DIM
    ys = []
    for g in range(B_GROUPS):
        gs = slice(g * D_STATE, (g + 1) * D_STATE)
        Cg, Bg = Cm[:, gs].astype(BF16), Bm[:, gs]
        cb = lax.dot_general(Cg, Bg.astype(BF16), _NT, preferred_element_type=F32)
        BTg = Bg.T.astype(BF16)
        for pp in range(SSD_PAIRS // B_GROUPS):
            pair = g * (SSD_PAIRS // B_GROUPS) + pp
            lanes = slice(pair * LANE, (pair + 1) * LANE)
            hT = h_scr[pair]
            y_pair = jnp.dot(Cg, hT.astype(BF16), preferred_element_type=F32) * e_b[:, lanes]
            for j in range(2):
                h = 2 * pair + j
                seg = cs_full[:, h * Q:(h + 1) * Q] - csT[h:h + 1, :]
                dec = jnp.exp(jnp.where(causal, seg, -jnp.inf))
                sc = (cb * dec * dtT[h:h + 1, :]).astype(BF16)
                xm = jnp.where(first_half if j == 0 else jnp.logical_not(first_half), xs_bf[:, lanes], 0.0)
                y_pair = y_pair + jnp.dot(sc, xm.astype(BF16), preferred_element_type=F32)
            h_scr[pair] = hT * jnp.exp(cs_last_b[:, lanes]) + jnp.dot(BTg, xs_te[:, lanes],
                                                                      preferred_element_type=F32)
            ys.append(y_pair)
    y = jnp.concatenate(ys, axis=-1)
    z = z_ref[0]
    y = (y + dskip_ref[...] * xs) * (z * jax.nn.sigmoid(z))
    gw = B_WIDTH // B_GROUPS
    outs = []
    for g in range(B_GROUPS):
        yg = y[:, g * gw:(g + 1) * gw]
        outs.append(yg * lax.rsqrt(jnp.mean(yg * yg, axis=-1, keepdims=True) + NORM_EPS))
    y_ref[0] = jnp.concatenate(outs, axis=-1) * nw_ref[...]
    hT_ref[0] = h_scr[...]


def mamba2_mixer_pallas(z, xbc, small, conv0, ssm0, conv_w, conv_b, dt_bias, a_log, d_skip, norm_w):
    n, L, _ = xbc.shape
    Q = SSD_CHUNK
    pad_l = lambda v: jnp.pad(v.astype(F32), (0, LANE - v.shape[0]))[None, :]
    tail0 = jnp.pad(conv0.astype(F32), ((0, 0), (CONV_TAIL - (CONV_W - 1), 0), (0, 0)))
    h0 = ssm0.astype(F32).reshape(n, SSD_PAIRS, 2, HEAD_DIM, D_STATE)
    h0 = jnp.transpose(h0, (0, 1, 4, 2, 3)).reshape(n, SSD_PAIRS, D_STATE, 2 * HEAD_DIM)
    full = lambda a: pl.BlockSpec(a.shape, lambda i, c: (0,) * a.ndim)
    seq = lambda w: pl.BlockSpec((1, Q, w), lambda i, c: (i, c, 0))
    args = (xbc, z, small, tail0, h0, conv_w.astype(F32), conv_b.astype(F32)[None, :], pad_l(dt_bias),
            pad_l(-jnp.exp(a_log.astype(F32))), jnp.repeat(d_skip.astype(F32), HEAD_DIM)[None, :],
            norm_w.astype(F32)[None, :])
    st_spec = pl.BlockSpec((1, SSD_PAIRS, D_STATE, 2 * HEAD_DIM), lambda i, c: (i, 0, 0, 0))
    y, hT = pl.pallas_call(
        _ssd_kernel,
        grid=(n, L // Q),
        in_specs=[seq(B_CONV_DIM), seq(B_WIDTH), seq(LANE),
                  pl.BlockSpec((1, CONV_TAIL, B_CONV_DIM), lambda i, c: (i, 0, 0)), st_spec]
                 + [full(a) for a in args[5:]],
        out_specs=[seq(B_WIDTH), st_spec],
        out_shape=[jax.ShapeDtypeStruct((n, L, B_WIDTH), F32),
                   jax.ShapeDtypeStruct((n, SSD_PAIRS, D_STATE, 2 * HEAD_DIM), F32)],
        scratch_shapes=[pltpu.VMEM((Q + CONV_TAIL, B_CONV_DIM), F32),
                        pltpu.VMEM((SSD_PAIRS, D_STATE, 2 * HEAD_DIM), F32)],
        compiler_params=pltpu.CompilerParams(dimension_semantics=("parallel", "arbitrary"),
                                             vmem_limit_bytes=VMEM_LIMIT),
        name="mamba2_ssd",
    )(*args)
    h_last = jnp.transpose(hT.reshape(n, SSD_PAIRS, D_STATE, 2, HEAD_DIM), (0, 1, 3, 4, 2))
    h_last = h_last.reshape(n, B_HEADS, HEAD_DIM, D_STATE)
    tail = jnp.concatenate([conv0.astype(xbc.dtype), xbc[:, -(CONV_W - 1):]], axis=1)[:, -(CONV_W - 1):]
    return y, h_last, tail


def fox_attention(q, k, v, logf, past):
    n, L, H, hd = q.shape
    f32 = jnp.float32
    cum = jnp.cumsum(logf, axis=1)
    if past is None:
        P = 0
        keys_k, keys_v, key_c = k, v, cum
    else:
        pool_k, pool_v, pool_logf, page_table = past
        P = page_table.shape[1] * PAGE_SIZE
        gather = lambda pool: pool[page_table].reshape((n, P) + pool.shape[2:])
        plf = gather(pool_logf).astype(f32)
        suffix = jnp.cumsum(plf[:, ::-1], axis=1)[:, ::-1] - plf
        keys_k = jnp.concatenate([gather(pool_k).astype(k.dtype), k], axis=1)
        keys_v = jnp.concatenate([gather(pool_v).astype(v.dtype), v], axis=1)
        key_c = jnp.concatenate([-suffix, cum], axis=1)
    kc = jnp.moveaxis(key_c, 2, 1)
    kpos = jnp.arange(P + L)
    qb = Q_BLOCK if L % Q_BLOCK == 0 else L
    nb = L // qb
    scale = hd ** -0.5

    def block(args):
        q_blk, c_blk, t0 = args
        s = jnp.einsum('nqhd,nkhd->nhqk', q_blk, keys_k, preferred_element_type=f32) * scale
        s = s + jnp.moveaxis(c_blk, 2, 1)[..., None] - kc[:, :, None, :]
        mask = kpos[None, :] <= (t0 + jnp.arange(qb))[:, None]
        prob = jax.nn.softmax(jnp.where(mask, s, -jnp.inf), axis=-1)
        return jnp.einsum('nhqk,nkhd->nqhd', prob.astype(keys_v.dtype), keys_v)

    q_blocks = jnp.moveaxis(q.reshape(n, nb, qb, H, hd), 1, 0)
    c_blocks = jnp.moveaxis(cum.reshape(n, nb, qb, H), 1, 0)
    starts = P + jnp.arange(nb) * qb
    out = lax.map(block, (q_blocks, c_blocks, starts))
    return jnp.moveaxis(out, 0, 1).reshape(n, L, H, hd)


SUBLANE = 8


def _fox_paged_kernel(pt_ref, qbd_ref, knew_ref, vnew_ref, cq_ref, ckn_ref, *refs, n_pages, n_new):
    k_refs, v_refs, lf_refs = refs[:n_pages], refs[n_pages:2 * n_pages], refs[2 * n_pages:3 * n_pages]
    o_ref = refs[3 * n_pages]
    PS = k_refs[0].shape[1]
    qbd = qbd_ref[0]
    cq = cq_ref[0]
    lf = jnp.concatenate([r[0] for r in lf_refs], axis=0)
    after = (lax.broadcasted_iota(jnp.int32, (PS, PS), 0)
             > lax.broadcasted_iota(jnp.int32, (PS, PS), 1)).astype(BF16)
    suf = _exact_rhs_mm(lf, after)
    page_sum = jnp.sum(lf, axis=-1, keepdims=True)
    tails = [jnp.zeros((SUBLANE, 1), F32)]
    for j in range(n_pages - 1, 0, -1):
        tails.append(tails[-1] + page_sum[j * SUBLANE:(j + 1) * SUBLANE])
    tails = tails[::-1]
    scores = []
    for j in range(n_pages):
        s = lax.dot_general(qbd, k_refs[j][0].astype(BF16), _NT, preferred_element_type=F32)
        brow = suf[j * SUBLANE:(j + 1) * SUBLANE] + tails[j]
        bias = jnp.concatenate([jnp.broadcast_to(brow[h:h + 1], (SUBLANE, PS)) for h in range(C_HEADS)], axis=0)
        scores.append(s + cq + bias)
    sn = lax.dot_general(qbd, knew_ref[0], _NT, preferred_element_type=F32)
    t_idx = lax.broadcasted_iota(jnp.int32, sn.shape, 0) % SUBLANE
    s_idx = lax.broadcasted_iota(jnp.int32, sn.shape, 1)
    sn = jnp.where((s_idx <= t_idx) & (s_idx < n_new), sn + cq - ckn_ref[0], -jnp.inf)
    m = jnp.max(sn, axis=-1, keepdims=True)
    for s in scores:
        m = jnp.maximum(m, jnp.max(s, axis=-1, keepdims=True))
    pn = jnp.exp(sn - m)
    l = jnp.sum(pn, axis=-1, keepdims=True)
    acc = jnp.dot(pn.astype(BF16), vnew_ref[0], preferred_element_type=F32)
    for j in range(n_pages):
        pj = jnp.exp(scores[j] - m)
        l = l + jnp.sum(pj, axis=-1, keepdims=True)
        acc = acc + jnp.dot(pj.astype(BF16), v_refs[j][0].astype(BF16), preferred_element_type=F32)
    out = acc / l
    lane_head = lax.broadcasted_iota(jnp.int32, (SUBLANE, C_WIDTH), 1) // HEAD_DIM
    o = jnp.zeros((SUBLANE, C_WIDTH), F32)
    for h in range(C_HEADS):
        o = o + jnp.where(lane_head == h, out[h * SUBLANE:(h + 1) * SUBLANE], 0.0)
    o_ref[0] = o


def fox_paged_attention(qh, kh, vh, logf, pools_k, pools_v, pools_logf, page_table, layer):
    n, L, H, hd = qh.shape
    depth, n_phys, PS = pools_k.shape[:3]
    n_pages = page_table.shape[1]
    R = H * SUBLANE
    pad_t = lambda t: jnp.pad(t, ((0, 0), (0, 0), (0, SUBLANE - L)) + ((0, 0),) * (t.ndim - 3))
    q_hm = pad_t(jnp.transpose(qh * (hd ** -0.5), (0, 2, 1, 3)))
    qbd = (jnp.eye(H, dtype=F32)[None, :, None, :, None] * q_hm[:, :, :, None, :]).reshape(n, R, H * hd)
    new_rows = lambda t: jnp.pad(t.reshape(n, L, H * hd), ((0, 0), (0, SUBLANE - L), (0, 0))).astype(BF16)
    cum = pad_t(jnp.transpose(jnp.cumsum(logf, axis=1), (0, 2, 1)))
    cq = cum.reshape(n, R, 1)
    ckn = jnp.broadcast_to(cum[:, :, None, :], (n, H, SUBLANE, SUBLANE)).reshape(n, R, SUBLANE)
    lf_t = jnp.pad(jnp.transpose(pools_logf[layer].astype(F32), (0, 2, 1)), ((0, 0), (0, SUBLANE - H), (0, 0)))
    pk, pv = (t.reshape(depth * n_phys, PS, H * hd) for t in (pools_k, pools_v))
    base = layer * n_phys
    seq = lambda r, c: pl.BlockSpec((1, r, c), lambda i, pt: (i, 0, 0))
    page = lambda j, r, c, off: pl.BlockSpec((1, r, c), lambda i, pt, j=j: (pt[i, j] + off, 0, 0))
    grid_spec = pltpu.PrefetchScalarGridSpec(
        num_scalar_prefetch=1,
        grid=(n,),
        in_specs=[seq(R, H * hd), seq(SUBLANE, H * hd), seq(SUBLANE, H * hd), seq(R, 1), seq(R, SUBLANE)]
                 + [page(j, PS, H * hd, base) for j in range(n_pages)] * 2
                 + [page(j, SUBLANE, PS, 0) for j in range(n_pages)],
        out_specs=seq(SUBLANE, H * hd))
    o = pl.pallas_call(
        functools.partial(_fox_paged_kernel, n_pages=n_pages, n_new=L),
        grid_spec=grid_spec,
        out_shape=jax.ShapeDtypeStruct((n, SUBLANE, H * hd), F32),
        compiler_params=pltpu.CompilerParams(dimension_semantics=("parallel",), vmem_limit_bytes=VMEM_LIMIT),
        name="fox_paged_attention",
    )(page_table, qbd.astype(BF16), new_rows(kh), new_rows(vh), cq, ckn,
      *([pk] * n_pages), *([pv] * n_pages), *([lf_t] * n_pages))
    return o[:, :L]


def fox_mixer(q, k, v, f_raw, f_bias, qn_w, kn_w, past, layer):
    n, L, _ = q.shape
    qh = head_rms(q.reshape(n, L, C_HEADS, HEAD_DIM), qn_w)
    kh = head_rms(k.reshape(n, L, C_HEADS, HEAD_DIM), kn_w)
    vh = v.reshape(n, L, C_HEADS, HEAD_DIM)
    logf = jax.nn.log_sigmoid(f_raw.astype(jnp.float32) + f_bias)
    if past is None and L % ATT_TILE == 0:
        cum = jnp.transpose(jnp.cumsum(logf, axis=1), (0, 2, 1))
        hm = lambda t: jnp.transpose(t, (0, 2, 1, 3)).astype(BF16)
        o = fox_flash_attention(hm(qh * (HEAD_DIM ** -0.5)), hm(kh), hm(vh), cum[..., None], cum[:, :, None, :])
        o = jnp.transpose(o, (0, 2, 1, 3))
    elif past is not None and L <= SUBLANE:
        o = fox_paged_attention(qh, kh, vh, logf, *past, layer)
    else:
        o = fox_attention(qh, kh, vh, logf,
                          None if past is None else (past[0][layer], past[1][layer], past[2][layer], past[3]))
    return o.reshape(n, L, C_WIDTH).astype(q.dtype), kh, vh, logf


def mix_group(x, c, l, p, s0, shift0, ssm0, conv0, past):
    n, L, d = x.shape
    mods = ada_modulation(c, p['w_ada'][l], p['b_ada'][l])
    if L >= ROW_TILE:
        xg = x
    else:
        xg = x.reshape(1, n * L, d)
        mods = [jnp.broadcast_to(m, (n, L, d)).reshape(1, n * L, d) for m in mods]
    sh1, sc1, g1, sh2, sc2, g2 = mods
    a_cols, b_z, b_xbc, c_q, c_k, c_v, small = in_projection(xg, p['norm1_w'][l], sc1, sh1, p['w_in_perm'][l])
    rs = lambda t: t.reshape(n, L, t.shape[-1])
    a_cols, b_z, b_xbc = rs(a_cols), rs(b_z), rs(b_xbc)
    b_dt, c_f = rs(small[:, :B_HEADS]), rs(small[:, B_HEADS:B_HEADS + C_HEADS])
    c_q, c_k, c_v = rs(c_q), rs(c_k), rs(c_v)
    y_a, s_new, shift_new = rwkv7_mixer(
        a_cols, shift0, s0, p['a_mu'][l], p['a_w0'][l], p['a_w2'][l], p['a_a0'][l], p['a_a2'][l],
        p['a_g2'][l], p['a_kk'][l], p['a_ka'][l], p['a_rk'][l], p['a_ln_w'][l], p['a_ln_b'][l])
    mamba = mamba2_mixer_pallas if L % SSD_CHUNK == 0 else mamba2_mixer
    y_b, ssm_new, conv_new = mamba(
        b_z, b_xbc, rs(small) if L % SSD_CHUNK == 0 else b_dt, conv0, ssm0, p['b_conv_w'][l], p['b_conv_b'][l],
        p['b_dt_bias'][l], p['b_a_log'][l], p['b_d'][l], p['b_norm_w'][l])
    y_c, k_rows, v_rows, logf_rows = fox_mixer(
        c_q, c_k, c_v, c_f, p['c_f_bias'][l], p['c_qnorm_w'][l], p['c_knorm_w'][l], past, l)
    fl = lambda t: t.reshape(n * L, t.shape[-1])
    x1, u2, logits = out_projection(fl(y_a), fl(y_b), fl(y_c), p['w_out_bf16'][l], xg, g1, p['norm2_w'][l],
                                    sc2, sh2, p['router_w_pad'][l // 2], p['router_b_pad'][l // 2])
    dt = x.dtype
    states = (s_new.astype(dt), shift_new.astype(dt), ssm_new.astype(dt), conv_new.astype(dt),
              k_rows.astype(dt), v_rows.astype(dt), logf_rows.astype(dt))
    return x1.reshape(xg.shape), u2, logits, g2, states


def channel_mix(l, p, x1s, u2s, logits, g2s):
    u2 = jnp.concatenate(u2s, axis=0)
    j = l // 2
    if l % 2 == 0:
        f = dense_swiglu(u2, p['ffn_w_gate'][j], p['ffn_w_up'][j], p['ffn_w_down'][j], FFN_TILE_DENSE)
    else:
        lg = jnp.concatenate(logits, axis=0)[:, :N_EXPERTS]
        f = moe_swiglu(u2, lg, p['moe_w_gate'][j], p['moe_w_up'][j], p['moe_w_down'][j], FFN_TILE_EXPERT)
    outs, row0 = [], 0
    for x1, g2 in zip(x1s, g2s):
        outs.append(gated_residual(x1, g2, f, row0))
        row0 += x1.shape[0] * x1.shape[1]
    return outs


def run_trunk(xs, cs, p, init_states, pasts):
    outs = [[] for _ in xs]
    shapes = [x.shape for x in xs]
    for l in range(DEPTH):
        halves = [mix_group(x.reshape(s), c, l, p, *st[l], past)
                  for x, s, c, st, past in zip(xs, shapes, cs, init_states, pasts)]
        for o, h in zip(outs, halves):
            o.append(h[4])
        xs = channel_mix(l, p, [h[0] for h in halves], [h[1] for h in halves], [h[2] for h in halves],
                         [h[3] for h in halves])
    stacked = [[jnp.stack([o[i] for o in og]) for i in range(7)] for og in outs]
    return [x.reshape(s) for x, s in zip(xs, shapes)], stacked


def kernel(x_prompt, x_sample, cache_k, cache_v, cache_logf, state_rwkv, state_shift, state_ssm,
           state_conv, page_table, c_prompt, c_sample, norm1_w, norm2_w, w_ada, b_ada, w_in, w_out,
           a_mu, a_w0, a_w2, a_a0, a_a2, a_g2, a_kk, a_ka, a_rk, a_ln_w, a_ln_b,
           b_conv_w, b_conv_b, b_dt_bias, b_a_log, b_d, b_norm_w,
           c_f_bias, c_qnorm_w, c_knorm_w, ffn_w_gate, ffn_w_up, ffn_w_down,
           moe_router_w, moe_router_b, moe_w_gate, moe_w_up, moe_w_down):
    p = dict(norm1_w=norm1_w, norm2_w=norm2_w, w_ada=w_ada, b_ada=b_ada, w_in=w_in, w_out=w_out,
             a_mu=a_mu, a_w0=a_w0, a_w2=a_w2, a_a0=a_a0, a_a2=a_a2, a_g2=a_g2, a_kk=a_kk, a_ka=a_ka,
             a_rk=a_rk, a_ln_w=a_ln_w, a_ln_b=a_ln_b, b_conv_w=b_conv_w, b_conv_b=b_conv_b,
             b_dt_bias=b_dt_bias, b_a_log=b_a_log, b_d=b_d, b_norm_w=b_norm_w, c_f_bias=c_f_bias,
             c_qnorm_w=c_qnorm_w, c_knorm_w=c_knorm_w, ffn_w_gate=ffn_w_gate, ffn_w_up=ffn_w_up,
             ffn_w_down=ffn_w_down, moe_router_w=moe_router_w, moe_router_b=moe_router_b,
             moe_w_gate=moe_w_gate, moe_w_up=moe_w_up, moe_w_down=moe_w_down)
    o_dt, o_q, o_f = A_IN + B_WIDTH + B_CONV_DIM, A_IN + B_IN, A_IN + B_IN + 3 * C_WIDTH
    p['w_in_perm'] = jnp.concatenate(
        [w_in[:, :, :o_dt], w_in[:, :, o_q:o_f], w_in[:, :, o_dt:o_q], w_in[:, :, o_f:],
         jnp.zeros((DEPTH, D_MODEL, LANE - B_HEADS - C_HEADS), w_in.dtype)], axis=-1).astype(BF16)
    p['w_out_bf16'] = w_out.astype(BF16)
    p['router_w_pad'] = jnp.pad(moe_router_w, ((0, 0), (0, 0), (0, LANE - N_EXPERTS))).astype(BF16)
    p['router_b_pad'] = jnp.pad(moe_router_b.astype(F32), ((0, 0), (0, LANE - N_EXPERTS)))[:, None, :]
    for name in ('ffn_w_gate', 'ffn_w_up', 'ffn_w_down'):
        p[name] = p[name].astype(BF16)
    for name in ('moe_w_gate', 'moe_w_up', 'moe_w_down'):
        p[name] = jnp.stack([cast_bf16(w) for w in p[name]]) if N_MOE > 1 else cast_bf16(p[name][0])[None]
    n_p = x_prompt.shape[0]
    zero_state = (jnp.zeros((n_p, A_HEADS, HEAD_DIM, HEAD_DIM), jnp.float32),
                  jnp.zeros((n_p, A_IN), x_prompt.dtype),
                  jnp.zeros((n_p, B_HEADS, HEAD_DIM, D_STATE), jnp.float32),
                  jnp.zeros((n_p, CONV_W - 1, B_CONV_DIM), x_prompt.dtype))
    init_s = [(state_rwkv[l], state_shift[l], state_ssm[l], state_conv[l]) for l in range(DEPTH)]
    (y_prompt, y_sample), (st_p, st_s) = run_trunk(
        [x_prompt, x_sample], [c_prompt, c_sample], p, [[zero_state] * DEPTH, init_s],
        [None, (cache_k, cache_v, cache_logf, page_table)])
    rwkv_prompt, shift_prompt, ssm_prompt, conv_prompt, k_prompt, v_prompt, logf_prompt = st_p
    rwkv_sample, shift_sample, ssm_sample, conv_sample, k_sample, v_sample, logf_sample = st_s
    return (y_prompt, y_sample,
            k_prompt, v_prompt, logf_prompt, rwkv_prompt, shift_prompt, ssm_prompt, conv_prompt,
            k_sample, v_sample, logf_sample, rwkv_sample, shift_sample, ssm_sample, conv_sample)
```

```python
import math
import numpy as np
import jax
import jax.numpy as jnp
import functools
from jax import lax
from jax.experimental import pallas as pl
from jax.experimental.pallas import tpu as pltpu

D_MODEL = 1024
BATCH = 4
SEQ = 4096
DEPTH = 2
DEC_BATCH = 128
DEC_SEQ = 4
PAST_LEN = 2048
PAGE_SIZE = 128

D_MIX = D_MODEL
HEAD_DIM = 64
A_WIDTH = D_MIX // 4
A_HEADS = A_WIDTH // HEAD_DIM
A_DECAY_LORA = 64
A_ICL_LORA = 64
A_GATE_LORA = 128
A_IN = 3 * A_WIDTH + A_DECAY_LORA + A_ICL_LORA + A_GATE_LORA
GN_EPS = HEAD_DIM * 1e-5
B_WIDTH = D_MIX // 2
B_HEADS = B_WIDTH // HEAD_DIM
B_GROUPS = 2
D_STATE = 128
CONV_W = 4
SSD_CHUNK = 128
B_CONV_DIM = B_WIDTH + 2 * B_GROUPS * D_STATE
B_IN = B_WIDTH + B_CONV_DIM + B_HEADS
C_WIDTH = D_MIX - A_WIDTH - B_WIDTH
C_HEADS = C_WIDTH // HEAD_DIM
C_IN = 3 * C_WIDTH + C_HEADS
Q_BLOCK = 128
IN_WIDTH = A_IN + B_IN + C_IN
D_FF = 2816
N_EXPERTS = 8
TOP_K = 2
D_FF_EXPERT = 3584
N_DENSE = (DEPTH + 1) // 2
N_MOE = DEPTH // 2
NORM_EPS = 1e-6


BF16 = jnp.bfloat16
F32 = jnp.float32
LANE = 128
VMEM_LIMIT = 48 * 1024 * 1024
ROW_TILE = 256
FFN_ROW_TILE = 512
FFN_TILE_DENSE = D_FF // 2
FFN_TILE_EXPERT = D_FF_EXPERT // 4
ATT_TILE = 1024
CAST_BLOCK_ELEMS = 1024 * 1024
IN_MAIN = A_IN + B_WIDTH + B_CONV_DIM + 3 * C_WIDTH
IN_PAD = IN_MAIN + LANE


def _mod_spec(mod, tile):
    if mod.shape[1] == 1:
        return pl.BlockSpec((1, 1, mod.shape[2]), lambda g, i: (g, 0, 0))
    return pl.BlockSpec((1, tile, mod.shape[2]), lambda g, i: (g, i, 0))


def _modulated_rms(x, nw, sc, sh):
    y = x * lax.rsqrt(jnp.mean(x * x, axis=-1, keepdims=True) + NORM_EPS) * nw
    return y * (1.0 + sc) + sh


def _cast_kernel(x_ref, o_ref):
    o_ref[...] = x_ref[...].astype(o_ref.dtype)


def cast_bf16(w):
    E, A, B = w.shape
    ta = next(t for t in (1024, 512, 256, 128, 64, 32, 16) if t * B <= CAST_BLOCK_ELEMS and A % t == 0)
    spec = pl.BlockSpec((1, ta, B), lambda e, i: (e, i, 0))
    return pl.pallas_call(
        _cast_kernel, grid=(E, A // ta), in_specs=[spec], out_specs=spec,
        out_shape=jax.ShapeDtypeStruct(w.shape, BF16),
        compiler_params=pltpu.CompilerParams(dimension_semantics=("parallel", "parallel")),
        name="cast_bf16",
    )(w)


def _gated_residual_kernel(x_ref, g_ref, f_ref, o_ref):
    o_ref[0] = x_ref[0] + g_ref[0] * f_ref[...]


def gated_residual(x, g, f, row0):
    G, R, d = x.shape
    t = min(R, 512)
    nt = R // t
    t0 = row0 // t
    return pl.pallas_call(
        _gated_residual_kernel,
        grid=(G, nt),
        in_specs=[pl.BlockSpec((1, t, d), lambda g, i: (g, i, 0)),
                  _mod_spec(g, t),
                  pl.BlockSpec((t, d), lambda g, i: (t0 + g * nt + i, 0))],
        out_specs=pl.BlockSpec((1, t, d), lambda g, i: (g, i, 0)),
        out_shape=jax.ShapeDtypeStruct(x.shape, x.dtype),
        name="gated_residual",
    )(x, g, f)


def _inproj_kernel(x_ref, nw_ref, sc_ref, sh_ref, w_ref, *out_refs):
    u = _modulated_rms(x_ref[0], nw_ref[...], sc_ref[0], sh_ref[0])
    p = jnp.dot(u.astype(BF16), w_ref[...], preferred_element_type=F32)
    o = 0
    for ref in out_refs:
        n = ref.shape[1]
        ref[...] = p[:, o:o + n]
        o += n


def in_projection(x, norm_w, sc, sh, w_perm):
    G, R, d = x.shape
    t = min(R, ROW_TILE)
    nt = R // t
    widths = (A_IN, B_WIDTH, B_CONV_DIM, C_WIDTH, C_WIDTH, C_WIDTH, LANE)
    return pl.pallas_call(
        _inproj_kernel,
        grid=(G, nt),
        in_specs=[pl.BlockSpec((1, t, d), lambda g, i: (g, i, 0)),
                  pl.BlockSpec((1, d), lambda g, i: (0, 0)),
                  _mod_spec(sc, t), _mod_spec(sh, t),
                  pl.BlockSpec((d, IN_PAD), lambda g, i: (0, 0))],
        out_specs=[pl.BlockSpec((t, n), lambda g, i: (g * nt + i, 0)) for n in widths],
        out_shape=[jax.ShapeDtypeStruct((G * R, n), F32) for n in widths],
        compiler_params=pltpu.CompilerParams(dimension_semantics=("parallel", "parallel"),
                                             vmem_limit_bytes=VMEM_LIMIT),
        name="in_projection",
    )(x, norm_w.reshape(1, d), sc, sh, w_perm)


def _outproj_kernel(ya_ref, yb_ref, yc_ref, w_ref, x_ref, g_ref, nw_ref, sc_ref, sh_ref, rw_ref, rb_ref,
                    x1_ref, u2_ref, lg_ref):
    acc = jnp.dot(ya_ref[...].astype(BF16), w_ref[0:A_WIDTH, :], preferred_element_type=F32)
    acc += jnp.dot(yb_ref[...].astype(BF16), w_ref[A_WIDTH:A_WIDTH + B_WIDTH, :], preferred_element_type=F32)
    acc += jnp.dot(yc_ref[...].astype(BF16), w_ref[A_WIDTH + B_WIDTH:, :], preferred_element_type=F32)
    x1 = x_ref[0] + g_ref[0] * acc
    x1_ref[...] = x1
    u2 = _modulated_rms(x1, nw_ref[...], sc_ref[0], sh_ref[0])
    u2_ref[...] = u2
    lg_ref[...] = jnp.dot(u2.astype(BF16), rw_ref[...], preferred_element_type=F32) + rb_ref[...]


def out_projection(ya, yb, yc, w_out, x, g1, norm_w, sc, sh, router_w, router_b):
    G, R, d = x.shape
    t = min(R, ROW_TILE)
    nt = R // t
    rows = lambda n: pl.BlockSpec((t, n), lambda g, i: (g * nt + i, 0))
    full = lambda a: pl.BlockSpec(a.shape, lambda g, i: (0,) * a.ndim)
    nw = norm_w.reshape(1, d)
    return pl.pallas_call(
        _outproj_kernel,
        grid=(G, nt),
        in_specs=[rows(A_WIDTH), rows(B_WIDTH), rows(C_WIDTH), full(w_out),
                  pl.BlockSpec((1, t, d), lambda g, i: (g, i, 0)), _mod_spec(g1, t), full(nw),
                  _mod_spec(sc, t), _mod_spec(sh, t), full(router_w), full(router_b)],
        out_specs=[rows(d), rows(d), rows(LANE)],
        out_shape=[jax.ShapeDtypeStruct((G * R, d), F32), jax.ShapeDtypeStruct((G * R, d), F32),
                   jax.ShapeDtypeStruct((G * R, LANE), F32)],
        compiler_params=pltpu.CompilerParams(dimension_semantics=("parallel", "parallel"),
                                             vmem_limit_bytes=VMEM_LIMIT),
        name="out_projection",
    )(ya, yb, yc, w_out, x, g1, nw, sc, sh, router_w, router_b)


def _swiglu_kernel(te_ref, tv_ref, u_ref, wg_ref, wu_ref, wd_ref, o_ref, acc_ref):
    i, j = pl.program_id(0), pl.program_id(1)
    last = pl.num_programs(1) - 1

    @pl.when(tv_ref[i] == 1)
    def _():
        u = u_ref[...].astype(BF16)
        g = jnp.dot(u, wg_ref[0], preferred_element_type=F32)
        up = jnp.dot(u, wu_ref[0], preferred_element_type=F32)
        h = (g * jax.nn.sigmoid(g) * up).astype(BF16)
        part = jnp.dot(h, wd_ref[0], preferred_element_type=F32)

        @pl.when(j == 0)
        def _():
            acc_ref[...] = part

        @pl.when(j > 0)
        def _():
            acc_ref[...] += part

    @pl.when(j == last)
    def _():
        o_ref[...] = jnp.where(tv_ref[i] == 1, acc_ref[...], 0.0)


def grouped_swiglu(u, w_gate, w_up, w_down, tile_expert, tile_valid, ff_tile):
    P, d = u.shape
    F = w_gate.shape[2]
    t = FFN_ROW_TILE
    grid_spec = pltpu.PrefetchScalarGridSpec(
        num_scalar_prefetch=2,
        grid=(P // t, F // ff_tile),
        in_specs=[pl.BlockSpec((t, d), lambda i, j, te, tv: (i, 0)),
                  pl.BlockSpec((1, d, ff_tile), lambda i, j, te, tv: (te[i], 0, j)),
                  pl.BlockSpec((1, d, ff_tile), lambda i, j, te, tv: (te[i], 0, j)),
                  pl.BlockSpec((1, ff_tile, d), lambda i, j, te, tv: (te[i], j, 0))],
        out_specs=pl.BlockSpec((t, d), lambda i, j, te, tv: (i, 0)),
        scratch_shapes=[pltpu.VMEM((t, d), F32)])
    return pl.pallas_call(
        _swiglu_kernel,
        grid_spec=grid_spec,
        out_shape=jax.ShapeDtypeStruct((P, d), F32),
        compiler_params=pltpu.CompilerParams(dimension_semantics=("parallel", "arbitrary"),
                                             vmem_limit_bytes=VMEM_LIMIT),
        name="grouped_swiglu",
    )(tile_expert, tile_valid, u, w_gate, w_up, w_down)


def _fox_flash_kernel(q_ref, k_ref, v_ref, cq_ref, ck_ref, o_ref, m_scr, l_scr, acc_scr):
    qi, ki = pl.program_id(1), pl.program_id(2)
    tq, tk, C = q_ref.shape[1], k_ref.shape[1], q_ref.shape[2]
    H = m_scr.shape[0]
    lane_head = lax.broadcasted_iota(jnp.int32, (tq, C), 1) // (C // H)

    @pl.when(ki == 0)
    def _():
        m_scr[...] = jnp.full(m_scr.shape, -jnp.inf, F32)
        l_scr[...] = jnp.zeros(l_scr.shape, F32)
        acc_scr[...] = jnp.zeros(acc_scr.shape, F32)

    def tile(on_diagonal):
        q, k, v = q_ref[0], k_ref[0], v_ref[0]
        acc = acc_scr[...]
        for h in range(H):
            mine = lane_head == h
            s = lax.dot_general(jnp.where(mine, q, jnp.zeros_like(q)), k, _NT, preferred_element_type=F32)
            s = s + cq_ref[0, h] - ck_ref[0, h]
            if on_diagonal:
                s = jnp.where(lax.broadcasted_iota(jnp.int32, (tq, tk), 1)
                              <= lax.broadcasted_iota(jnp.int32, (tq, tk), 0), s, -jnp.inf)
            m_prev = m_scr[h]
            m_new = jnp.maximum(m_prev, jnp.max(s, axis=-1, keepdims=True))
            alpha = jnp.exp(m_prev - m_new)
            p = jnp.exp(s - m_new)
            l_scr[h] = alpha * l_scr[h] + jnp.sum(p, axis=-1, keepdims=True)
            m_scr[h] = m_new
            pv = jnp.dot(p.astype(BF16), v, preferred_element_type=F32)
            acc = jnp.where(mine, alpha * acc + pv, acc)
        acc_scr[...] = acc

    pl.when(ki < qi)(functools.partial(tile, False))
    pl.when(ki == qi)(functools.partial(tile, True))

    @pl.when(ki == pl.num_programs(2) - 1)
    def _():
        l = jnp.zeros((tq, C), F32)
        for h in range(H):
            l = jnp.where(lane_head == h, l_scr[h], l)
        o_ref[0] = acc_scr[...] / l


def fox_flash_attention(q, k, v, cq, ck):
    N, L, C = q.shape
    H = cq.shape[1]
    t = min(L, ATT_TILE)
    nt = L // t
    kv_spec = pl.BlockSpec((1, t, C), lambda n, qi, ki: (n, jnp.minimum(ki, qi), 0))
    return pl.pallas_call(
        _fox_flash_kernel,
        grid=(N, nt, nt),
        in_specs=[pl.BlockSpec((1, t, C), lambda n, qi, ki: (n, qi, 0)), kv_spec, kv_spec,
                  pl.BlockSpec((1, H, t, 1), lambda n, qi, ki: (n, 0, qi, 0)),
                  pl.BlockSpec((1, H, 1, t), lambda n, qi, ki: (n, 0, 0, jnp.minimum(ki, qi)))],
        out_specs=pl.BlockSpec((1, t, C), lambda n, qi, ki: (n, qi, 0)),
        out_shape=jax.ShapeDtypeStruct((N, L, C), F32),
        scratch_shapes=[pltpu.VMEM((H, t, 1), F32), pltpu.VMEM((H, t, 1), F32), pltpu.VMEM((t, C), F32)],
        compiler_params=pltpu.CompilerParams(
            dimension_semantics=("parallel", "parallel", "arbitrary"), vmem_limit_bytes=VMEM_LIMIT),
        name="fox_flash_attention",
    )(q, k, v, cq, ck)


def _fox_prep_kernel(q_ref, k_ref, v_ref, qw_ref, kw_ref, qb_ref, kn_ref, kb_ref, vb_ref):
    C = q_ref.shape[1]
    same_head = (lax.broadcasted_iota(jnp.int32, (C, C), 0) // HEAD_DIM
                 == lax.broadcasted_iota(jnp.int32, (C, C), 1) // HEAD_DIM).astype(BF16)

    def head_norm(x, w):
        ms = _exact_rhs_mm(x * x, same_head) * (1.0 / HEAD_DIM)
        return x * lax.rsqrt(ms + NORM_EPS) * w

    qb_ref[...] = (head_norm(q_ref[...], qw_ref[...]) * (HEAD_DIM ** -0.5)).astype(BF16)
    kn = head_norm(k_ref[...], kw_ref[...])
    kn_ref[...] = kn
    kb_ref[...] = kn.astype(BF16)
    vb_ref[...] = v_ref[...].astype(BF16)


def fox_prep(q, k, v, qn_w, kn_w):
    T, C = q.shape
    t = min(T, 1024)
    rows = pl.BlockSpec((t, C), lambda i: (i, 0))
    w_spec = pl.BlockSpec((1, C), lambda i: (0, 0))
    tile_w = lambda w: jnp.tile(w.astype(F32), C // HEAD_DIM)[None, :]
    return pl.pallas_call(
        _fox_prep_kernel,
        grid=(T // t,),
        in_specs=[rows, rows, rows, w_spec, w_spec],
        out_specs=[rows, rows, rows, rows],
        out_shape=[jax.ShapeDtypeStruct((T, C), BF16), jax.ShapeDtypeStruct((T, C), F32),
                   jax.ShapeDtypeStruct((T, C), BF16), jax.ShapeDtypeStruct((T, C), BF16)],
        compiler_params=pltpu.CompilerParams(dimension_semantics=("parallel",)),
        name="fox_prep",
    )(q, k, v, tile_w(qn_w), tile_w(kn_w))


def _split(t, sizes):
    return jnp.split(t, np.cumsum(sizes)[:-1].tolist(), axis=-1)


def rms_norm(x, w):
    xf = x.astype(jnp.float32)
    y = xf * lax.rsqrt(jnp.mean(xf * xf, axis=-1, keepdims=True) + NORM_EPS)
    return (y * w.astype(jnp.float32)).astype(x.dtype)


def head_rms(t, w):
    tf = t.astype(jnp.float32)
    return (tf * lax.rsqrt(jnp.mean(tf * tf, axis=-1, keepdims=True) + NORM_EPS) * w).astype(t.dtype)


def ada_modulation(c, w_ada, b_ada):
    m = jax.nn.silu(c) @ w_ada + b_ada
    return jnp.split(m[:, None, :], 6, axis=-1)


def swiglu(t, w_gate, w_up, w_down):
    return (jax.nn.silu(t @ w_gate) * (t @ w_up)) @ w_down


def dense_swiglu(u, w_gate, w_up, w_down, ff_tile):
    nt = u.shape[0] // FFN_ROW_TILE
    return grouped_swiglu(u, w_gate[None], w_up[None], w_down[None],
                          jnp.zeros((nt,), jnp.int32), jnp.ones((nt,), jnp.int32), ff_tile)


def moe_swiglu(u, logits, w_gate, w_up, w_down, ff_tile):
    T, d = u.shape
    E = logits.shape[1]
    t = FFN_ROW_TILE
    top_val, top_idx = lax.top_k(logits, TOP_K)
    weights = jax.nn.softmax(top_val, axis=-1)
    eid = top_idx.reshape(-1).astype(jnp.int32)
    n_pairs = T * TOP_K
    order = jnp.argsort(eid, stable=True).astype(jnp.int32)
    counts = jnp.sum(jax.nn.one_hot(eid, E, dtype=jnp.int32), axis=0)
    padded = ((counts + t - 1) // t) * t
    ends_p = jnp.cumsum(padded)
    start_p = ends_p - padded
    start = jnp.cumsum(counts) - counts
    P = n_pairs + E * t
    tile_start = jnp.arange(P // t, dtype=jnp.int32) * t
    tile_expert = jnp.minimum(jnp.searchsorted(ends_p, tile_start, side='right'), E - 1).astype(jnp.int32)
    tile_valid = (tile_start < ends_p[-1]).astype(jnp.int32)
    per_row = lambda table: jnp.repeat(table[tile_expert], t)
    row_off = jnp.arange(P, dtype=jnp.int32) - per_row(start_p)
    row_ok = row_off < per_row(counts)
    src_pair = order[jnp.clip(per_row(start) + row_off, 0, n_pairs - 1)]
    src_tok = jnp.where(row_ok, src_pair // TOP_K, 0)
    rank = jnp.argsort(order).astype(jnp.int32)
    shift_e = jnp.sum(jax.nn.one_hot(eid, E, dtype=jnp.int32) * (start_p - start)[None, :], axis=1)
    pos_of_pair = (rank + shift_e).reshape(T, TOP_K)
    ys = grouped_swiglu(u[src_tok], w_gate, w_up, w_down, tile_expert, tile_valid, ff_tile)
    out = weights[:, 0:1] * ys[pos_of_pair[:, 0]]
    for s in range(1, TOP_K):
        out = out + weights[:, s:s + 1] * ys[pos_of_pair[:, s]]
    return out


RWKV_CHUNK = 64
RWKV_BLOCK = 256
RWKV_INV_BLOCK = 16

_NN = (((1,), (0,)), ((), ()))
_NT = (((1,), (1,)), ((), ()))


def _dot(a, b, dims):
    return lax.dot_general(a, b, dims, preferred_element_type=jnp.float32)


_BNN = (((2,), (1,)), ((0,), (0,)))
_BNT = (((2,), (2,)), ((0,), (0,)))


def _sp(x):
    hi = x.astype(BF16)
    lo = (x - hi.astype(F32)).astype(BF16)
    return hi, lo


def _mmp(a, b, dims=_BNN):
    (ah, al), (bh, bl) = a, b
    return _dot(ah, bh, dims) + _dot(ah, bl, dims) + _dot(al, bh, dims)


def _mm_exact_lhs(a_bf16, x):
    h1 = x.astype(BF16)
    r1 = x - h1.astype(F32)
    h2 = r1.astype(BF16)
    h3 = (r1 - h2.astype(F32)).astype(BF16)
    return _dot(a_bf16, h1, _BNN) + _dot(a_bf16, h2, _BNN) + _dot(a_bf16, h3, _BNN)


def _unit_lower_inverse(a, blk_mask, eye):
    ad = jnp.where(blk_mask, a, 0.0)
    e = a - ad
    ad_p = _sp(ad)
    p2 = _mmp(ad_p, ad_p)
    p2_p = _sp(p2)
    p4 = _mmp(p2_p, p2_p)
    p4_p = _sp(p4)
    p8 = _mmp(p4_p, p4_p)
    x = _mmp(_sp(eye - ad), _sp(eye + p2))
    x = _mmp(_sp(x), _sp(eye + p4))
    dinv_p = _sp(_mmp(_sp(x), _sp(eye + p8)))
    n = _mmp(dinv_p, _sp(e))
    n_p = _sp(n)
    n2 = _mmp(n_p, n_p)
    return _mmp(_sp(_mmp(_sp(eye - n), _sp(eye + n2))), dinv_p)


def _rwkv_chunk_prep(r, lw, k, v, vt, kk, a):
    B, T, _ = r.shape
    row = lax.broadcasted_iota(jnp.int32, (T, T), 0)
    col = lax.broadcasted_iota(jnp.int32, (T, T), 1)
    strict, incl = row > col, row >= col
    blk = (row // RWKV_INV_BLOCK) == (col // RWKV_INV_BLOCK)
    eye = (row == col).astype(F32)
    c = _mm_exact_lhs(jnp.broadcast_to(incl.astype(BF16), (B, T, T)), lw)
    c_last = c[:, T - 1:T, :]
    e_nc = jnp.exp(-c)
    e_end = jnp.exp(c_last - c)
    b = kk * a
    kt = _sp(kk * jnp.exp(c - lw))
    rt = _sp(r * jnp.exp(c))
    kh = _sp(k * e_nc)
    bh = _sp(b * e_nc)
    v_p, vt_p = _sp(v), _sp(vt)
    a_kb = jnp.where(strict, _mmp(kt, bh, _BNT), 0.0)
    a_kk = _sp(jnp.where(strict, _mmp(kt, kh, _BNT), 0.0))
    a_rk = _sp(jnp.where(incl, _mmp(rt, kh, _BNT), 0.0))
    a_rb = _sp(jnp.where(incl, _mmp(rt, bh, _BNT), 0.0))
    m = _sp(_unit_lower_inverse(a_kb, blk, eye))
    mk = _sp(_mmp(m, kt))
    g1 = _mmp(m, _sp(_mmp(a_kk, v_p)))
    g1t = _mmp(_sp(_mmp(vt_p, a_kk, _BNT)), m, _BNT)
    y0 = _mmp(a_rk, v_p)
    s_add = _mmp(vt_p, _sp(k * e_end))
    return dict(mk=mk, g1=g1, g1t=g1t, rt=rt, a_rb=a_rb, y0=y0, s_add=s_add, bbar=_sp(b * e_end),
                decay=jnp.exp(c_last))


def _rwkv_chunk_step(S, q, lo, hi):
    pick = lambda x: tuple(t[lo:hi] for t in x) if isinstance(x, tuple) else x[lo:hi]
    S_p, mk = _sp(S), pick(q['mk'])
    u = _mmp(mk, S_p, _BNT) + pick(q['g1'])
    y = _mmp(pick(q['rt']), S_p, _BNT) + pick(q['y0']) - _mmp(pick(q['a_rb']), _sp(u))
    ut = _mmp(S_p, mk, _BNT) + pick(q['g1t'])
    s_new = S * pick(q['decay']) + pick(q['s_add']) - _mmp(_sp(ut), pick(q['bbar']))
    return y, s_new


def _rwkv_kernel(r_ref, lw_ref, k_ref, v_ref, vt_ref, kk_ref, a_ref, s0_ref, y_ref, sT_ref, s_scr):
    H = r_ref.shape[1]
    T = RWKV_CHUNK

    @pl.when(pl.program_id(1) == 0)
    def _():
        s_scr[...] = s0_ref[0]

    J = RWKV_BLOCK // T
    rows = lambda ref: jnp.concatenate([ref[0, :, j * T:(j + 1) * T, :] for j in range(J)], axis=0)
    vt = jnp.concatenate([vt_ref[0, :, :, j * T:(j + 1) * T] for j in range(J)], axis=0)
    q = _rwkv_chunk_prep(rows(r_ref), rows(lw_ref), rows(k_ref), rows(v_ref), vt, rows(kk_ref), rows(a_ref))
    S = s_scr[...]
    for j in range(J):
        y, S = _rwkv_chunk_step(S, q, j * H, (j + 1) * H)
        y_ref[0, :, j * T:(j + 1) * T, :] = y
    s_scr[...] = S
    sT_ref[0] = S


def rwkv7_recurrence_pallas(r, lw, k, v, kk, a, s0):
    N, L, H, D = r.shape
    hm = lambda t: jnp.transpose(t, (0, 2, 1, 3))
    vt = jnp.transpose(v, (0, 2, 3, 1))
    row_spec = pl.BlockSpec((1, H, RWKV_BLOCK, D), lambda n, c: (n, 0, c, 0))
    st_spec = pl.BlockSpec((1, H, D, D), lambda n, c: (n, 0, 0, 0))
    y, s_last = pl.pallas_call(
        _rwkv_kernel,
        grid=(N, L // RWKV_BLOCK),
        in_specs=[row_spec, row_spec, row_spec, row_spec,
                  pl.BlockSpec((1, H, D, RWKV_BLOCK), lambda n, c: (n, 0, 0, c)),
                  row_spec, row_spec, st_spec],
        out_specs=[row_spec, st_spec],
        out_shape=[jax.ShapeDtypeStruct((N, H, L, D), jnp.float32),
                   jax.ShapeDtypeStruct((N, H, D, D), jnp.float32)],
        scratch_shapes=[pltpu.VMEM((H, D, D), jnp.float32)],
        compiler_params=pltpu.CompilerParams(dimension_semantics=("parallel", "arbitrary")),
        name="rwkv7_chunked",
    )(hm(r), hm(lw), hm(k), hm(v), vt, hm(kk), hm(a), s0)
    return jnp.transpose(y, (0, 2, 1, 3)), s_last


RWKV_SHORT_SEQS = 8


def _rwkv_short_kernel(r_ref, w_ref, k_ref, vt_ref, kk_ref, a_ref, s0_ref, yt_ref, sT_ref):
    nb, H, L, D = r_ref.shape
    B = nb * H
    lane_t = lax.broadcasted_iota(jnp.int32, (B, D, L), 2)
    S = s0_ref[...].reshape(B, D, D)
    yt = jnp.zeros((B, D, L), F32)
    for t in range(L):
        row = lambda ref: ref[:, :, t:t + 1, :].reshape(B, 1, D)
        kk_t = row(kk_ref)
        s_kk = jnp.sum(S * kk_t, axis=-1, keepdims=True)
        S = S * row(w_ref) - s_kk * (kk_t * row(a_ref)) + vt_ref[:, :, :, t:t + 1].reshape(B, D, 1) * row(k_ref)
        yt = jnp.where(lane_t == t, jnp.sum(S * row(r_ref), axis=-1, keepdims=True), yt)
    yt_ref[...] = yt.reshape(nb, H, D, L)
    sT_ref[...] = S.reshape(nb, H, D, D)


def rwkv7_recurrence_short(r, decay, k, v, kk, a, s0):
    N, L, H, D = r.shape
    nb = RWKV_SHORT_SEQS
    hm = lambda t: jnp.transpose(t, (0, 2, 1, 3))
    row_spec = pl.BlockSpec((nb, H, L, D), lambda n: (n, 0, 0, 0))
    col_spec = pl.BlockSpec((nb, H, D, L), lambda n: (n, 0, 0, 0))
    st_spec = pl.BlockSpec((nb, H, D, D), lambda n: (n, 0, 0, 0))
    yt, s_last = pl.pallas_call(
        _rwkv_short_kernel,
        grid=(N // nb,),
        in_specs=[row_spec, row_spec, row_spec, col_spec, row_spec, row_spec, st_spec],
        out_specs=[col_spec, st_spec],
        out_shape=[jax.ShapeDtypeStruct((N, H, D, L), F32), jax.ShapeDtypeStruct((N, H, D, D), F32)],
        compiler_params=pltpu.CompilerParams(dimension_semantics=("parallel",)),
        name="rwkv7_short",
    )(hm(r), hm(decay), hm(k), jnp.transpose(v, (0, 2, 3, 1)), hm(kk), hm(a), s0)
    return jnp.transpose(yt, (0, 3, 1, 2)), s_last


def rwkv7_recurrence(r, decay, k, v, kk, a, s0):
    def step(s, inp):
        r_t, w_t, k_t, v_t, kk_t, a_t = inp
        s_kk = jnp.einsum('nhvk,nhk->nhv', s, kk_t)
        s = (s * w_t[:, :, None, :]
             - s_kk[..., None] * (kk_t * a_t)[:, :, None, :]
             + v_t[..., None] * k_t[:, :, None, :])
        return s, jnp.einsum('nhvk,nhk->nhv', s, r_t)
    seq_first = tuple(jnp.moveaxis(t, 1, 0) for t in (r, decay, k, v, kk, a))
    s_last, ys = lax.scan(step, s0, seq_first)
    return jnp.moveaxis(ys, 0, 1), s_last


def rwkv7_mixer(cols, shift0, s0, mu, w0, w2, a0, a2, g2, k_k, k_a, r_k, ln_w, ln_b):
    n, L, _ = cols.shape
    f32 = jnp.float32
    prev = jnp.concatenate([shift0[:, None, :].astype(cols.dtype), cols[:, :-1]], axis=1)
    xs = cols + (prev - cols) * mu
    r, k, v, xw, xa, xg = _split(xs, [A_WIDTH, A_WIDTH, A_WIDTH, A_DECAY_LORA, A_ICL_LORA, A_GATE_LORA])
    w = -jax.nn.softplus(-(w0 + jnp.tanh(xw) @ w2)) - 0.5
    decay = jnp.exp(-jnp.exp(w.astype(f32)))
    a = jax.nn.sigmoid(a0 + xa @ a2)
    g = jax.nn.sigmoid(xg) @ g2
    heads = lambda t: t.astype(f32).reshape(n, L, A_HEADS, HEAD_DIM)
    kk = heads(k * k_k)
    kk = kk / jnp.maximum(jnp.sqrt(jnp.sum(kk * kk, axis=-1, keepdims=True)), 1e-12)
    k = k * (1.0 + (a - 1.0) * k_a)
    r_h, k_h, v_h = heads(r), heads(k), heads(v)
    if L % RWKV_BLOCK == 0:
        y, s_last = rwkv7_recurrence_pallas(r_h, heads(-jnp.exp(w.astype(f32))), k_h, v_h, kk, heads(a),
                                            s0.astype(f32))
    elif L <= SUBLANE and n % RWKV_SHORT_SEQS == 0:
        y, s_last = rwkv7_recurrence_short(r_h, heads(decay), k_h, v_h, kk, heads(a), s0.astype(f32))
    else:
        y, s_last = rwkv7_recurrence(r_h, heads(decay), k_h, v_h, kk, heads(a), s0.astype(f32))
    mean = jnp.mean(y, axis=-1, keepdims=True)
    var = jnp.mean(jnp.square(y - mean), axis=-1, keepdims=True)
    y = ((y - mean) * lax.rsqrt(var + GN_EPS)).reshape(n, L, A_WIDTH) * ln_w + ln_b
    bonus = jnp.sum(r_h * k_h * r_k, axis=-1, keepdims=True) * v_h
    y = (y + bonus.reshape(n, L, A_WIDTH)) * g
    return y.astype(cols.dtype), s_last, cols[:, -1]


def ssd_chunked(x, dt, A, B, C, h0):
    N, L, H, P = x.shape
    q = min(SSD_CHUNK, L)
    pad = (-L) % q
    if pad:
        padl = lambda t: jnp.pad(t, [(0, 0), (0, pad)] + [(0, 0)] * (t.ndim - 2))
        x, dt, B, C = padl(x), padl(dt), padl(B), padl(C)
    nc = (L + pad) // q
    ch = lambda t: t.reshape((N, nc, q) + t.shape[2:])
    xc, dtc, Bc, Cc = ch(x), ch(dt), ch(B), ch(C)
    cs = jnp.cumsum(dtc * A, axis=2)
    causal = jnp.tril(jnp.ones((q, q), bool))[None, None, :, :, None]
    seg = cs[:, :, :, None, :] - cs[:, :, None, :, :]
    decay = jnp.exp(jnp.where(causal, seg, -jnp.inf))
    scores = jnp.einsum('nclhd,ncshd->nclsh', Cc, Bc) * decay * dtc[:, :, None, :, :]
    y_intra = jnp.einsum('nclsh,ncshp->nclhp', scores, xc)
    to_end = jnp.exp(cs[:, :, -1:, :] - cs) * dtc
    chunk_states = jnp.einsum('ncsh,ncshd,ncshp->nchpd', to_end, Bc, xc)
    chunk_decay = jnp.exp(cs[:, :, -1, :])
    def carry(h, inp):
        dec, st = inp
        return h * dec[:, :, None, None] + st, h
    h_last, h_in = lax.scan(carry, h0, (jnp.moveaxis(chunk_decay, 1, 0), jnp.moveaxis(chunk_states, 1, 0)))
    h_in = jnp.moveaxis(h_in, 0, 1)
    y_inter = jnp.einsum('nclhd,nchpd->nclhp', Cc, h_in) * jnp.exp(cs)[..., None]
    y = (y_intra + y_inter).reshape(N, nc * q, H, P)[:, :L]
    return y, h_last


def mamba2_mixer(z, xbc, dt_raw, conv0, ssm0, conv_w, conv_b, dt_bias, a_log, d_skip, norm_w):
    n, L, _ = xbc.shape
    f32 = jnp.float32
    xpad = jnp.concatenate([conv0.astype(xbc.dtype), xbc], axis=1)
    conv = conv_b + sum(xpad[:, j:j + L] * conv_w[j] for j in range(CONV_W))
    xs, Bm, Cm = _split(jax.nn.silu(conv), [B_WIDTH, B_GROUPS * D_STATE, B_GROUPS * D_STATE])
    xh = xs.astype(f32).reshape(n, L, B_HEADS, HEAD_DIM)
    rep = B_HEADS // B_GROUPS
    Bh = jnp.repeat(Bm.astype(f32).reshape(n, L, B_GROUPS, D_STATE), rep, axis=2)
    Ch = jnp.repeat(Cm.astype(f32).reshape(n, L, B_GROUPS, D_STATE), rep, axis=2)
    dt = jax.nn.softplus(dt_raw.astype(f32) + dt_bias)
    A = -jnp.exp(a_log.astype(f32))
    y, h_last = ssd_chunked(xh, dt, A, Bh, Ch, ssm0.astype(f32))
    y = (y + d_skip[:, None] * xh).reshape(n, L, B_WIDTH) * jax.nn.silu(z.astype(f32))
    yg = y.reshape(n, L, B_GROUPS, B_WIDTH // B_GROUPS)
    yg = yg * lax.rsqrt(jnp.mean(yg * yg, axis=-1, keepdims=True) + NORM_EPS)
    y = yg.reshape(n, L, B_WIDTH) * norm_w
    return y.astype(z.dtype), h_last, xpad[:, -(CONV_W - 1):]


SSD_PAIRS = B_HEADS // 2
CONV_TAIL = 8


def _exact_rhs_mm(x, sel_bf16):
    h1 = x.astype(BF16)
    r1 = x - h1.astype(F32)
    h2 = r1.astype(BF16)
    h3 = (r1 - h2.astype(F32)).astype(BF16)
    d = lambda a: jnp.dot(a, sel_bf16, preferred_element_type=F32)
    return d(h1) + d(h2) + d(h3)


def _ssd_kernel(xbc_ref, z_ref, sm_ref, conv0_ref, h0_ref, cw_ref, cb_ref, dtb_ref, a_ref, dskip_ref, nw_ref,
                y_ref, hT_ref, xbuf, h_scr):
    Q = SSD_CHUNK
    c = pl.program_id(1)

    @pl.when(c == 0)
    def _():
        xbuf[0:CONV_TAIL, :] = conv0_ref[0]
        h_scr[...] = h0_ref[0]

    @pl.when(c > 0)
    def _():
        xbuf[0:CONV_TAIL, :] = xbuf[Q:Q + CONV_TAIL, :]

    xbuf[CONV_TAIL:CONV_TAIL + Q, :] = xbc_ref[0]
    conv = cb_ref[...]
    for j in range(CONV_W):
        conv = conv + cw_ref[j:j + 1, :] * xbuf[pl.ds(CONV_TAIL - (CONV_W - 1) + j, Q), :]
    act = conv * jax.nn.sigmoid(conv)
    xs = act[:, :B_WIDTH]
    n_bc = B_GROUPS * D_STATE
    Bm, Cm = act[:, B_WIDTH:B_WIDTH + n_bc], act[:, B_WIDTH + n_bc:]

    pre = sm_ref[0] + dtb_ref[...]
    dt = jnp.maximum(pre, 0.0) + jnp.log(1.0 + jnp.exp(-jnp.abs(pre)))
    dA = dt * a_ref[...]
    row = lax.broadcasted_iota(jnp.int32, (Q, Q), 0)
    col = lax.broadcasted_iota(jnp.int32, (Q, Q), 1)
    causal = row >= col
    tri = causal.astype(BF16)
    h1 = dA.astype(BF16)
    r1 = dA - h1.astype(F32)
    h2 = r1.astype(BF16)
    h3 = (r1 - h2.astype(F32)).astype(BF16)
    cs = (jnp.dot(tri, h1, preferred_element_type=F32) + jnp.dot(tri, h2, preferred_element_type=F32)
          + jnp.dot(tri, h3, preferred_element_type=F32))
    csT, dtT = cs.T, dt.T
    sel_b = (lax.broadcasted_iota(jnp.int32, (LANE, B_WIDTH), 0)
             == lax.broadcasted_iota(jnp.int32, (LANE, B_WIDTH), 1) // HEAD_DIM).astype(BF16)
    sel_h = (lax.broadcasted_iota(jnp.int32, (LANE, B_HEADS * Q), 0)
             == lax.broadcasted_iota(jnp.int32, (LANE, B_HEADS * Q), 1) // Q).astype(BF16)
    cs_b = _exact_rhs_mm(cs, sel_b)
    dt_b = _exact_rhs_mm(dt, sel_b)
    cs_full = _exact_rhs_mm(cs, sel_h)
    cs_last_b = cs_b[Q - 1:Q, :]
    e_b = jnp.exp(cs_b)
    xs_bf = xs.astype(BF16)
    xs_te = (xs * (jnp.exp(cs_last_b - cs_b) * dt_b)).astype(BF16)
    first_half = lax.broadcasted_iota(jnp.int32, (Q, LANE), 1) < HEAD_DIM
    ys = []
    for g in range(B_GROUPS):
        gs = slice(g * D_STATE, (g + 1) * D_STATE)
        Cg, Bg = Cm[:, gs].astype(BF16), Bm[:, gs]
        cb = lax.dot_general(Cg, Bg.astype(BF16), _NT, preferred_element_type=F32)
        BTg = Bg.T.astype(BF16)
        for pp in range(SSD_PAIRS // B_GROUPS):
            pair = g * (SSD_PAIRS // B_GROUPS) + pp
            lanes = slice(pair * LANE, (pair + 1) * LANE)
            hT = h_scr[pair]
            y_pair = jnp.dot(Cg, hT.astype(BF16), preferred_element_type=F32) * e_b[:, lanes]
            for j in range(2):
                h = 2 * pair + j
                seg = cs_full[:, h * Q:(h + 1) * Q] - csT[h:h + 1, :]
                dec = jnp.exp(jnp.where(causal, seg, -jnp.inf))
                sc = (cb * dec * dtT[h:h + 1, :]).astype(BF16)
                xm = jnp.where(first_half if j == 0 else jnp.logical_not(first_half), xs_bf[:, lanes], 0.0)
                y_pair = y_pair + jnp.dot(sc, xm.astype(BF16), preferred_element_type=F32)
            h_scr[pair] = hT * jnp.exp(cs_last_b[:, lanes]) + jnp.dot(BTg, xs_te[:, lanes],
                                                                      preferred_element_type=F32)
            ys.append(y_pair)
    y = jnp.concatenate(ys, axis=-1)
    z = z_ref[0]
    y = (y + dskip_ref[...] * xs) * (z * jax.nn.sigmoid(z))
    gw = B_WIDTH // B_GROUPS
    outs = []
    for g in range(B_GROUPS):
        yg = y[:, g * gw:(g + 1) * gw]
        outs.append(yg * lax.rsqrt(jnp.mean(yg * yg, axis=-1, keepdims=True) + NORM_EPS))
    y_ref[0] = jnp.concatenate(outs, axis=-1) * nw_ref[...]
    hT_ref[0] = h_scr[...]


def mamba2_mixer_pallas(z, xbc, small, conv0, ssm0, conv_w, conv_b, dt_bias, a_log, d_skip, norm_w):
    n, L, _ = xbc.shape
    Q = SSD_CHUNK
    pad_l = lambda v: jnp.pad(v.astype(F32), (0, LANE - v.shape[0]))[None, :]
    tail0 = jnp.pad(conv0.astype(F32), ((0, 0), (CONV_TAIL - (CONV_W - 1), 0), (0, 0)))
    h0 = ssm0.astype(F32).reshape(n, SSD_PAIRS, 2, HEAD_DIM, D_STATE)
    h0 = jnp.transpose(h0, (0, 1, 4, 2, 3)).reshape(n, SSD_PAIRS, D_STATE, 2 * HEAD_DIM)
    full = lambda a: pl.BlockSpec(a.shape, lambda i, c: (0,) * a.ndim)
    seq = lambda w: pl.BlockSpec((1, Q, w), lambda i, c: (i, c, 0))
    args = (xbc, z, small, tail0, h0, conv_w.astype(F32), conv_b.astype(F32)[None, :], pad_l(dt_bias),
            pad_l(-jnp.exp(a_log.astype(F32))), jnp.repeat(d_skip.astype(F32), HEAD_DIM)[None, :],
            norm_w.astype(F32)[None, :])
    st_spec = pl.BlockSpec((1, SSD_PAIRS, D_STATE, 2 * HEAD_DIM), lambda i, c: (i, 0, 0, 0))
    y, hT = pl.pallas_call(
        _ssd_kernel,
        grid=(n, L // Q),
        in_specs=[seq(B_CONV_DIM), seq(B_WIDTH), seq(LANE),
                  pl.BlockSpec((1, CONV_TAIL, B_CONV_DIM), lambda i, c: (i, 0, 0)), st_spec]
                 + [full(a) for a in args[5:]],
        out_specs=[seq(B_WIDTH), st_spec],
        out_shape=[jax.ShapeDtypeStruct((n, L, B_WIDTH), F32),
                   jax.ShapeDtypeStruct((n, SSD_PAIRS, D_STATE, 2 * HEAD_DIM), F32)],
        scratch_shapes=[pltpu.VMEM((Q + CONV_TAIL, B_CONV_DIM), F32),
                        pltpu.VMEM((SSD_PAIRS, D_STATE, 2 * HEAD_DIM), F32)],
        compiler_params=pltpu.CompilerParams(dimension_semantics=("parallel", "arbitrary"),
                                             vmem_limit_bytes=VMEM_LIMIT),
        name="mamba2_ssd",
    )(*args)
    h_last = jnp.transpose(hT.reshape(n, SSD_PAIRS, D_STATE, 2, HEAD_DIM), (0, 1, 3, 4, 2))
    h_last = h_last.reshape(n, B_HEADS, HEAD_DIM, D_STATE)
    tail = jnp.concatenate([conv0.astype(xbc.dtype), xbc[:, -(CONV_W - 1):]], axis=1)[:, -(CONV_W - 1):]
    return y, h_last, tail


def fox_attention(q, k, v, logf, past):
    n, L, H, hd = q.shape
    f32 = jnp.float32
    cum = jnp.cumsum(logf, axis=1)
    if past is None:
        P = 0
        keys_k, keys_v, key_c = k, v, cum
    else:
        pool_k, pool_v, pool_logf, page_table = past
        P = page_table.shape[1] * PAGE_SIZE
        gather = lambda pool: pool[page_table].reshape((n, P) + pool.shape[2:])
        plf = gather(pool_logf).astype(f32)
        suffix = jnp.cumsum(plf[:, ::-1], axis=1)[:, ::-1] - plf
        keys_k = jnp.concatenate([gather(pool_k).astype(k.dtype), k], axis=1)
        keys_v = jnp.concatenate([gather(pool_v).astype(v.dtype), v], axis=1)
        key_c = jnp.concatenate([-suffix, cum], axis=1)
    kc = jnp.moveaxis(key_c, 2, 1)
    kpos = jnp.arange(P + L)
    qb = Q_BLOCK if L % Q_BLOCK == 0 else L
    nb = L // qb
    scale = hd ** -0.5

    def block(args):
        q_blk, c_blk, t0 = args
        s = jnp.einsum('nqhd,nkhd->nhqk', q_blk, keys_k, preferred_element_type=f32) * scale
        s = s + jnp.moveaxis(c_blk, 2, 1)[..., None] - kc[:, :, None, :]
        mask = kpos[None, :] <= (t0 + jnp.arange(qb))[:, None]
        prob = jax.nn.softmax(jnp.where(mask, s, -jnp.inf), axis=-1)
        return jnp.einsum('nhqk,nkhd->nqhd', prob.astype(keys_v.dtype), keys_v)

    q_blocks = jnp.moveaxis(q.reshape(n, nb, qb, H, hd), 1, 0)
    c_blocks = jnp.moveaxis(cum.reshape(n, nb, qb, H), 1, 0)
    starts = P + jnp.arange(nb) * qb
    out = lax.map(block, (q_blocks, c_blocks, starts))
    return jnp.moveaxis(out, 0, 1).reshape(n, L, H, hd)


SUBLANE = 8


def _fox_paged_kernel(pt_ref, qbd_ref, knew_ref, vnew_ref, cq_ref, ckn_ref, *refs, n_pages, n_new):
    k_refs, v_refs, lf_refs = refs[:n_pages], refs[n_pages:2 * n_pages], refs[2 * n_pages:3 * n_pages]
    o_ref = refs[3 * n_pages]
    PS = k_refs[0].shape[2]
    qbd = qbd_ref[0]
    cq = cq_ref[0]
    lf = jnp.concatenate([r[0] for r in lf_refs], axis=0)
    after = (lax.broadcasted_iota(jnp.int32, (PS, PS), 0)
             > lax.broadcasted_iota(jnp.int32, (PS, PS), 1)).astype(BF16)
    suf = _exact_rhs_mm(lf, after)
    page_sum = jnp.sum(lf, axis=-1, keepdims=True)
    tails = [jnp.zeros((SUBLANE, 1), F32)]
    for j in range(n_pages - 1, 0, -1):
        tails.append(tails[-1] + page_sum[j * SUBLANE:(j + 1) * SUBLANE])
    tails = tails[::-1]
    scores = []
    for j in range(n_pages):
        s = jnp.dot(qbd, k_refs[j][0].astype(BF16), preferred_element_type=F32)
        brow = suf[j * SUBLANE:(j + 1) * SUBLANE] + tails[j]
        bias = jnp.concatenate([jnp.broadcast_to(brow[h:h + 1], (SUBLANE, PS)) for h in range(C_HEADS)], axis=0)
        scores.append(s + cq + bias)
    sn = lax.dot_general(qbd, knew_ref[0], _NT, preferred_element_type=F32)
    t_idx = lax.broadcasted_iota(jnp.int32, sn.shape, 0) % SUBLANE
    s_idx = lax.broadcasted_iota(jnp.int32, sn.shape, 1)
    sn = jnp.where((s_idx <= t_idx) & (s_idx < n_new), sn + cq - ckn_ref[0], -jnp.inf)
    m = jnp.max(sn, axis=-1, keepdims=True)
    for s in scores:
        m = jnp.maximum(m, jnp.max(s, axis=-1, keepdims=True))
    pn = jnp.exp(sn - m)
    l = jnp.sum(pn, axis=-1, keepdims=True)
    acc = jnp.dot(pn.astype(BF16), vnew_ref[0], preferred_element_type=F32)
    for j in range(n_pages):
        pj = jnp.exp(scores[j] - m)
        l = l + jnp.sum(pj, axis=-1, keepdims=True)
        acc = acc + lax.dot_general(pj.astype(BF16), v_refs[j][0].astype(BF16), _NT, preferred_element_type=F32)
    out = acc / l
    lane_head = lax.broadcasted_iota(jnp.int32, (SUBLANE, C_WIDTH), 1) // HEAD_DIM
    o = jnp.zeros((SUBLANE, C_WIDTH), F32)
    for h in range(C_HEADS):
        o = o + jnp.where(lane_head == h, out[h * SUBLANE:(h + 1) * SUBLANE], 0.0)
    o_ref[0] = o


def fox_paged_attention(qh, kh, vh, logf, pools_k, pools_v, pools_logf, page_table, layer):
    n, L, H, hd = qh.shape
    depth, n_phys, PS = pools_k.shape[:3]
    n_pages = page_table.shape[1]
    R = H * SUBLANE
    pad_t = lambda t: jnp.pad(t, ((0, 0), (0, 0), (0, SUBLANE - L)) + ((0, 0),) * (t.ndim - 3))
    q_hm = pad_t(jnp.transpose(qh * (hd ** -0.5), (0, 2, 1, 3)))
    qbd = (jnp.eye(H, dtype=F32)[None, :, None, :, None] * q_hm[:, :, :, None, :]).reshape(n, R, H * hd)
    new_rows = lambda t: jnp.pad(t.reshape(n, L, H * hd), ((0, 0), (0, SUBLANE - L), (0, 0))).astype(BF16)
    cum = pad_t(jnp.transpose(jnp.cumsum(logf, axis=1), (0, 2, 1)))
    cq = cum.reshape(n, R, 1)
    ckn = jnp.broadcast_to(cum[:, :, None, :], (n, H, SUBLANE, SUBLANE)).reshape(n, R, SUBLANE)
    lf_t = jnp.pad(jnp.transpose(pools_logf[layer].astype(F32), (0, 2, 1)), ((0, 0), (0, SUBLANE - H), (0, 0)))
    pk, pv = (jnp.transpose(t, (0, 1, 3, 4, 2)).reshape(depth * n_phys, H * hd, PS) for t in (pools_k, pools_v))
    base = layer * n_phys
    seq = lambda r, c: pl.BlockSpec((1, r, c), lambda i, pt: (i, 0, 0))
    page = lambda j, r, c, off: pl.BlockSpec((1, r, c), lambda i, pt, j=j: (pt[i, j] + off, 0, 0))
    grid_spec = pltpu.PrefetchScalarGridSpec(
        num_scalar_prefetch=1,
        grid=(n,),
        in_specs=[seq(R, H * hd), seq(SUBLANE, H * hd), seq(SUBLANE, H * hd), seq(R, 1), seq(R, SUBLANE)]
                 + [page(j, H * hd, PS, base) for j in range(n_pages)] * 2
                 + [page(j, SUBLANE, PS, 0) for j in range(n_pages)],
        out_specs=seq(SUBLANE, H * hd))
    o = pl.pallas_call(
        functools.partial(_fox_paged_kernel, n_pages=n_pages, n_new=L),
        grid_spec=grid_spec,
        out_shape=jax.ShapeDtypeStruct((n, SUBLANE, H * hd), F32),
        compiler_params=pltpu.CompilerParams(dimension_semantics=("parallel",), vmem_limit_bytes=VMEM_LIMIT),
        name="fox_paged_attention",
    )(page_table, qbd.astype(BF16), new_rows(kh), new_rows(vh), cq, ckn,
      *([pk] * n_pages), *([pv] * n_pages), *([lf_t] * n_pages))
    return o[:, :L]


def fox_mixer(q, k, v, f_raw, f_bias, qn_w, kn_w, past, layer):
    n, L, _ = q.shape
    vh = v.reshape(n, L, C_HEADS, HEAD_DIM)
    logf = jax.nn.log_sigmoid(f_raw.astype(jnp.float32) + f_bias)
    if past is None and L % ATT_TILE == 0:
        fl = lambda t: t.reshape(n * L, C_WIDTH)
        qb, kn, kb, vb = (t.reshape(n, L, C_WIDTH) for t in fox_prep(fl(q), fl(k), fl(v), qn_w, kn_w))
        cum = jnp.transpose(jnp.cumsum(logf, axis=1), (0, 2, 1))
        o = fox_flash_attention(qb, kb, vb, cum[..., None], cum[:, :, None, :])
        return o.astype(q.dtype), kn.reshape(n, L, C_HEADS, HEAD_DIM), vh, logf
    qh = head_rms(q.reshape(n, L, C_HEADS, HEAD_DIM), qn_w)
    kh = head_rms(k.reshape(n, L, C_HEADS, HEAD_DIM), kn_w)
    if past is not None and L <= SUBLANE:
        o = fox_paged_attention(qh, kh, vh, logf, *past, layer)
    else:
        o = fox_attention(qh, kh, vh, logf,
                          None if past is None else (past[0][layer], past[1][layer], past[2][layer], past[3]))
    return o.reshape(n, L, C_WIDTH).astype(q.dtype), kh, vh, logf


def mix_group(x, c, l, p, s0, shift0, ssm0, conv0, past):
    n, L, d = x.shape
    mods = ada_modulation(c, p['w_ada'][l], p['b_ada'][l])
    if L >= ROW_TILE:
        xg = x
    else:
        xg = x.reshape(1, n * L, d)
        mods = [jnp.broadcast_to(m, (n, L, d)).reshape(1, n * L, d) for m in mods]
    sh1, sc1, g1, sh2, sc2, g2 = mods
    a_cols, b_z, b_xbc, c_q, c_k, c_v, small = in_projection(xg, p['norm1_w'][l], sc1, sh1, p['w_in_perm'][l])
    rs = lambda t: t.reshape(n, L, t.shape[-1])
    a_cols, b_z, b_xbc = rs(a_cols), rs(b_z), rs(b_xbc)
    b_dt, c_f = rs(small[:, :B_HEADS]), rs(small[:, B_HEADS:B_HEADS + C_HEADS])
    c_q, c_k, c_v = rs(c_q), rs(c_k), rs(c_v)
    y_a, s_new, shift_new = rwkv7_mixer(
        a_cols, shift0, s0, p['a_mu'][l], p['a_w0'][l], p['a_w2'][l], p['a_a0'][l], p['a_a2'][l],
        p['a_g2'][l], p['a_kk'][l], p['a_ka'][l], p['a_rk'][l], p['a_ln_w'][l], p['a_ln_b'][l])
    mamba = mamba2_mixer_pallas if L % SSD_CHUNK == 0 else mamba2_mixer
    y_b, ssm_new, conv_new = mamba(
        b_z, b_xbc, rs(small) if L % SSD_CHUNK == 0 else b_dt, conv0, ssm0, p['b_conv_w'][l], p['b_conv_b'][l],
        p['b_dt_bias'][l], p['b_a_log'][l], p['b_d'][l], p['b_norm_w'][l])
    y_c, k_rows, v_rows, logf_rows = fox_mixer(
        c_q, c_k, c_v, c_f, p['c_f_bias'][l], p['c_qnorm_w'][l], p['c_knorm_w'][l], past, l)
    fl = lambda t: t.reshape(n * L, t.shape[-1])
    x1, u2, logits = out_projection(fl(y_a), fl(y_b), fl(y_c), p['w_out_bf16'][l], xg, g1, p['norm2_w'][l],
                                    sc2, sh2, p['router_w_pad'][l // 2], p['router_b_pad'][l // 2])
    dt = x.dtype
    states = (s_new.astype(dt), shift_new.astype(dt), ssm_new.astype(dt), conv_new.astype(dt),
              k_rows.astype(dt), v_rows.astype(dt), logf_rows.astype(dt))
    return x1.reshape(xg.shape), u2, logits, g2, states


def channel_mix(l, p, x1s, u2s, logits, g2s):
    u2 = jnp.concatenate(u2s, axis=0)
    j = l // 2
    if l % 2 == 0:
        f = dense_swiglu(u2, p['ffn_w_gate'][j], p['ffn_w_up'][j], p['ffn_w_down'][j], FFN_TILE_DENSE)
    else:
        lg = jnp.concatenate(logits, axis=0)[:, :N_EXPERTS]
        f = moe_swiglu(u2, lg, p['moe_w_gate'][j], p['moe_w_up'][j], p['moe_w_down'][j], FFN_TILE_EXPERT)
    outs, row0 = [], 0
    for x1, g2 in zip(x1s, g2s):
        outs.append(gated_residual(x1, g2, f, row0))
        row0 += x1.shape[0] * x1.shape[1]
    return outs


def run_trunk(xs, cs, p, init_states, pasts):
    outs = [[] for _ in xs]
    shapes = [x.shape for x in xs]
    for l in range(DEPTH):
        halves = [mix_group(x.reshape(s), c, l, p, *st[l], past)
                  for x, s, c, st, past in zip(xs, shapes, cs, init_states, pasts)]
        for o, h in zip(outs, halves):
            o.append(h[4])
        xs = channel_mix(l, p, [h[0] for h in halves], [h[1] for h in halves], [h[2] for h in halves],
                         [h[3] for h in halves])
    stacked = [[jnp.stack([o[i] for o in og]) for i in range(7)] for og in outs]
    return [x.reshape(s) for x, s in zip(xs, shapes)], stacked


def kernel(x_prompt, x_sample, cache_k, cache_v, cache_logf, state_rwkv, state_shift, state_ssm,
           state_conv, page_table, c_prompt, c_sample, norm1_w, norm2_w, w_ada, b_ada, w_in, w_out,
           a_mu, a_w0, a_w2, a_a0, a_a2, a_g2, a_kk, a_ka, a_rk, a_ln_w, a_ln_b,
           b_conv_w, b_conv_b, b_dt_bias, b_a_log, b_d, b_norm_w,
           c_f_bias, c_qnorm_w, c_knorm_w, ffn_w_gate, ffn_w_up, ffn_w_down,
           moe_router_w, moe_router_b, moe_w_gate, moe_w_up, moe_w_down):
    p = dict(norm1_w=norm1_w, norm2_w=norm2_w, w_ada=w_ada, b_ada=b_ada, w_in=w_in, w_out=w_out,
             a_mu=a_mu, a_w0=a_w0, a_w2=a_w2, a_a0=a_a0, a_a2=a_a2, a_g2=a_g2, a_kk=a_kk, a_ka=a_ka,
             a_rk=a_rk, a_ln_w=a_ln_w, a_ln_b=a_ln_b, b_conv_w=b_conv_w, b_conv_b=b_conv_b,
             b_dt_bias=b_dt_bias, b_a_log=b_a_log, b_d=b_d, b_norm_w=b_norm_w, c_f_bias=c_f_bias,
             c_qnorm_w=c_qnorm_w, c_knorm_w=c_knorm_w, ffn_w_gate=ffn_w_gate, ffn_w_up=ffn_w_up,
             ffn_w_down=ffn_w_down, moe_router_w=moe_router_w, moe_router_b=moe_router_b,
             moe_w_gate=moe_w_gate, moe_w_up=moe_w_up, moe_w_down=moe_w_down)
    o_dt, o_q, o_f = A_IN + B_WIDTH + B_CONV_DIM, A_IN + B_IN, A_IN + B_IN + 3 * C_WIDTH
    p['w_in_perm'] = jnp.concatenate(
        [w_in[:, :, :o_dt], w_in[:, :, o_q:o_f], w_in[:, :, o_dt:o_q], w_in[:, :, o_f:],
         jnp.zeros((DEPTH, D_MODEL, LANE - B_HEADS - C_HEADS), w_in.dtype)], axis=-1).astype(BF16)
    p['w_out_bf16'] = w_out.astype(BF16)
    p['router_w_pad'] = jnp.pad(moe_router_w, ((0, 0), (0, 0), (0, LANE - N_EXPERTS))).astype(BF16)
    p['router_b_pad'] = jnp.pad(moe_router_b.astype(F32), ((0, 0), (0, LANE - N_EXPERTS)))[:, None, :]
    for name in ('ffn_w_gate', 'ffn_w_up', 'ffn_w_down'):
        p[name] = p[name].astype(BF16)
    for name in ('moe_w_gate', 'moe_w_up', 'moe_w_down'):
        p[name] = jnp.stack([cast_bf16(w) for w in p[name]]) if N_MOE > 1 else cast_bf16(p[name][0])[None]
    n_p = x_prompt.shape[0]
    zero_state = (jnp.zeros((n_p, A_HEADS, HEAD_DIM, HEAD_DIM), jnp.float32),
                  jnp.zeros((n_p, A_IN), x_prompt.dtype),
                  jnp.zeros((n_p, B_HEADS, HEAD_DIM, D_STATE), jnp.float32),
                  jnp.zeros((n_p, CONV_W - 1, B_CONV_DIM), x_prompt.dtype))
    init_s = [(state_rwkv[l], state_shift[l], state_ssm[l], state_conv[l]) for l in range(DEPTH)]
    (y_prompt, y_sample), (st_p, st_s) = run_trunk(
        [x_prompt, x_sample], [c_prompt, c_sample], p, [[zero_state] * DEPTH, init_s],
        [None, (cache_k, cache_v, cache_logf, page_table)])
    rwkv_prompt, shift_prompt, ssm_prompt, conv_prompt, k_prompt, v_prompt, logf_prompt = st_p
    rwkv_sample, shift_sample, ssm_sample, conv_sample, k_sample, v_sample, logf_sample = st_s
    return (y_prompt, y_sample,
            k_prompt, v_prompt, logf_prompt, rwkv_prompt, shift_prompt, ssm_prompt, conv_prompt,
            k_sample, v_sample, logf_sample, rwkv_sample, shift_sample, ssm_sample, conv_sample)
```

```python
import math
import numpy as np
import jax
import jax.numpy as jnp
import functools
from jax import lax
from jax.experimental import pallas as pl
from jax.experimental.pallas import tpu as pltpu

D_MODEL = 1024
BATCH = 4
SEQ = 4096
DEPTH = 2
DEC_BATCH = 128
DEC_SEQ = 4
PAST_LEN = 2048
PAGE_SIZE = 128

D_MIX = D_MODEL
HEAD_DIM = 64
A_WIDTH = D_MIX // 4
A_HEADS = A_WIDTH // HEAD_DIM
A_DECAY_LORA = 64
A_ICL_LORA = 64
A_GATE_LORA = 128
A_IN = 3 * A_WIDTH + A_DECAY_LORA + A_ICL_LORA + A_GATE_LORA
GN_EPS = HEAD_DIM * 1e-5
B_WIDTH = D_MIX // 2
B_HEADS = B_WIDTH // HEAD_DIM
B_GROUPS = 2
D_STATE = 128
CONV_W = 4
SSD_CHUNK = 128
B_CONV_DIM = B_WIDTH + 2 * B_GROUPS * D_STATE
B_IN = B_WIDTH + B_CONV_DIM + B_HEADS
C_WIDTH = D_MIX - A_WIDTH - B_WIDTH
C_HEADS = C_WIDTH // HEAD_DIM
C_IN = 3 * C_WIDTH + C_HEADS
Q_BLOCK = 128
IN_WIDTH = A_IN + B_IN + C_IN
D_FF = 2816
N_EXPERTS = 8
TOP_K = 2
D_FF_EXPERT = 3584
N_DENSE = (DEPTH + 1) // 2
N_MOE = DEPTH // 2
NORM_EPS = 1e-6


BF16 = jnp.bfloat16
F32 = jnp.float32
LANE = 128
VMEM_LIMIT = 48 * 1024 * 1024
ROW_TILE = 256
FFN_ROW_TILE = 512
FFN_TILE_DENSE = D_FF // 2
FFN_TILE_EXPERT = D_FF_EXPERT // 4
ATT_TILE = 1024
CAST_BLOCK_ELEMS = 1024 * 1024
IN_MAIN = A_IN + B_WIDTH + B_CONV_DIM + 3 * C_WIDTH
IN_PAD = IN_MAIN + LANE


def _mod_spec(mod, tile):
    if mod.shape[1] == 1:
        return pl.BlockSpec((1, 1, mod.shape[2]), lambda g, i: (g, 0, 0))
    return pl.BlockSpec((1, tile, mod.shape[2]), lambda g, i: (g, i, 0))


def _modulated_rms(x, nw, sc, sh):
    y = x * lax.rsqrt(jnp.mean(x * x, axis=-1, keepdims=True) + NORM_EPS) * nw
    return y * (1.0 + sc) + sh


def _cast_kernel(x_ref, o_ref):
    o_ref[...] = x_ref[...].astype(o_ref.dtype)


def cast_bf16(w):
    E, A, B = w.shape
    ta = next(t for t in (1024, 512, 256, 128, 64, 32, 16) if t * B <= CAST_BLOCK_ELEMS and A % t == 0)
    spec = pl.BlockSpec((1, ta, B), lambda e, i: (e, i, 0))
    return pl.pallas_call(
        _cast_kernel, grid=(E, A // ta), in_specs=[spec], out_specs=spec,
        out_shape=jax.ShapeDtypeStruct(w.shape, BF16),
        compiler_params=pltpu.CompilerParams(dimension_semantics=("parallel", "parallel")),
        name="cast_bf16",
    )(w)


def _gated_residual_kernel(x_ref, g_ref, f_ref, o_ref):
    o_ref[0] = x_ref[0] + g_ref[0] * f_ref[...]


def gated_residual(x, g, f, row0):
    G, R, d = x.shape
    t = min(R, FFN_ROW_TILE)
    nt = R // t
    t0 = row0 // t
    return pl.pallas_call(
        _gated_residual_kernel,
        grid=(G, nt),
        in_specs=[pl.BlockSpec((1, t, d), lambda g, i: (g, i, 0)),
                  _mod_spec(g, t),
                  pl.BlockSpec((t, d), lambda g, i: (t0 + g * nt + i, 0))],
        out_specs=pl.BlockSpec((1, t, d), lambda g, i: (g, i, 0)),
        out_shape=jax.ShapeDtypeStruct(x.shape, x.dtype),
        name="gated_residual",
    )(x, g, f)


def _inproj_kernel(x_ref, nw_ref, sc_ref, sh_ref, w_ref, *out_refs):
    u = _modulated_rms(x_ref[0], nw_ref[...], sc_ref[0], sh_ref[0])
    p = jnp.dot(u.astype(BF16), w_ref[...], preferred_element_type=F32)
    o = 0
    for ref in out_refs:
        n = ref.shape[1]
        ref[...] = p[:, o:o + n]
        o += n


def in_projection(x, norm_w, sc, sh, w_perm):
    G, R, d = x.shape
    t = min(R, ROW_TILE)
    nt = R // t
    widths = (A_IN, B_WIDTH, B_CONV_DIM, C_WIDTH, C_WIDTH, C_WIDTH, LANE)
    return pl.pallas_call(
        _inproj_kernel,
        grid=(G, nt),
        in_specs=[pl.BlockSpec((1, t, d), lambda g, i: (g, i, 0)),
                  pl.BlockSpec((1, d), lambda g, i: (0, 0)),
                  _mod_spec(sc, t), _mod_spec(sh, t),
                  pl.BlockSpec((d, IN_PAD), lambda g, i: (0, 0))],
        out_specs=[pl.BlockSpec((t, n), lambda g, i: (g * nt + i, 0)) for n in widths],
        out_shape=[jax.ShapeDtypeStruct((G * R, n), F32) for n in widths],
        compiler_params=pltpu.CompilerParams(dimension_semantics=("parallel", "parallel"),
                                             vmem_limit_bytes=VMEM_LIMIT),
        name="in_projection",
    )(x, norm_w.reshape(1, d), sc, sh, w_perm)


def _outproj_kernel(ya_ref, yb_ref, yc_ref, w_ref, x_ref, g_ref, nw_ref, sc_ref, sh_ref, rw_ref, rb_ref,
                    x1_ref, u2_ref, lg_ref):
    acc = jnp.dot(ya_ref[...].astype(BF16), w_ref[0:A_WIDTH, :], preferred_element_type=F32)
    acc += jnp.dot(yb_ref[...].astype(BF16), w_ref[A_WIDTH:A_WIDTH + B_WIDTH, :], preferred_element_type=F32)
    acc += jnp.dot(yc_ref[...].astype(BF16), w_ref[A_WIDTH + B_WIDTH:, :], preferred_element_type=F32)
    x1 = x_ref[0] + g_ref[0] * acc
    x1_ref[...] = x1
    u2 = _modulated_rms(x1, nw_ref[...], sc_ref[0], sh_ref[0])
    u2_ref[...] = u2
    lg_ref[...] = jnp.dot(u2.astype(BF16), rw_ref[...], preferred_element_type=F32) + rb_ref[...]


def out_projection(ya, yb, yc, w_out, x, g1, norm_w, sc, sh, router_w, router_b):
    G, R, d = x.shape
    t = min(R, ROW_TILE)
    nt = R // t
    rows = lambda n: pl.BlockSpec((t, n), lambda g, i: (g * nt + i, 0))
    full = lambda a: pl.BlockSpec(a.shape, lambda g, i: (0,) * a.ndim)
    nw = norm_w.reshape(1, d)
    return pl.pallas_call(
        _outproj_kernel,
        grid=(G, nt),
        in_specs=[rows(A_WIDTH), rows(B_WIDTH), rows(C_WIDTH), full(w_out),
                  pl.BlockSpec((1, t, d), lambda g, i: (g, i, 0)), _mod_spec(g1, t), full(nw),
                  _mod_spec(sc, t), _mod_spec(sh, t), full(router_w), full(router_b)],
        out_specs=[rows(d), rows(d), rows(LANE)],
        out_shape=[jax.ShapeDtypeStruct((G * R, d), F32), jax.ShapeDtypeStruct((G * R, d), F32),
                   jax.ShapeDtypeStruct((G * R, LANE), F32)],
        compiler_params=pltpu.CompilerParams(dimension_semantics=("parallel", "parallel"),
                                             vmem_limit_bytes=VMEM_LIMIT),
        name="out_projection",
    )(ya, yb, yc, w_out, x, g1, nw, sc, sh, router_w, router_b)


def _swiglu_kernel(te_ref, tv_ref, u_ref, wg_ref, wu_ref, wd_ref, o_ref, acc_ref):
    i, j = pl.program_id(0), pl.program_id(1)
    last = pl.num_programs(1) - 1

    @pl.when(tv_ref[i] == 1)
    def _():
        u = u_ref[...].astype(BF16)
        g = jnp.dot(u, wg_ref[0], preferred_element_type=F32)
        up = jnp.dot(u, wu_ref[0], preferred_element_type=F32)
        h = (g * jax.nn.sigmoid(g) * up).astype(BF16)
        part = jnp.dot(h, wd_ref[0], preferred_element_type=F32)

        @pl.when(j == 0)
        def _():
            acc_ref[...] = part

        @pl.when(j > 0)
        def _():
            acc_ref[...] += part

    @pl.when(j == last)
    def _():
        o_ref[...] = jnp.where(tv_ref[i] == 1, acc_ref[...], 0.0)


def grouped_swiglu(u, w_gate, w_up, w_down, tile_expert, tile_valid, ff_tile):
    P, d = u.shape
    F = w_gate.shape[2]
    t = FFN_ROW_TILE
    grid_spec = pltpu.PrefetchScalarGridSpec(
        num_scalar_prefetch=2,
        grid=(P // t, F // ff_tile),
        in_specs=[pl.BlockSpec((t, d), lambda i, j, te, tv: (i, 0)),
                  pl.BlockSpec((1, d, ff_tile), lambda i, j, te, tv: (te[i], 0, j)),
                  pl.BlockSpec((1, d, ff_tile), lambda i, j, te, tv: (te[i], 0, j)),
                  pl.BlockSpec((1, ff_tile, d), lambda i, j, te, tv: (te[i], j, 0))],
        out_specs=pl.BlockSpec((t, d), lambda i, j, te, tv: (i, 0)),
        scratch_shapes=[pltpu.VMEM((t, d), F32)])
    return pl.pallas_call(
        _swiglu_kernel,
        grid_spec=grid_spec,
        out_shape=jax.ShapeDtypeStruct((P, d), F32),
        compiler_params=pltpu.CompilerParams(dimension_semantics=("parallel", "arbitrary"),
                                             vmem_limit_bytes=VMEM_LIMIT),
        name="grouped_swiglu",
    )(tile_expert, tile_valid, u, w_gate, w_up, w_down)


def _fox_flash_kernel(q_ref, k_ref, v_ref, cq_ref, ck_ref, o_ref, m_scr, l_scr, acc_scr):
    qi, ki = pl.program_id(1), pl.program_id(2)
    tq, tk, C = q_ref.shape[1], k_ref.shape[1], q_ref.shape[2]
    H = m_scr.shape[0]
    lane_head = lax.broadcasted_iota(jnp.int32, (tq, C), 1) // (C // H)

    @pl.when(ki == 0)
    def _():
        m_scr[...] = jnp.full(m_scr.shape, -jnp.inf, F32)
        l_scr[...] = jnp.zeros(l_scr.shape, F32)
        acc_scr[...] = jnp.zeros(acc_scr.shape, F32)

    def tile(on_diagonal):
        q, k, v = q_ref[0], k_ref[0], v_ref[0]
        acc = acc_scr[...]
        for h in range(H):
            mine = lane_head == h
            s = lax.dot_general(jnp.where(mine, q, jnp.zeros_like(q)), k, _NT, preferred_element_type=F32)
            s = s + cq_ref[0, h] - ck_ref[0, h]
            if on_diagonal:
                s = jnp.where(lax.broadcasted_iota(jnp.int32, (tq, tk), 1)
                              <= lax.broadcasted_iota(jnp.int32, (tq, tk), 0), s, -jnp.inf)
            m_prev = m_scr[h]
            m_new = jnp.maximum(m_prev, jnp.max(s, axis=-1, keepdims=True))
            alpha = jnp.exp(m_prev - m_new)
            p = jnp.exp(s - m_new)
            l_scr[h] = alpha * l_scr[h] + jnp.sum(p, axis=-1, keepdims=True)
            m_scr[h] = m_new
            pv = jnp.dot(p.astype(BF16), v, preferred_element_type=F32)
            acc = jnp.where(mine, alpha * acc + pv, acc)
        acc_scr[...] = acc

    pl.when(ki < qi)(functools.partial(tile, False))
    pl.when(ki == qi)(functools.partial(tile, True))

    @pl.when(ki == pl.num_programs(2) - 1)
    def _():
        l = jnp.zeros((tq, C), F32)
        for h in range(H):
            l = jnp.where(lane_head == h, l_scr[h], l)
        o_ref[0] = acc_scr[...] / l


def fox_flash_attention(q, k, v, cq, ck):
    N, L, C = q.shape
    H = cq.shape[1]
    t = min(L, ATT_TILE)
    nt = L // t
    kv_spec = pl.BlockSpec((1, t, C), lambda n, qi, ki: (n, jnp.minimum(ki, qi), 0))
    return pl.pallas_call(
        _fox_flash_kernel,
        grid=(N, nt, nt),
        in_specs=[pl.BlockSpec((1, t, C), lambda n, qi, ki: (n, qi, 0)), kv_spec, kv_spec,
                  pl.BlockSpec((1, H, t, 1), lambda n, qi, ki: (n, 0, qi, 0)),
                  pl.BlockSpec((1, H, 1, t), lambda n, qi, ki: (n, 0, 0, jnp.minimum(ki, qi)))],
        out_specs=pl.BlockSpec((1, t, C), lambda n, qi, ki: (n, qi, 0)),
        out_shape=jax.ShapeDtypeStruct((N, L, C), F32),
        scratch_shapes=[pltpu.VMEM((H, t, 1), F32), pltpu.VMEM((H, t, 1), F32), pltpu.VMEM((t, C), F32)],
        compiler_params=pltpu.CompilerParams(
            dimension_semantics=("parallel", "parallel", "arbitrary"), vmem_limit_bytes=VMEM_LIMIT),
        name="fox_flash_attention",
    )(q, k, v, cq, ck)


def _fox_prep_kernel(q_ref, k_ref, v_ref, qw_ref, kw_ref, qb_ref, kn_ref, kb_ref, vb_ref):
    C = q_ref.shape[1]
    same_head = (lax.broadcasted_iota(jnp.int32, (C, C), 0) // HEAD_DIM
                 == lax.broadcasted_iota(jnp.int32, (C, C), 1) // HEAD_DIM).astype(BF16)

    def head_norm(x, w):
        ms = _exact_rhs_mm(x * x, same_head) * (1.0 / HEAD_DIM)
        return x * lax.rsqrt(ms + NORM_EPS) * w

    qb_ref[...] = (head_norm(q_ref[...], qw_ref[...]) * (HEAD_DIM ** -0.5)).astype(BF16)
    kn = head_norm(k_ref[...], kw_ref[...])
    kn_ref[...] = kn
    kb_ref[...] = kn.astype(BF16)
    vb_ref[...] = v_ref[...].astype(BF16)


def fox_prep(q, k, v, qn_w, kn_w):
    T, C = q.shape
    t = min(T, ATT_TILE)
    rows = pl.BlockSpec((t, C), lambda i: (i, 0))
    w_spec = pl.BlockSpec((1, C), lambda i: (0, 0))
    tile_w = lambda w: jnp.tile(w.astype(F32), C // HEAD_DIM)[None, :]
    return pl.pallas_call(
        _fox_prep_kernel,
        grid=(T // t,),
        in_specs=[rows, rows, rows, w_spec, w_spec],
        out_specs=[rows, rows, rows, rows],
        out_shape=[jax.ShapeDtypeStruct((T, C), BF16), jax.ShapeDtypeStruct((T, C), F32),
                   jax.ShapeDtypeStruct((T, C), BF16), jax.ShapeDtypeStruct((T, C), BF16)],
        compiler_params=pltpu.CompilerParams(dimension_semantics=("parallel",)),
        name="fox_prep",
    )(q, k, v, tile_w(qn_w), tile_w(kn_w))


def _split(t, sizes):
    return jnp.split(t, np.cumsum(sizes)[:-1].tolist(), axis=-1)


def head_rms(t, w):
    tf = t.astype(jnp.float32)
    return (tf * lax.rsqrt(jnp.mean(tf * tf, axis=-1, keepdims=True) + NORM_EPS) * w).astype(t.dtype)


ADA_COL_TILE = 768


def _ada_kernel(c_ref, w_ref, b_ref, o_ref):
    c = c_ref[...]
    h = (c * jax.nn.sigmoid(c)).astype(BF16)
    o_ref[...] = jnp.dot(h, w_ref[...].astype(BF16), preferred_element_type=F32) + b_ref[...]


def ada_modulation(c, w_ada, b_ada):
    N, d = c.shape
    M = -(-N // SUBLANE) * SUBLANE
    n_out = w_ada.shape[1]
    m = pl.pallas_call(
        _ada_kernel,
        grid=(n_out // ADA_COL_TILE,),
        in_specs=[pl.BlockSpec((M, d), lambda j: (0, 0)),
                  pl.BlockSpec((d, ADA_COL_TILE), lambda j: (0, j)),
                  pl.BlockSpec((1, ADA_COL_TILE), lambda j: (0, j))],
        out_specs=pl.BlockSpec((M, ADA_COL_TILE), lambda j: (0, j)),
        out_shape=jax.ShapeDtypeStruct((M, n_out), F32),
        compiler_params=pltpu.CompilerParams(dimension_semantics=("parallel",)),
        name="ada_modulation",
    )(jnp.pad(c.astype(F32), ((0, M - N), (0, 0))), w_ada, b_ada.astype(F32)[None, :])
    return jnp.split(m[:N, None, :], 6, axis=-1)


def dense_swiglu(u, w_gate, w_up, w_down, ff_tile):
    nt = u.shape[0] // FFN_ROW_TILE
    return grouped_swiglu(u, w_gate[None], w_up[None], w_down[None],
                          jnp.zeros((nt,), jnp.int32), jnp.ones((nt,), jnp.int32), ff_tile)


def moe_swiglu(u, logits, w_gate, w_up, w_down, ff_tile):
    T, d = u.shape
    E = logits.shape[1]
    t = FFN_ROW_TILE
    top_val, top_idx = lax.top_k(logits, TOP_K)
    weights = jax.nn.softmax(top_val, axis=-1)
    eid = top_idx.reshape(-1).astype(jnp.int32)
    n_pairs = T * TOP_K
    order = jnp.argsort(eid, stable=True).astype(jnp.int32)
    counts = jnp.sum(jax.nn.one_hot(eid, E, dtype=jnp.int32), axis=0)
    padded = ((counts + t - 1) // t) * t
    ends_p = jnp.cumsum(padded)
    start_p = ends_p - padded
    start = jnp.cumsum(counts) - counts
    P = n_pairs + E * t
    tile_start = jnp.arange(P // t, dtype=jnp.int32) * t
    tile_expert = jnp.minimum(jnp.searchsorted(ends_p, tile_start, side='right'), E - 1).astype(jnp.int32)
    tile_valid = (tile_start < ends_p[-1]).astype(jnp.int32)
    per_row = lambda table: jnp.repeat(table[tile_expert], t)
    row_off = jnp.arange(P, dtype=jnp.int32) - per_row(start_p)
    row_ok = row_off < per_row(counts)
    src_pair = order[jnp.clip(per_row(start) + row_off, 0, n_pairs - 1)]
    src_tok = jnp.where(row_ok, src_pair // TOP_K, 0)
    rank = jnp.argsort(order).astype(jnp.int32)
    shift_e = jnp.sum(jax.nn.one_hot(eid, E, dtype=jnp.int32) * (start_p - start)[None, :], axis=1)
    pos_of_pair = (rank + shift_e).reshape(T, TOP_K)
    ys = grouped_swiglu(u[src_tok], w_gate, w_up, w_down, tile_expert, tile_valid, ff_tile)
    out = weights[:, 0:1] * ys[pos_of_pair[:, 0]]
    for s in range(1, TOP_K):
        out = out + weights[:, s:s + 1] * ys[pos_of_pair[:, s]]
    return out


RWKV_CHUNK = 64
RWKV_BLOCK = 256
RWKV_INV_BLOCK = 16

_NN = (((1,), (0,)), ((), ()))
_NT = (((1,), (1,)), ((), ()))


def _dot(a, b, dims):
    return lax.dot_general(a, b, dims, preferred_element_type=jnp.float32)


_BNN = (((2,), (1,)), ((0,), (0,)))
_BNT = (((2,), (2,)), ((0,), (0,)))


def _sp(x):
    hi = x.astype(BF16)
    lo = (x - hi.astype(F32)).astype(BF16)
    return hi, lo


def _mmp(a, b, dims=_BNN):
    (ah, al), (bh, bl) = a, b
    return _dot(ah, bh, dims) + _dot(ah, bl, dims) + _dot(al, bh, dims)


def _mm_exact_lhs(a_bf16, x):
    h1 = x.astype(BF16)
    r1 = x - h1.astype(F32)
    h2 = r1.astype(BF16)
    h3 = (r1 - h2.astype(F32)).astype(BF16)
    return _dot(a_bf16, h1, _BNN) + _dot(a_bf16, h2, _BNN) + _dot(a_bf16, h3, _BNN)


def _unit_lower_inverse(a, blk_mask, eye):
    ad = jnp.where(blk_mask, a, 0.0)
    e = a - ad
    ad_p = _sp(ad)
    p2 = _mmp(ad_p, ad_p)
    p2_p = _sp(p2)
    p4 = _mmp(p2_p, p2_p)
    p4_p = _sp(p4)
    p8 = _mmp(p4_p, p4_p)
    x = _mmp(_sp(eye - ad), _sp(eye + p2))
    x = _mmp(_sp(x), _sp(eye + p4))
    dinv_p = _sp(_mmp(_sp(x), _sp(eye + p8)))
    n = _mmp(dinv_p, _sp(e))
    n_p = _sp(n)
    n2 = _mmp(n_p, n_p)
    return _mmp(_sp(_mmp(_sp(eye - n), _sp(eye + n2))), dinv_p)


def _rwkv_chunk_prep(r, lw, k, v, vt, kk, a):
    B, T, _ = r.shape
    row = lax.broadcasted_iota(jnp.int32, (T, T), 0)
    col = lax.broadcasted_iota(jnp.int32, (T, T), 1)
    strict, incl = row > col, row >= col
    blk = (row // RWKV_INV_BLOCK) == (col // RWKV_INV_BLOCK)
    eye = (row == col).astype(F32)
    c = _mm_exact_lhs(jnp.broadcast_to(incl.astype(BF16), (B, T, T)), lw)
    c_last = c[:, T - 1:T, :]
    e_nc = jnp.exp(-c)
    e_end = jnp.exp(c_last - c)
    b = kk * a
    kt = _sp(kk * jnp.exp(c - lw))
    rt = _sp(r * jnp.exp(c))
    kh = _sp(k * e_nc)
    bh = _sp(b * e_nc)
    v_p, vt_p = _sp(v), _sp(vt)
    a_kb = jnp.where(strict, _mmp(kt, bh, _BNT), 0.0)
    a_kk = _sp(jnp.where(strict, _mmp(kt, kh, _BNT), 0.0))
    a_rk = _sp(jnp.where(incl, _mmp(rt, kh, _BNT), 0.0))
    a_rb = _sp(jnp.where(incl, _mmp(rt, bh, _BNT), 0.0))
    m = _sp(_unit_lower_inverse(a_kb, blk, eye))
    mk = _sp(_mmp(m, kt))
    g1 = _mmp(m, _sp(_mmp(a_kk, v_p)))
    g1t = _mmp(_sp(_mmp(vt_p, a_kk, _BNT)), m, _BNT)
    y0 = _mmp(a_rk, v_p)
    s_add = _mmp(vt_p, _sp(k * e_end))
    return dict(mk=mk, g1=g1, g1t=g1t, rt=rt, a_rb=a_rb, y0=y0, s_add=s_add, bbar=_sp(b * e_end),
                decay=jnp.exp(c_last))


def _rwkv_chunk_step(S, q, lo, hi):
    pick = lambda x: tuple(t[lo:hi] for t in x) if isinstance(x, tuple) else x[lo:hi]
    S_p, mk = _sp(S), pick(q['mk'])
    u = _mmp(mk, S_p, _BNT) + pick(q['g1'])
    y = _mmp(pick(q['rt']), S_p, _BNT) + pick(q['y0']) - _mmp(pick(q['a_rb']), _sp(u))
    ut = _mmp(S_p, mk, _BNT) + pick(q['g1t'])
    s_new = S * pick(q['decay']) + pick(q['s_add']) - _mmp(_sp(ut), pick(q['bbar']))
    return y, s_new


def _rwkv_kernel(r_ref, lw_ref, k_ref, v_ref, vt_ref, kk_ref, a_ref, s0_ref, y_ref, sT_ref, s_scr):
    H = r_ref.shape[1]
    T = RWKV_CHUNK

    @pl.when(pl.program_id(1) == 0)
    def _():
        s_scr[...] = s0_ref[0]

    J = RWKV_BLOCK // T
    rows = lambda ref: jnp.concatenate([ref[0, :, j * T:(j + 1) * T, :] for j in range(J)], axis=0)
    vt = jnp.concatenate([vt_ref[0, :, :, j * T:(j + 1) * T] for j in range(J)], axis=0)
    q = _rwkv_chunk_prep(rows(r_ref), rows(lw_ref), rows(k_ref), rows(v_ref), vt, rows(kk_ref), rows(a_ref))
    S = s_scr[...]
    for j in range(J):
        y, S = _rwkv_chunk_step(S, q, j * H, (j + 1) * H)
        y_ref[0, :, j * T:(j + 1) * T, :] = y
    s_scr[...] = S
    sT_ref[0] = S


def rwkv7_recurrence_pallas(r, lw, k, v, kk, a, s0):
    N, L, H, D = r.shape
    hm = lambda t: jnp.transpose(t, (0, 2, 1, 3))
    vt = jnp.transpose(v, (0, 2, 3, 1))
    row_spec = pl.BlockSpec((1, H, RWKV_BLOCK, D), lambda n, c: (n, 0, c, 0))
    st_spec = pl.BlockSpec((1, H, D, D), lambda n, c: (n, 0, 0, 0))
    y, s_last = pl.pallas_call(
        _rwkv_kernel,
        grid=(N, L // RWKV_BLOCK),
        in_specs=[row_spec, row_spec, row_spec, row_spec,
                  pl.BlockSpec((1, H, D, RWKV_BLOCK), lambda n, c: (n, 0, 0, c)),
                  row_spec, row_spec, st_spec],
        out_specs=[row_spec, st_spec],
        out_shape=[jax.ShapeDtypeStruct((N, H, L, D), jnp.float32),
                   jax.ShapeDtypeStruct((N, H, D, D), jnp.float32)],
        scratch_shapes=[pltpu.VMEM((H, D, D), jnp.float32)],
        compiler_params=pltpu.CompilerParams(dimension_semantics=("parallel", "arbitrary")),
        name="rwkv7_chunked",
    )(hm(r), hm(lw), hm(k), hm(v), vt, hm(kk), hm(a), s0)
    return jnp.transpose(y, (0, 2, 1, 3)), s_last


RWKV_SHORT_SEQS = 8


def _rwkv_short_kernel(r_ref, w_ref, k_ref, vt_ref, kk_ref, a_ref, s0_ref, yt_ref, sT_ref):
    nb, H, L, D = r_ref.shape
    B = nb * H
    lane_t = lax.broadcasted_iota(jnp.int32, (B, D, L), 2)
    S = s0_ref[...].reshape(B, D, D)
    yt = jnp.zeros((B, D, L), F32)
    for t in range(L):
        row = lambda ref: ref[:, :, t:t + 1, :].reshape(B, 1, D)
        kk_t = row(kk_ref)
        s_kk = jnp.sum(S * kk_t, axis=-1, keepdims=True)
        S = S * row(w_ref) - s_kk * (kk_t * row(a_ref)) + vt_ref[:, :, :, t:t + 1].reshape(B, D, 1) * row(k_ref)
        yt = jnp.where(lane_t == t, jnp.sum(S * row(r_ref), axis=-1, keepdims=True), yt)
    yt_ref[...] = yt.reshape(nb, H, D, L)
    sT_ref[...] = S.reshape(nb, H, D, D)


def rwkv7_recurrence_short(r, decay, k, v, kk, a, s0):
    N, L, H, D = r.shape
    nb = RWKV_SHORT_SEQS
    hm = lambda t: jnp.transpose(t, (0, 2, 1, 3))
    row_spec = pl.BlockSpec((nb, H, L, D), lambda n: (n, 0, 0, 0))
    col_spec = pl.BlockSpec((nb, H, D, L), lambda n: (n, 0, 0, 0))
    st_spec = pl.BlockSpec((nb, H, D, D), lambda n: (n, 0, 0, 0))
    yt, s_last = pl.pallas_call(
        _rwkv_short_kernel,
        grid=(N // nb,),
        in_specs=[row_spec, row_spec, row_spec, col_spec, row_spec, row_spec, st_spec],
        out_specs=[col_spec, st_spec],
        out_shape=[jax.ShapeDtypeStruct((N, H, D, L), F32), jax.ShapeDtypeStruct((N, H, D, D), F32)],
        compiler_params=pltpu.CompilerParams(dimension_semantics=("parallel",)),
        name="rwkv7_short",
    )(hm(r), hm(decay), hm(k), jnp.transpose(v, (0, 2, 3, 1)), hm(kk), hm(a), s0)
    return jnp.transpose(yt, (0, 3, 1, 2)), s_last


RWKV_PREP_TILE = 512


def _rwkv_prep_kernel(cols_ref, prev_ref, shift0_ref, mu_ref, w0_ref, w2_ref, a0_ref, a2_ref, g2_ref, kkw_ref,
                      ka_ref, r_ref, lw_ref, k_ref, v_ref, kk_ref, a_ref, g_ref):
    i = pl.program_id(1)
    cols = cols_ref[0]
    t = cols.shape[0]
    before = jnp.where(i == 0, shift0_ref[0], prev_ref[SUBLANE - 1:SUBLANE, :])
    prev = jnp.where(lax.broadcasted_iota(jnp.int32, cols.shape, 0) == 0, before, pltpu.roll(cols, 1, 0))
    xs = cols + (prev - cols) * mu_ref[...]
    W = A_WIDTH
    r, k, v = xs[:, 0:W], xs[:, W:2 * W], xs[:, 2 * W:3 * W]
    lora_in = xs[:, 3 * W:3 * W + A_DECAY_LORA + A_ICL_LORA]
    xg = xs[:, 3 * W + A_DECAY_LORA + A_ICL_LORA:]
    pre = w0_ref[...] + jnp.dot(jnp.tanh(lora_in).astype(BF16), w2_ref[...], preferred_element_type=F32)
    w = -(jnp.maximum(-pre, 0.0) + jnp.log(1.0 + jnp.exp(-jnp.abs(pre)))) - 0.5
    a = jax.nn.sigmoid(a0_ref[...] + jnp.dot(lora_in.astype(BF16), a2_ref[...], preferred_element_type=F32))
    same_head = (lax.broadcasted_iota(jnp.int32, (W, W), 0) // HEAD_DIM
                 == lax.broadcasted_iota(jnp.int32, (W, W), 1) // HEAD_DIM).astype(BF16)
    kk = k * kkw_ref[...]
    norm = jnp.sqrt(_exact_rhs_mm(kk * kk, same_head))
    r_ref[0] = r
    lw_ref[0] = -jnp.exp(w)
    k_ref[0] = k * (1.0 + (a - 1.0) * ka_ref[...])
    v_ref[0] = v
    kk_ref[0] = kk / jnp.maximum(norm, 1e-12)
    a_ref[0] = a
    g_ref[0] = jnp.dot(jax.nn.sigmoid(xg).astype(BF16), g2_ref[...], preferred_element_type=F32)


def rwkv7_prep(cols, shift0, mu, w0, w2, a0, a2, g2, k_k, k_a):
    n, L, _ = cols.shape
    t = min(L, RWKV_PREP_TILE)
    nt = L // t
    row = lambda v: v.astype(F32)[None, :]
    w2p = jnp.pad(w2, ((0, A_ICL_LORA), (0, 0))).astype(BF16)
    a2p = jnp.pad(a2, ((A_DECAY_LORA, 0), (0, 0))).astype(BF16)
    full = lambda a: pl.BlockSpec(a.shape, lambda b, i: (0,) * a.ndim)
    consts = (row(mu), row(w0), w2p, row(a0), a2p, g2.astype(BF16), row(k_k), row(k_a))
    out = pl.BlockSpec((1, t, A_WIDTH), lambda b, i: (b, i, 0))
    prev_spec = pl.BlockSpec((SUBLANE, A_IN), lambda b, i: (jnp.maximum((b * L + i * t) // SUBLANE - 1, 0), 0))
    return pl.pallas_call(
        _rwkv_prep_kernel,
        grid=(n, nt),
        in_specs=[pl.BlockSpec((1, t, A_IN), lambda b, i: (b, i, 0)), prev_spec,
                  pl.BlockSpec((1, 1, A_IN), lambda b, i: (b, 0, 0))] + [full(c) for c in consts],
        out_specs=[out] * 7,
        out_shape=[jax.ShapeDtypeStruct((n, L, A_WIDTH), F32)] * 7,
        compiler_params=pltpu.CompilerParams(dimension_semantics=("parallel", "parallel")),
        name="rwkv7_prep",
    )(cols, cols.reshape(n * L, A_IN), shift0.astype(F32)[:, None, :], *consts)


def _rwkv_post_kernel(y_ref, r_ref, k_ref, v_ref, g_ref, rk_ref, lnw_ref, lnb_ref, o_ref):
    W = y_ref.shape[1]
    same_head = (lax.broadcasted_iota(jnp.int32, (W, W), 0) // HEAD_DIM
                 == lax.broadcasted_iota(jnp.int32, (W, W), 1) // HEAD_DIM).astype(BF16)
    head_sum = lambda x: _exact_rhs_mm(x, same_head)
    y = y_ref[...]
    d = y - head_sum(y) * (1.0 / HEAD_DIM)
    var = head_sum(d * d) * (1.0 / HEAD_DIM)
    gn = d * lax.rsqrt(var + GN_EPS) * lnw_ref[...] + lnb_ref[...]
    bonus = head_sum(r_ref[...] * k_ref[...] * rk_ref[...]) * v_ref[...]
    o_ref[...] = (gn + bonus) * g_ref[...]


def rwkv7_post(y, r, k, v, g, r_k, ln_w, ln_b):
    T, W = y.shape
    t = min(T, RWKV_PREP_TILE)
    rows = pl.BlockSpec((t, W), lambda i: (i, 0))
    vec = pl.BlockSpec((1, W), lambda i: (0, 0))
    row = lambda a: a.astype(F32).reshape(1, W)
    return pl.pallas_call(
        _rwkv_post_kernel,
        grid=(T // t,),
        in_specs=[rows] * 5 + [vec] * 3,
        out_specs=rows,
        out_shape=jax.ShapeDtypeStruct((T, W), F32),
        compiler_params=pltpu.CompilerParams(dimension_semantics=("parallel",)),
        name="rwkv7_post",
    )(y, r, k, v, g, row(r_k), row(ln_w), row(ln_b))


def rwkv7_mixer(cols, shift0, s0, mu, w0, w2, a0, a2, g2, k_k, k_a, r_k, ln_w, ln_b):
    n, L, _ = cols.shape
    f32 = jnp.float32
    heads = lambda t: t.astype(f32).reshape(n, L, A_HEADS, HEAD_DIM)
    if L % RWKV_BLOCK == 0:
        r, lw, k, v, kk, a, g = rwkv7_prep(cols, shift0, mu, w0, w2, a0, a2, g2, k_k, k_a)
        r_h, k_h, v_h = heads(r), heads(k), heads(v)
        y, s_last = rwkv7_recurrence_pallas(r_h, heads(lw), k_h, v_h, heads(kk), heads(a), s0.astype(f32))
    else:
        assert L <= SUBLANE and n % RWKV_SHORT_SEQS == 0, (n, L)
        prev = jnp.concatenate([shift0[:, None, :].astype(cols.dtype), cols[:, :-1]], axis=1)
        xs = cols + (prev - cols) * mu
        r, k, v, xw, xa, xg = _split(xs, [A_WIDTH, A_WIDTH, A_WIDTH, A_DECAY_LORA, A_ICL_LORA, A_GATE_LORA])
        w = -jax.nn.softplus(-(w0 + jnp.tanh(xw) @ w2)) - 0.5
        decay = jnp.exp(-jnp.exp(w.astype(f32)))
        a = jax.nn.sigmoid(a0 + xa @ a2)
        g = jax.nn.sigmoid(xg) @ g2
        kk = heads(k * k_k)
        kk = kk / jnp.maximum(jnp.sqrt(jnp.sum(kk * kk, axis=-1, keepdims=True)), 1e-12)
        k = k * (1.0 + (a - 1.0) * k_a)
        r_h, k_h, v_h = heads(r), heads(k), heads(v)
        y, s_last = rwkv7_recurrence_short(r_h, heads(decay), k_h, v_h, kk, heads(a), s0.astype(f32))
    fl = lambda t: t.reshape(n * L, A_WIDTH)
    y = rwkv7_post(fl(y), fl(r_h), fl(k_h), fl(v_h), fl(g), r_k, ln_w, ln_b).reshape(n, L, A_WIDTH)
    return y.astype(cols.dtype), s_last, cols[:, -1]


def ssd_chunked(x, dt, A, B, C, h0):
    N, L, H, P = x.shape
    q = min(SSD_CHUNK, L)
    pad = (-L) % q
    if pad:
        padl = lambda t: jnp.pad(t, [(0, 0), (0, pad)] + [(0, 0)] * (t.ndim - 2))
        x, dt, B, C = padl(x), padl(dt), padl(B), padl(C)
    nc = (L + pad) // q
    ch = lambda t: t.reshape((N, nc, q) + t.shape[2:])
    xc, dtc, Bc, Cc = ch(x), ch(dt), ch(B), ch(C)
    cs = jnp.cumsum(dtc * A, axis=2)
    causal = jnp.tril(jnp.ones((q, q), bool))[None, None, :, :, None]
    seg = cs[:, :, :, None, :] - cs[:, :, None, :, :]
    decay = jnp.exp(jnp.where(causal, seg, -jnp.inf))
    scores = jnp.einsum('nclhd,ncshd->nclsh', Cc, Bc) * decay * dtc[:, :, None, :, :]
    y_intra = jnp.einsum('nclsh,ncshp->nclhp', scores, xc)
    to_end = jnp.exp(cs[:, :, -1:, :] - cs) * dtc
    chunk_states = jnp.einsum('ncsh,ncshd,ncshp->nchpd', to_end, Bc, xc)
    chunk_decay = jnp.exp(cs[:, :, -1, :])
    def carry(h, inp):
        dec, st = inp
        return h * dec[:, :, None, None] + st, h
    h_last, h_in = lax.scan(carry, h0, (jnp.moveaxis(chunk_decay, 1, 0), jnp.moveaxis(chunk_states, 1, 0)))
    h_in = jnp.moveaxis(h_in, 0, 1)
    y_inter = jnp.einsum('nclhd,nchpd->nclhp', Cc, h_in) * jnp.exp(cs)[..., None]
    y = (y_intra + y_inter).reshape(N, nc * q, H, P)[:, :L]
    return y, h_last


def mamba2_mixer(z, xbc, dt_raw, conv0, ssm0, conv_w, conv_b, dt_bias, a_log, d_skip, norm_w):
    n, L, _ = xbc.shape
    f32 = jnp.float32
    xpad = jnp.concatenate([conv0.astype(xbc.dtype), xbc], axis=1)
    conv = conv_b + sum(xpad[:, j:j + L] * conv_w[j] for j in range(CONV_W))
    xs, Bm, Cm = _split(jax.nn.silu(conv), [B_WIDTH, B_GROUPS * D_STATE, B_GROUPS * D_STATE])
    xh = xs.astype(f32).reshape(n, L, B_HEADS, HEAD_DIM)
    rep = B_HEADS // B_GROUPS
    Bh = jnp.repeat(Bm.astype(f32).reshape(n, L, B_GROUPS, D_STATE), rep, axis=2)
    Ch = jnp.repeat(Cm.astype(f32).reshape(n, L, B_GROUPS, D_STATE), rep, axis=2)
    dt = jax.nn.softplus(dt_raw.astype(f32) + dt_bias)
    A = -jnp.exp(a_log.astype(f32))
    y, h_last = ssd_chunked(xh, dt, A, Bh, Ch, ssm0.astype(f32))
    y = (y + d_skip[:, None] * xh).reshape(n, L, B_WIDTH) * jax.nn.silu(z.astype(f32))
    yg = y.reshape(n, L, B_GROUPS, B_WIDTH // B_GROUPS)
    yg = yg * lax.rsqrt(jnp.mean(yg * yg, axis=-1, keepdims=True) + NORM_EPS)
    y = yg.reshape(n, L, B_WIDTH) * norm_w
    return y.astype(z.dtype), h_last, xpad[:, -(CONV_W - 1):]


SSD_PAIRS = B_HEADS // 2
CONV_TAIL = 8


def _exact_rhs_mm(x, sel_bf16):
    h1 = x.astype(BF16)
    r1 = x - h1.astype(F32)
    h2 = r1.astype(BF16)
    h3 = (r1 - h2.astype(F32)).astype(BF16)
    d = lambda a: jnp.dot(a, sel_bf16, preferred_element_type=F32)
    return d(h1) + d(h2) + d(h3)


def _ssd_kernel(xbc_ref, z_ref, sm_ref, conv0_ref, h0_ref, cw_ref, cb_ref, dtb_ref, a_ref, dskip_ref, nw_ref,
                y_ref, hT_ref, xbuf, h_scr):
    Q = SSD_CHUNK
    c = pl.program_id(1)

    @pl.when(c == 0)
    def _():
        xbuf[0:CONV_TAIL, :] = conv0_ref[0]
        h_scr[...] = h0_ref[0]

    @pl.when(c > 0)
    def _():
        xbuf[0:CONV_TAIL, :] = xbuf[Q:Q + CONV_TAIL, :]

    xbuf[CONV_TAIL:CONV_TAIL + Q, :] = xbc_ref[0]
    conv = cb_ref[...]
    for j in range(CONV_W):
        conv = conv + cw_ref[j:j + 1, :] * xbuf[pl.ds(CONV_TAIL - (CONV_W - 1) + j, Q), :]
    act = conv * jax.nn.sigmoid(conv)
    xs = act[:, :B_WIDTH]
    n_bc = B_GROUPS * D_STATE
    Bm, Cm = act[:, B_WIDTH:B_WIDTH + n_bc], act[:, B_WIDTH + n_bc:]

    pre = sm_ref[0] + dtb_ref[...]
    dt = jnp.maximum(pre, 0.0) + jnp.log(1.0 + jnp.exp(-jnp.abs(pre)))
    dA = dt * a_ref[...]
    row = lax.broadcasted_iota(jnp.int32, (Q, Q), 0)
    col = lax.broadcasted_iota(jnp.int32, (Q, Q), 1)
    causal = row >= col
    tri = causal.astype(BF16)
    h1 = dA.astype(BF16)
    r1 = dA - h1.astype(F32)
    h2 = r1.astype(BF16)
    h3 = (r1 - h2.astype(F32)).astype(BF16)
    cs = (jnp.dot(tri, h1, preferred_element_type=F32) + jnp.dot(tri, h2, preferred_element_type=F32)
          + jnp.dot(tri, h3, preferred_element_type=F32))
    csT, dtT = cs.T, dt.T
    sel_b = (lax.broadcasted_iota(jnp.int32, (LANE, B_WIDTH), 0)
             == lax.broadcasted_iota(jnp.int32, (LANE, B_WIDTH), 1) // HEAD_DIM).astype(BF16)
    sel_h = (lax.broadcasted_iota(jnp.int32, (LANE, B_HEADS * Q), 0)
             == lax.broadcasted_iota(jnp.int32, (LANE, B_HEADS * Q), 1) // Q).astype(BF16)
    cs_b = _exact_rhs_mm(cs, sel_b)
    dt_b = _exact_rhs_mm(dt, sel_b)
    cs_full = _exact_rhs_mm(cs, sel_h)
    cs_last_b = cs_b[Q - 1:Q, :]
    e_b = jnp.exp(cs_b)
    xs_bf = xs.astype(BF16)
    xs_te = (xs * (jnp.exp(cs_last_b - cs_b) * dt_b)).astype(BF16)
    first_half = lax.broadcasted_iota(jnp.int32, (Q, LANE), 1) < HEAD_DIM
    ys = []
    for g in range(B_GROUPS):
        gs = slice(g * D_STATE, (g + 1) * D_STATE)
        Cg, Bg = Cm[:, gs].astype(BF16), Bm[:, gs]
        cb = lax.dot_general(Cg, Bg.astype(BF16), _NT, preferred_element_type=F32)
        BTg = Bg.T.astype(BF16)
        for pp in range(SSD_PAIRS // B_GROUPS):
            pair = g * (SSD_PAIRS // B_GROUPS) + pp
            lanes = slice(pair * LANE, (pair + 1) * LANE)
            hT = h_scr[pair]
            y_pair = jnp.dot(Cg, hT.astype(BF16), preferred_element_type=F32) * e_b[:, lanes]
            for j in range(2):
                h = 2 * pair + j
                seg = cs_full[:, h * Q:(h + 1) * Q] - csT[h:h + 1, :]
                dec = jnp.exp(jnp.where(causal, seg, -jnp.inf))
                sc = (cb * dec * dtT[h:h + 1, :]).astype(BF16)
                xm = jnp.where(first_half if j == 0 else jnp.logical_not(first_half), xs_bf[:, lanes], 0.0)
                y_pair = y_pair + jnp.dot(sc, xm.astype(BF16), preferred_element_type=F32)
            h_scr[pair] = hT * jnp.exp(cs_last_b[:, lanes]) + jnp.dot(BTg, xs_te[:, lanes],
                                                                      preferred_element_type=F32)
            ys.append(y_pair)
    y = jnp.concatenate(ys, axis=-1)
    z = z_ref[0]
    y = (y + dskip_ref[...] * xs) * (z * jax.nn.sigmoid(z))
    gw = B_WIDTH // B_GROUPS
    outs = []
    for g in range(B_GROUPS):
        yg = y[:, g * gw:(g + 1) * gw]
        outs.append(yg * lax.rsqrt(jnp.mean(yg * yg, axis=-1, keepdims=True) + NORM_EPS))
    y_ref[0] = jnp.concatenate(outs, axis=-1) * nw_ref[...]
    hT_ref[0] = h_scr[...]


def mamba2_mixer_pallas(z, xbc, small, conv0, ssm0, conv_w, conv_b, dt_bias, a_log, d_skip, norm_w):
    n, L, _ = xbc.shape
    Q = SSD_CHUNK
    pad_l = lambda v: jnp.pad(v.astype(F32), (0, LANE - v.shape[0]))[None, :]
    tail0 = jnp.pad(conv0.astype(F32), ((0, 0), (CONV_TAIL - (CONV_W - 1), 0), (0, 0)))
    h0 = ssm0.astype(F32).reshape(n, SSD_PAIRS, 2, HEAD_DIM, D_STATE)
    h0 = jnp.transpose(h0, (0, 1, 4, 2, 3)).reshape(n, SSD_PAIRS, D_STATE, 2 * HEAD_DIM)
    full = lambda a: pl.BlockSpec(a.shape, lambda i, c: (0,) * a.ndim)
    seq = lambda w: pl.BlockSpec((1, Q, w), lambda i, c: (i, c, 0))
    args = (xbc, z, small, tail0, h0, conv_w.astype(F32), conv_b.astype(F32)[None, :], pad_l(dt_bias),
            pad_l(-jnp.exp(a_log.astype(F32))), jnp.repeat(d_skip.astype(F32), HEAD_DIM)[None, :],
            norm_w.astype(F32)[None, :])
    st_spec = pl.BlockSpec((1, SSD_PAIRS, D_STATE, 2 * HEAD_DIM), lambda i, c: (i, 0, 0, 0))
    y, hT = pl.pallas_call(
        _ssd_kernel,
        grid=(n, L // Q),
        in_specs=[seq(B_CONV_DIM), seq(B_WIDTH), seq(LANE),
                  pl.BlockSpec((1, CONV_TAIL, B_CONV_DIM), lambda i, c: (i, 0, 0)), st_spec]
                 + [full(a) for a in args[5:]],
        out_specs=[seq(B_WIDTH), st_spec],
        out_shape=[jax.ShapeDtypeStruct((n, L, B_WIDTH), F32),
                   jax.ShapeDtypeStruct((n, SSD_PAIRS, D_STATE, 2 * HEAD_DIM), F32)],
        scratch_shapes=[pltpu.VMEM((Q + CONV_TAIL, B_CONV_DIM), F32),
                        pltpu.VMEM((SSD_PAIRS, D_STATE, 2 * HEAD_DIM), F32)],
        compiler_params=pltpu.CompilerParams(dimension_semantics=("parallel", "arbitrary"),
                                             vmem_limit_bytes=VMEM_LIMIT),
        name="mamba2_ssd",
    )(*args)
    h_last = jnp.transpose(hT.reshape(n, SSD_PAIRS, D_STATE, 2, HEAD_DIM), (0, 1, 3, 4, 2))
    h_last = h_last.reshape(n, B_HEADS, HEAD_DIM, D_STATE)
    tail = jnp.concatenate([conv0.astype(xbc.dtype), xbc[:, -(CONV_W - 1):]], axis=1)[:, -(CONV_W - 1):]
    return y, h_last, tail


SUBLANE = 8


def _fox_paged_kernel(pt_ref, qbd_ref, knew_ref, vnew_ref, cq_ref, ckn_ref, *refs, n_pages, n_new):
    k_refs, v_refs, lf_refs = refs[:n_pages], refs[n_pages:2 * n_pages], refs[2 * n_pages:3 * n_pages]
    o_ref = refs[3 * n_pages]
    PS = k_refs[0].shape[2]
    qbd = qbd_ref[0]
    cq = cq_ref[0]
    lf = jnp.concatenate([r[0] for r in lf_refs], axis=0)
    after = (lax.broadcasted_iota(jnp.int32, (PS, PS), 0)
             > lax.broadcasted_iota(jnp.int32, (PS, PS), 1)).astype(BF16)
    suf = _exact_rhs_mm(lf, after)
    page_sum = jnp.sum(lf, axis=-1, keepdims=True)
    tails = [jnp.zeros((SUBLANE, 1), F32)]
    for j in range(n_pages - 1, 0, -1):
        tails.append(tails[-1] + page_sum[j * SUBLANE:(j + 1) * SUBLANE])
    tails = tails[::-1]
    scores = []
    for j in range(n_pages):
        s = jnp.dot(qbd, k_refs[j][0].astype(BF16), preferred_element_type=F32)
        brow = suf[j * SUBLANE:(j + 1) * SUBLANE] + tails[j]
        bias = jnp.concatenate([jnp.broadcast_to(brow[h:h + 1], (SUBLANE, PS)) for h in range(C_HEADS)], axis=0)
        scores.append(s + cq + bias)
    sn = lax.dot_general(qbd, knew_ref[0], _NT, preferred_element_type=F32)
    t_idx = lax.broadcasted_iota(jnp.int32, sn.shape, 0) % SUBLANE
    s_idx = lax.broadcasted_iota(jnp.int32, sn.shape, 1)
    sn = jnp.where((s_idx <= t_idx) & (s_idx < n_new), sn + cq - ckn_ref[0], -jnp.inf)
    m = jnp.max(sn, axis=-1, keepdims=True)
    for s in scores:
        m = jnp.maximum(m, jnp.max(s, axis=-1, keepdims=True))
    pn = jnp.exp(sn - m)
    l = jnp.sum(pn, axis=-1, keepdims=True)
    acc = jnp.dot(pn.astype(BF16), vnew_ref[0], preferred_element_type=F32)
    for j in range(n_pages):
        pj = jnp.exp(scores[j] - m)
        l = l + jnp.sum(pj, axis=-1, keepdims=True)
        acc = acc + lax.dot_general(pj.astype(BF16), v_refs[j][0].astype(BF16), _NT, preferred_element_type=F32)
    out = acc / l
    lane_head = lax.broadcasted_iota(jnp.int32, (SUBLANE, C_WIDTH), 1) // HEAD_DIM
    o = jnp.zeros((SUBLANE, C_WIDTH), F32)
    for h in range(C_HEADS):
        o = o + jnp.where(lane_head == h, out[h * SUBLANE:(h + 1) * SUBLANE], 0.0)
    o_ref[0] = o


def fox_paged_attention(qh, kh, vh, logf, pools_k, pools_v, pools_logf, page_table, layer):
    n, L, H, hd = qh.shape
    depth, n_phys, PS = pools_k.shape[:3]
    n_pages = page_table.shape[1]
    R = H * SUBLANE
    pad_t = lambda t: jnp.pad(t, ((0, 0), (0, 0), (0, SUBLANE - L)) + ((0, 0),) * (t.ndim - 3))
    q_hm = pad_t(jnp.transpose(qh * (hd ** -0.5), (0, 2, 1, 3)))
    qbd = (jnp.eye(H, dtype=F32)[None, :, None, :, None] * q_hm[:, :, :, None, :]).reshape(n, R, H * hd)
    new_rows = lambda t: jnp.pad(t.reshape(n, L, H * hd), ((0, 0), (0, SUBLANE - L), (0, 0))).astype(BF16)
    cum = pad_t(jnp.transpose(jnp.cumsum(logf, axis=1), (0, 2, 1)))
    cq = cum.reshape(n, R, 1)
    ckn = jnp.broadcast_to(cum[:, :, None, :], (n, H, SUBLANE, SUBLANE)).reshape(n, R, SUBLANE)
    lf_t = jnp.pad(jnp.transpose(pools_logf[layer].astype(F32), (0, 2, 1)), ((0, 0), (0, SUBLANE - H), (0, 0)))
    pk, pv = (jnp.transpose(t, (0, 1, 3, 4, 2)).reshape(depth * n_phys, H * hd, PS) for t in (pools_k, pools_v))
    base = layer * n_phys
    seq = lambda r, c: pl.BlockSpec((1, r, c), lambda i, pt: (i, 0, 0))
    page = lambda j, r, c, off: pl.BlockSpec((1, r, c), lambda i, pt, j=j: (pt[i, j] + off, 0, 0))
    grid_spec = pltpu.PrefetchScalarGridSpec(
        num_scalar_prefetch=1,
        grid=(n,),
        in_specs=[seq(R, H * hd), seq(SUBLANE, H * hd), seq(SUBLANE, H * hd), seq(R, 1), seq(R, SUBLANE)]
                 + [page(j, H * hd, PS, base) for j in range(n_pages)] * 2
                 + [page(j, SUBLANE, PS, 0) for j in range(n_pages)],
        out_specs=seq(SUBLANE, H * hd))
    o = pl.pallas_call(
        functools.partial(_fox_paged_kernel, n_pages=n_pages, n_new=L),
        grid_spec=grid_spec,
        out_shape=jax.ShapeDtypeStruct((n, SUBLANE, H * hd), F32),
        compiler_params=pltpu.CompilerParams(dimension_semantics=("parallel",), vmem_limit_bytes=VMEM_LIMIT),
        name="fox_paged_attention",
    )(page_table, qbd.astype(BF16), new_rows(kh), new_rows(vh), cq, ckn,
      *([pk] * n_pages), *([pv] * n_pages), *([lf_t] * n_pages))
    return o[:, :L]


def fox_mixer(q, k, v, f_raw, f_bias, qn_w, kn_w, past, layer):
    n, L, _ = q.shape
    vh = v.reshape(n, L, C_HEADS, HEAD_DIM)
    logf = jax.nn.log_sigmoid(f_raw.astype(jnp.float32) + f_bias)
    if past is None and L % ATT_TILE == 0:
        fl = lambda t: t.reshape(n * L, C_WIDTH)
        qb, kn, kb, vb = (t.reshape(n, L, C_WIDTH) for t in fox_prep(fl(q), fl(k), fl(v), qn_w, kn_w))
        cum = jnp.transpose(jnp.cumsum(logf, axis=1), (0, 2, 1))
        o = fox_flash_attention(qb, kb, vb, cum[..., None], cum[:, :, None, :])
        return o.astype(q.dtype), kn.reshape(n, L, C_HEADS, HEAD_DIM), vh, logf
    qh = head_rms(q.reshape(n, L, C_HEADS, HEAD_DIM), qn_w)
    kh = head_rms(k.reshape(n, L, C_HEADS, HEAD_DIM), kn_w)
    assert past is not None and L <= SUBLANE, "supported: long fresh sequences, or few new tokens over a paged past"
    o = fox_paged_attention(qh, kh, vh, logf, *past, layer)
    return o.reshape(n, L, C_WIDTH).astype(q.dtype), kh, vh, logf


def mix_group(x, c, l, p, s0, shift0, ssm0, conv0, past):
    n, L, d = x.shape
    mods = ada_modulation(c, p['w_ada'][l], p['b_ada'][l])
    if L >= ROW_TILE:
        xg = x
    else:
        xg = x.reshape(1, n * L, d)
        mods = [jnp.broadcast_to(m, (n, L, d)).reshape(1, n * L, d) for m in mods]
    sh1, sc1, g1, sh2, sc2, g2 = mods
    a_cols, b_z, b_xbc, c_q, c_k, c_v, small = in_projection(xg, p['norm1_w'][l], sc1, sh1, p['w_in_perm'][l])
    rs = lambda t: t.reshape(n, L, t.shape[-1])
    a_cols, b_z, b_xbc = rs(a_cols), rs(b_z), rs(b_xbc)
    b_dt, c_f = rs(small[:, :B_HEADS]), rs(small[:, B_HEADS:B_HEADS + C_HEADS])
    c_q, c_k, c_v = rs(c_q), rs(c_k), rs(c_v)
    y_a, s_new, shift_new = rwkv7_mixer(
        a_cols, shift0, s0, p['a_mu'][l], p['a_w0'][l], p['a_w2'][l], p['a_a0'][l], p['a_a2'][l],
        p['a_g2'][l], p['a_kk'][l], p['a_ka'][l], p['a_rk'][l], p['a_ln_w'][l], p['a_ln_b'][l])
    mamba = mamba2_mixer_pallas if L % SSD_CHUNK == 0 else mamba2_mixer
    y_b, ssm_new, conv_new = mamba(
        b_z, b_xbc, rs(small) if L % SSD_CHUNK == 0 else b_dt, conv0, ssm0, p['b_conv_w'][l], p['b_conv_b'][l],
        p['b_dt_bias'][l], p['b_a_log'][l], p['b_d'][l], p['b_norm_w'][l])
    y_c, k_rows, v_rows, logf_rows = fox_mixer(
        c_q, c_k, c_v, c_f, p['c_f_bias'][l], p['c_qnorm_w'][l], p['c_knorm_w'][l], past, l)
    fl = lambda t: t.reshape(n * L, t.shape[-1])
    x1, u2, logits = out_projection(fl(y_a), fl(y_b), fl(y_c), p['w_out_bf16'][l], xg, g1, p['norm2_w'][l],
                                    sc2, sh2, p['router_w_pad'][l // 2], p['router_b_pad'][l // 2])
    dt = x.dtype
    states = (s_new.astype(dt), shift_new.astype(dt), ssm_new.astype(dt), conv_new.astype(dt),
              k_rows.astype(dt), v_rows.astype(dt), logf_rows.astype(dt))
    return x1.reshape(xg.shape), u2, logits, g2, states


def channel_mix(l, p, x1s, u2s, logits, g2s):
    u2 = jnp.concatenate(u2s, axis=0)
    j = l // 2
    if l % 2 == 0:
        f = dense_swiglu(u2, p['ffn_w_gate'][j], p['ffn_w_up'][j], p['ffn_w_down'][j], FFN_TILE_DENSE)
    else:
        lg = jnp.concatenate(logits, axis=0)[:, :N_EXPERTS]
        f = moe_swiglu(u2, lg, p['moe_w_gate'][j], p['moe_w_up'][j], p['moe_w_down'][j], FFN_TILE_EXPERT)
    outs, row0 = [], 0
    for x1, g2 in zip(x1s, g2s):
        outs.append(gated_residual(x1, g2, f, row0))
        row0 += x1.shape[0] * x1.shape[1]
    return outs


def run_trunk(xs, cs, p, init_states, pasts):
    outs = [[] for _ in xs]
    shapes = [x.shape for x in xs]
    for l in range(DEPTH):
        halves = [mix_group(x.reshape(s), c, l, p, *st[l], past)
                  for x, s, c, st, past in zip(xs, shapes, cs, init_states, pasts)]
        for o, h in zip(outs, halves):
            o.append(h[4])
        xs = channel_mix(l, p, [h[0] for h in halves], [h[1] for h in halves], [h[2] for h in halves],
                         [h[3] for h in halves])
    stacked = [[jnp.stack([o[i] for o in og]) for i in range(7)] for og in outs]
    return [x.reshape(s) for x, s in zip(xs, shapes)], stacked


def kernel(x_prompt, x_sample, cache_k, cache_v, cache_logf, state_rwkv, state_shift, state_ssm,
           state_conv, page_table, c_prompt, c_sample, norm1_w, norm2_w, w_ada, b_ada, w_in, w_out,
           a_mu, a_w0, a_w2, a_a0, a_a2, a_g2, a_kk, a_ka, a_rk, a_ln_w, a_ln_b,
           b_conv_w, b_conv_b, b_dt_bias, b_a_log, b_d, b_norm_w,
           c_f_bias, c_qnorm_w, c_knorm_w, ffn_w_gate, ffn_w_up, ffn_w_down,
           moe_router_w, moe_router_b, moe_w_gate, moe_w_up, moe_w_down):
    p = dict(norm1_w=norm1_w, norm2_w=norm2_w, w_ada=w_ada, b_ada=b_ada, w_in=w_in, w_out=w_out,
             a_mu=a_mu, a_w0=a_w0, a_w2=a_w2, a_a0=a_a0, a_a2=a_a2, a_g2=a_g2, a_kk=a_kk, a_ka=a_ka,
             a_rk=a_rk, a_ln_w=a_ln_w, a_ln_b=a_ln_b, b_conv_w=b_conv_w, b_conv_b=b_conv_b,
             b_dt_bias=b_dt_bias, b_a_log=b_a_log, b_d=b_d, b_norm_w=b_norm_w, c_f_bias=c_f_bias,
             c_qnorm_w=c_qnorm_w, c_knorm_w=c_knorm_w, ffn_w_gate=ffn_w_gate, ffn_w_up=ffn_w_up,
             ffn_w_down=ffn_w_down, moe_router_w=moe_router_w, moe_router_b=moe_router_b,
             moe_w_gate=moe_w_gate, moe_w_up=moe_w_up, moe_w_down=moe_w_down)
    o_dt, o_q, o_f = A_IN + B_WIDTH + B_CONV_DIM, A_IN + B_IN, A_IN + B_IN + 3 * C_WIDTH
    p['w_in_perm'] = jnp.concatenate(
        [w_in[:, :, :o_dt], w_in[:, :, o_q:o_f], w_in[:, :, o_dt:o_q], w_in[:, :, o_f:],
         jnp.zeros((DEPTH, D_MODEL, LANE - B_HEADS - C_HEADS), w_in.dtype)], axis=-1).astype(BF16)
    p['w_out_bf16'] = w_out.astype(BF16)
    p['router_w_pad'] = jnp.pad(moe_router_w, ((0, 0), (0, 0), (0, LANE - N_EXPERTS))).astype(BF16)
    p['router_b_pad'] = jnp.pad(moe_router_b.astype(F32), ((0, 0), (0, LANE - N_EXPERTS)))[:, None, :]
    for name in ('ffn_w_gate', 'ffn_w_up', 'ffn_w_down'):
        p[name] = p[name].astype(BF16)
    for name in ('moe_w_gate', 'moe_w_up', 'moe_w_down'):
        p[name] = jnp.stack([cast_bf16(w) for w in p[name]]) if N_MOE > 1 else cast_bf16(p[name][0])[None]
    n_p = x_prompt.shape[0]
    zero_state = (jnp.zeros((n_p, A_HEADS, HEAD_DIM, HEAD_DIM), jnp.float32),
                  jnp.zeros((n_p, A_IN), x_prompt.dtype),
                  jnp.zeros((n_p, B_HEADS, HEAD_DIM, D_STATE), jnp.float32),
                  jnp.zeros((n_p, CONV_W - 1, B_CONV_DIM), x_prompt.dtype))
    init_s = [(state_rwkv[l], state_shift[l], state_ssm[l], state_conv[l]) for l in range(DEPTH)]
    (y_prompt, y_sample), (st_p, st_s) = run_trunk(
        [x_prompt, x_sample], [c_prompt, c_sample], p, [[zero_state] * DEPTH, init_s],
        [None, (cache_k, cache_v, cache_logf, page_table)])
    rwkv_prompt, shift_prompt, ssm_prompt, conv_prompt, k_prompt, v_prompt, logf_prompt = st_p
    rwkv_sample, shift_sample, ssm_sample, conv_sample, k_sample, v_sample, logf_sample = st_s
    return (y_prompt, y_sample,
            k_prompt, v_prompt, logf_prompt, rwkv_prompt, shift_prompt, ssm_prompt, conv_prompt,
            k_sample, v_sample, logf_sample, rwkv_sample, shift_sample, ssm_sample, conv_sample)
```

```python
import math
import numpy as np
import jax
import jax.numpy as jnp
import functools
from jax import lax
from jax.experimental import pallas as pl
from jax.experimental.pallas import tpu as pltpu

D_MODEL = 1024
BATCH = 4
SEQ = 4096
DEPTH = 2
DEC_BATCH = 128
DEC_SEQ = 4
PAST_LEN = 2048
PAGE_SIZE = 128

D_MIX = D_MODEL
HEAD_DIM = 64
A_WIDTH = D_MIX // 4
A_HEADS = A_WIDTH // HEAD_DIM
A_DECAY_LORA = 64
A_ICL_LORA = 64
A_GATE_LORA = 128
A_IN = 3 * A_WIDTH + A_DECAY_LORA + A_ICL_LORA + A_GATE_LORA
GN_EPS = HEAD_DIM * 1e-5
B_WIDTH = D_MIX // 2
B_HEADS = B_WIDTH // HEAD_DIM
B_GROUPS = 2
D_STATE = 128
CONV_W = 4
SSD_CHUNK = 128
B_CONV_DIM = B_WIDTH + 2 * B_GROUPS * D_STATE
B_IN = B_WIDTH + B_CONV_DIM + B_HEADS
C_WIDTH = D_MIX - A_WIDTH - B_WIDTH
C_HEADS = C_WIDTH // HEAD_DIM
C_IN = 3 * C_WIDTH + C_HEADS
Q_BLOCK = 128
IN_WIDTH = A_IN + B_IN + C_IN
D_FF = 2816
N_EXPERTS = 8
TOP_K = 2
D_FF_EXPERT = 3584
N_DENSE = (DEPTH + 1) // 2
N_MOE = DEPTH // 2
NORM_EPS = 1e-6


BF16 = jnp.bfloat16
F32 = jnp.float32
LANE = 128
VMEM_LIMIT = 48 * 1024 * 1024
ROW_TILE = 256
FFN_ROW_TILE = 512
FFN_TILE_DENSE = D_FF // 2
FFN_TILE_EXPERT = D_FF_EXPERT // 2
ATT_TILE = 1024
CAST_BLOCK_ELEMS = 2048 * 1024
IN_MAIN = A_IN + B_WIDTH + B_CONV_DIM + 3 * C_WIDTH
IN_PAD = IN_MAIN + LANE


def _mod_spec(mod, tile):
    if mod.shape[1] == 1:
        return pl.BlockSpec((1, 1, mod.shape[2]), lambda g, i: (g, 0, 0))
    return pl.BlockSpec((1, tile, mod.shape[2]), lambda g, i: (g, i, 0))


def _modulated_rms(x, nw, sc, sh):
    y = x * lax.rsqrt(jnp.mean(x * x, axis=-1, keepdims=True) + NORM_EPS) * nw
    return y * (1.0 + sc) + sh


def _cast_kernel(x_ref, o_ref):
    o_ref[...] = x_ref[...].astype(o_ref.dtype)


def cast_bf16(w):
    E, A, B = w.shape
    ta = next(t for t in (1024, 512, 256, 128, 64, 32, 16) if t * B <= CAST_BLOCK_ELEMS and A % t == 0)
    spec = pl.BlockSpec((1, ta, B), lambda e, i: (e, i, 0))
    return pl.pallas_call(
        _cast_kernel, grid=(E, A // ta), in_specs=[spec], out_specs=spec,
        out_shape=jax.ShapeDtypeStruct(w.shape, BF16),
        compiler_params=pltpu.CompilerParams(dimension_semantics=("parallel", "parallel")),
        name="cast_bf16",
    )(w)


def _gated_residual_kernel(x_ref, g_ref, f_ref, o_ref):
    o_ref[0] = x_ref[0] + g_ref[0] * f_ref[...]


def gated_residual(x, g, f, row0):
    G, R, d = x.shape
    t = min(R, FFN_ROW_TILE)
    nt = R // t
    t0 = row0 // t
    return pl.pallas_call(
        _gated_residual_kernel,
        grid=(G, nt),
        in_specs=[pl.BlockSpec((1, t, d), lambda g, i: (g, i, 0)),
                  _mod_spec(g, t),
                  pl.BlockSpec((t, d), lambda g, i: (t0 + g * nt + i, 0))],
        out_specs=pl.BlockSpec((1, t, d), lambda g, i: (g, i, 0)),
        out_shape=jax.ShapeDtypeStruct(x.shape, x.dtype),
        name="gated_residual",
    )(x, g, f)


def _inproj_kernel(x_ref, nw_ref, sc_ref, sh_ref, w_ref, *out_refs):
    u = _modulated_rms(x_ref[0], nw_ref[...], sc_ref[0], sh_ref[0])
    p = jnp.dot(u.astype(BF16), w_ref[...], preferred_element_type=F32)
    o = 0
    for ref in out_refs:
        n = ref.shape[1]
        ref[...] = p[:, o:o + n]
        o += n


def in_projection(x, norm_w, sc, sh, w_perm):
    G, R, d = x.shape
    t = min(R, ROW_TILE)
    nt = R // t
    widths = (A_IN, B_WIDTH, B_CONV_DIM, C_WIDTH, C_WIDTH, C_WIDTH, LANE)
    return pl.pallas_call(
        _inproj_kernel,
        grid=(G, nt),
        in_specs=[pl.BlockSpec((1, t, d), lambda g, i: (g, i, 0)),
                  pl.BlockSpec((1, d), lambda g, i: (0, 0)),
                  _mod_spec(sc, t), _mod_spec(sh, t),
                  pl.BlockSpec((d, IN_PAD), lambda g, i: (0, 0))],
        out_specs=[pl.BlockSpec((t, n), lambda g, i: (g * nt + i, 0)) for n in widths],
        out_shape=[jax.ShapeDtypeStruct((G * R, n), F32) for n in widths],
        compiler_params=pltpu.CompilerParams(dimension_semantics=("parallel", "parallel"),
                                             vmem_limit_bytes=VMEM_LIMIT),
        name="in_projection",
    )(x, norm_w.reshape(1, d), sc, sh, w_perm)


def _outproj_kernel(ya_ref, yb_ref, yc_ref, w_ref, x_ref, g_ref, nw_ref, sc_ref, sh_ref, rw_ref, rb_ref,
                    x1_ref, u2_ref, lg_ref):
    acc = jnp.dot(ya_ref[...].astype(BF16), w_ref[0:A_WIDTH, :], preferred_element_type=F32)
    acc += jnp.dot(yb_ref[...].astype(BF16), w_ref[A_WIDTH:A_WIDTH + B_WIDTH, :], preferred_element_type=F32)
    acc += jnp.dot(yc_ref[...].astype(BF16), w_ref[A_WIDTH + B_WIDTH:, :], preferred_element_type=F32)
    x1 = x_ref[0] + g_ref[0] * acc
    x1_ref[...] = x1
    u2 = _modulated_rms(x1, nw_ref[...], sc_ref[0], sh_ref[0])
    u2_ref[...] = u2
    lg_ref[...] = jnp.dot(u2.astype(BF16), rw_ref[...], preferred_element_type=F32) + rb_ref[...]


def out_projection(ya, yb, yc, w_out, x, g1, norm_w, sc, sh, router_w, router_b):
    G, R, d = x.shape
    t = min(R, ROW_TILE)
    nt = R // t
    rows = lambda n: pl.BlockSpec((t, n), lambda g, i: (g * nt + i, 0))
    full = lambda a: pl.BlockSpec(a.shape, lambda g, i: (0,) * a.ndim)
    nw = norm_w.reshape(1, d)
    return pl.pallas_call(
        _outproj_kernel,
        grid=(G, nt),
        in_specs=[rows(A_WIDTH), rows(B_WIDTH), rows(C_WIDTH), full(w_out),
                  pl.BlockSpec((1, t, d), lambda g, i: (g, i, 0)), _mod_spec(g1, t), full(nw),
                  _mod_spec(sc, t), _mod_spec(sh, t), full(router_w), full(router_b)],
        out_specs=[rows(d), rows(d), rows(LANE)],
        out_shape=[jax.ShapeDtypeStruct((G * R, d), F32), jax.ShapeDtypeStruct((G * R, d), F32),
                   jax.ShapeDtypeStruct((G * R, LANE), F32)],
        compiler_params=pltpu.CompilerParams(dimension_semantics=("parallel", "parallel"),
                                             vmem_limit_bytes=VMEM_LIMIT),
        name="out_projection",
    )(ya, yb, yc, w_out, x, g1, nw, sc, sh, router_w, router_b)


def _swiglu_kernel(te_ref, tv_ref, u_ref, wg_ref, wu_ref, wd_ref, o_ref, acc_ref):
    i, j = pl.program_id(0), pl.program_id(1)
    last = pl.num_programs(1) - 1

    @pl.when(tv_ref[i] == 1)
    def _():
        u = u_ref[...].astype(BF16)
        g = jnp.dot(u, wg_ref[0], preferred_element_type=F32)
        up = jnp.dot(u, wu_ref[0], preferred_element_type=F32)
        h = (g * jax.nn.sigmoid(g) * up).astype(BF16)
        part = jnp.dot(h, wd_ref[0], preferred_element_type=F32)

        @pl.when(j == 0)
        def _():
            acc_ref[...] = part

        @pl.when(j > 0)
        def _():
            acc_ref[...] += part

    @pl.when(j == last)
    def _():
        o_ref[...] = jnp.where(tv_ref[i] == 1, acc_ref[...], 0.0)


def grouped_swiglu(u, w_gate, w_up, w_down, tile_expert, tile_valid, ff_tile):
    P, d = u.shape
    F = w_gate.shape[2]
    t = FFN_ROW_TILE
    grid_spec = pltpu.PrefetchScalarGridSpec(
        num_scalar_prefetch=2,
        grid=(P // t, F // ff_tile),
        in_specs=[pl.BlockSpec((t, d), lambda i, j, te, tv: (i, 0)),
                  pl.BlockSpec((1, d, ff_tile), lambda i, j, te, tv: (te[i], 0, j)),
                  pl.BlockSpec((1, d, ff_tile), lambda i, j, te, tv: (te[i], 0, j)),
                  pl.BlockSpec((1, ff_tile, d), lambda i, j, te, tv: (te[i], j, 0))],
        out_specs=pl.BlockSpec((t, d), lambda i, j, te, tv: (i, 0)),
        scratch_shapes=[pltpu.VMEM((t, d), F32)])
    return pl.pallas_call(
        _swiglu_kernel,
        grid_spec=grid_spec,
        out_shape=jax.ShapeDtypeStruct((P, d), F32),
        compiler_params=pltpu.CompilerParams(dimension_semantics=("parallel", "arbitrary"),
                                             vmem_limit_bytes=VMEM_LIMIT),
        name="grouped_swiglu",
    )(tile_expert, tile_valid, u, w_gate, w_up, w_down)


def _fox_flash_kernel(q_ref, k_ref, v_ref, cq_ref, ck_ref, o_ref, m_scr, l_scr, acc_scr):
    qi, ki = pl.program_id(1), pl.program_id(2)
    tq, tk, C = q_ref.shape[1], k_ref.shape[1], q_ref.shape[2]
    H = m_scr.shape[0]
    lane_head = lax.broadcasted_iota(jnp.int32, (tq, C), 1) // (C // H)

    @pl.when(ki == 0)
    def _():
        m_scr[...] = jnp.full(m_scr.shape, -jnp.inf, F32)
        l_scr[...] = jnp.zeros(l_scr.shape, F32)
        acc_scr[...] = jnp.zeros(acc_scr.shape, F32)

    def tile(on_diagonal):
        q, k, v = q_ref[0], k_ref[0], v_ref[0]
        acc = acc_scr[...]
        for h in range(H):
            mine = lane_head == h
            s = lax.dot_general(jnp.where(mine, q, jnp.zeros_like(q)), k, _NT, preferred_element_type=F32)
            s = s + cq_ref[0, h] - ck_ref[0, h]
            if on_diagonal:
                s = jnp.where(lax.broadcasted_iota(jnp.int32, (tq, tk), 1)
                              <= lax.broadcasted_iota(jnp.int32, (tq, tk), 0), s, -jnp.inf)
            m_prev = m_scr[h]
            m_new = jnp.maximum(m_prev, jnp.max(s, axis=-1, keepdims=True))
            alpha = jnp.exp(m_prev - m_new)
            p = jnp.exp(s - m_new)
            l_scr[h] = alpha * l_scr[h] + jnp.sum(p, axis=-1, keepdims=True)
            m_scr[h] = m_new
            pv = jnp.dot(p.astype(BF16), v, preferred_element_type=F32)
            acc = jnp.where(mine, alpha * acc + pv, acc)
        acc_scr[...] = acc

    pl.when(ki < qi)(functools.partial(tile, False))
    pl.when(ki == qi)(functools.partial(tile, True))

    @pl.when(ki == pl.num_programs(2) - 1)
    def _():
        l = jnp.zeros((tq, C), F32)
        for h in range(H):
            l = jnp.where(lane_head == h, l_scr[h], l)
        o_ref[0] = acc_scr[...] / l


def fox_flash_attention(q, k, v, cq, ck):
    N, L, C = q.shape
    H = cq.shape[1]
    t = min(L, ATT_TILE)
    nt = L // t
    kv_spec = pl.BlockSpec((1, t, C), lambda n, qi, ki: (n, jnp.minimum(ki, qi), 0))
    return pl.pallas_call(
        _fox_flash_kernel,
        grid=(N, nt, nt),
        in_specs=[pl.BlockSpec((1, t, C), lambda n, qi, ki: (n, qi, 0)), kv_spec, kv_spec,
                  pl.BlockSpec((1, H, t, 1), lambda n, qi, ki: (n, 0, qi, 0)),
                  pl.BlockSpec((1, H, 1, t), lambda n, qi, ki: (n, 0, 0, jnp.minimum(ki, qi)))],
        out_specs=pl.BlockSpec((1, t, C), lambda n, qi, ki: (n, qi, 0)),
        out_shape=jax.ShapeDtypeStruct((N, L, C), F32),
        scratch_shapes=[pltpu.VMEM((H, t, 1), F32), pltpu.VMEM((H, t, 1), F32), pltpu.VMEM((t, C), F32)],
        compiler_params=pltpu.CompilerParams(
            dimension_semantics=("parallel", "parallel", "arbitrary"), vmem_limit_bytes=VMEM_LIMIT),
        name="fox_flash_attention",
    )(q, k, v, cq, ck)


def _fox_prep_kernel(q_ref, k_ref, v_ref, qw_ref, kw_ref, qb_ref, kn_ref, kb_ref, vb_ref):
    C = q_ref.shape[1]
    same_head = (lax.broadcasted_iota(jnp.int32, (C, C), 0) // HEAD_DIM
                 == lax.broadcasted_iota(jnp.int32, (C, C), 1) // HEAD_DIM).astype(BF16)

    def head_norm(x, w):
        ms = _exact_rhs_mm(x * x, same_head) * (1.0 / HEAD_DIM)
        return x * lax.rsqrt(ms + NORM_EPS) * w

    qb_ref[...] = (head_norm(q_ref[...], qw_ref[...]) * (HEAD_DIM ** -0.5)).astype(BF16)
    kn = head_norm(k_ref[...], kw_ref[...])
    kn_ref[...] = kn
    kb_ref[...] = kn.astype(BF16)
    vb_ref[...] = v_ref[...].astype(BF16)


def fox_prep(q, k, v, qn_w, kn_w):
    T, C = q.shape
    t = min(T, ATT_TILE)
    rows = pl.BlockSpec((t, C), lambda i: (i, 0))
    w_spec = pl.BlockSpec((1, C), lambda i: (0, 0))
    tile_w = lambda w: jnp.tile(w.astype(F32), C // HEAD_DIM)[None, :]
    return pl.pallas_call(
        _fox_prep_kernel,
        grid=(T // t,),
        in_specs=[rows, rows, rows, w_spec, w_spec],
        out_specs=[rows, rows, rows, rows],
        out_shape=[jax.ShapeDtypeStruct((T, C), BF16), jax.ShapeDtypeStruct((T, C), F32),
                   jax.ShapeDtypeStruct((T, C), BF16), jax.ShapeDtypeStruct((T, C), BF16)],
        compiler_params=pltpu.CompilerParams(dimension_semantics=("parallel",)),
        name="fox_prep",
    )(q, k, v, tile_w(qn_w), tile_w(kn_w))


def _split(t, sizes):
    return jnp.split(t, np.cumsum(sizes)[:-1].tolist(), axis=-1)


def head_rms(t, w):
    tf = t.astype(jnp.float32)
    return (tf * lax.rsqrt(jnp.mean(tf * tf, axis=-1, keepdims=True) + NORM_EPS) * w).astype(t.dtype)


ADA_COL_TILE = 768


def _ada_kernel(c_ref, w_ref, b_ref, o_ref):
    c = c_ref[...]
    h = (c * jax.nn.sigmoid(c)).astype(BF16)
    o_ref[...] = jnp.dot(h, w_ref[...].astype(BF16), preferred_element_type=F32) + b_ref[...]


def ada_modulation(c, w_ada, b_ada):
    N, d = c.shape
    M = -(-N // SUBLANE) * SUBLANE
    n_out = w_ada.shape[1]
    m = pl.pallas_call(
        _ada_kernel,
        grid=(n_out // ADA_COL_TILE,),
        in_specs=[pl.BlockSpec((M, d), lambda j: (0, 0)),
                  pl.BlockSpec((d, ADA_COL_TILE), lambda j: (0, j)),
                  pl.BlockSpec((1, ADA_COL_TILE), lambda j: (0, j))],
        out_specs=pl.BlockSpec((M, ADA_COL_TILE), lambda j: (0, j)),
        out_shape=jax.ShapeDtypeStruct((M, n_out), F32),
        compiler_params=pltpu.CompilerParams(dimension_semantics=("parallel",)),
        name="ada_modulation",
    )(jnp.pad(c.astype(F32), ((0, M - N), (0, 0))), w_ada, b_ada.astype(F32)[None, :])
    return jnp.split(m[:N, None, :], 6, axis=-1)


def dense_swiglu(u, w_gate, w_up, w_down, ff_tile):
    nt = u.shape[0] // FFN_ROW_TILE
    return grouped_swiglu(u, w_gate[None], w_up[None], w_down[None],
                          jnp.zeros((nt,), jnp.int32), jnp.ones((nt,), jnp.int32), ff_tile)


def moe_swiglu(u, logits, w_gate, w_up, w_down, ff_tile):
    T, d = u.shape
    E = logits.shape[1]
    t = FFN_ROW_TILE
    top_val, top_idx = lax.top_k(logits, TOP_K)
    weights = jax.nn.softmax(top_val, axis=-1)
    eid = top_idx.reshape(-1).astype(jnp.int32)
    n_pairs = T * TOP_K
    order = jnp.argsort(eid, stable=True).astype(jnp.int32)
    counts = jnp.sum(jax.nn.one_hot(eid, E, dtype=jnp.int32), axis=0)
    padded = ((counts + t - 1) // t) * t
    ends_p = jnp.cumsum(padded)
    start_p = ends_p - padded
    start = jnp.cumsum(counts) - counts
    P = n_pairs + E * t
    tile_start = jnp.arange(P // t, dtype=jnp.int32) * t
    tile_expert = jnp.minimum(jnp.searchsorted(ends_p, tile_start, side='right'), E - 1).astype(jnp.int32)
    tile_valid = (tile_start < ends_p[-1]).astype(jnp.int32)
    per_row = lambda table: jnp.repeat(table[tile_expert], t)
    row_off = jnp.arange(P, dtype=jnp.int32) - per_row(start_p)
    row_ok = row_off < per_row(counts)
    src_pair = order[jnp.clip(per_row(start) + row_off, 0, n_pairs - 1)]
    src_tok = jnp.where(row_ok, src_pair // TOP_K, 0)
    rank = jnp.argsort(order).astype(jnp.int32)
    shift_e = jnp.sum(jax.nn.one_hot(eid, E, dtype=jnp.int32) * (start_p - start)[None, :], axis=1)
    pos_of_pair = (rank + shift_e).reshape(T, TOP_K)
    ys = grouped_swiglu(u[src_tok], w_gate, w_up, w_down, tile_expert, tile_valid, ff_tile)
    out = weights[:, 0:1] * ys[pos_of_pair[:, 0]]
    for s in range(1, TOP_K):
        out = out + weights[:, s:s + 1] * ys[pos_of_pair[:, s]]
    return out


RWKV_CHUNK = 64
RWKV_BLOCK = 256
RWKV_INV_BLOCK = 16

_NN = (((1,), (0,)), ((), ()))
_NT = (((1,), (1,)), ((), ()))


def _dot(a, b, dims):
    return lax.dot_general(a, b, dims, preferred_element_type=jnp.float32)


_BNN = (((2,), (1,)), ((0,), (0,)))
_BNT = (((2,), (2,)), ((0,), (0,)))


def _sp(x):
    hi = x.astype(BF16)
    lo = (x - hi.astype(F32)).astype(BF16)
    return hi, lo


def _mmp(a, b, dims=_BNN):
    (ah, al), (bh, bl) = a, b
    return _dot(ah, bh, dims) + _dot(ah, bl, dims) + _dot(al, bh, dims)


def _mm_exact_lhs(a_bf16, x):
    h1 = x.astype(BF16)
    r1 = x - h1.astype(F32)
    h2 = r1.astype(BF16)
    h3 = (r1 - h2.astype(F32)).astype(BF16)
    return _dot(a_bf16, h1, _BNN) + _dot(a_bf16, h2, _BNN) + _dot(a_bf16, h3, _BNN)


def _unit_lower_inverse(a, blk_mask, eye):
    ad = jnp.where(blk_mask, a, 0.0)
    e = a - ad
    ad_p = _sp(ad)
    p2 = _mmp(ad_p, ad_p)
    p2_p = _sp(p2)
    p4 = _mmp(p2_p, p2_p)
    p4_p = _sp(p4)
    p8 = _mmp(p4_p, p4_p)
    x = _mmp(_sp(eye - ad), _sp(eye + p2))
    x = _mmp(_sp(x), _sp(eye + p4))
    dinv_p = _sp(_mmp(_sp(x), _sp(eye + p8)))
    n = _mmp(dinv_p, _sp(e))
    n_p = _sp(n)
    n2 = _mmp(n_p, n_p)
    return _mmp(_sp(_mmp(_sp(eye - n), _sp(eye + n2))), dinv_p)


def _rwkv_chunk_prep(r, lw, k, v, vt, kk, a):
    B, T, _ = r.shape
    row = lax.broadcasted_iota(jnp.int32, (T, T), 0)
    col = lax.broadcasted_iota(jnp.int32, (T, T), 1)
    strict, incl = row > col, row >= col
    blk = (row // RWKV_INV_BLOCK) == (col // RWKV_INV_BLOCK)
    eye = (row == col).astype(F32)
    c = _mm_exact_lhs(jnp.broadcast_to(incl.astype(BF16), (B, T, T)), lw)
    c_last = c[:, T - 1:T, :]
    e_nc = jnp.exp(-c)
    e_end = jnp.exp(c_last - c)
    b = kk * a
    kt = _sp(kk * jnp.exp(c - lw))
    rt = _sp(r * jnp.exp(c))
    kh = _sp(k * e_nc)
    bh = _sp(b * e_nc)
    v_p, vt_p = _sp(v), _sp(vt)
    a_kb = jnp.where(strict, _mmp(kt, bh, _BNT), 0.0)
    a_kk = _sp(jnp.where(strict, _mmp(kt, kh, _BNT), 0.0))
    a_rk = _sp(jnp.where(incl, _mmp(rt, kh, _BNT), 0.0))
    a_rb = _sp(jnp.where(incl, _mmp(rt, bh, _BNT), 0.0))
    m = _sp(_unit_lower_inverse(a_kb, blk, eye))
    mk = _sp(_mmp(m, kt))
    g1 = _mmp(m, _sp(_mmp(a_kk, v_p)))
    g1t = _mmp(_sp(_mmp(vt_p, a_kk, _BNT)), m, _BNT)
    y0 = _mmp(a_rk, v_p)
    s_add = _mmp(vt_p, _sp(k * e_end))
    return dict(mk=mk, g1=g1, g1t=g1t, rt=rt, a_rb=a_rb, y0=y0, s_add=s_add, bbar=_sp(b * e_end),
                decay=jnp.exp(c_last))


def _rwkv_chunk_step(S, q, lo, hi):
    pick = lambda x: tuple(t[lo:hi] for t in x) if isinstance(x, tuple) else x[lo:hi]
    S_p, mk = _sp(S), pick(q['mk'])
    u = _mmp(mk, S_p, _BNT) + pick(q['g1'])
    y = _mmp(pick(q['rt']), S_p, _BNT) + pick(q['y0']) - _mmp(pick(q['a_rb']), _sp(u))
    ut = _mmp(S_p, mk, _BNT) + pick(q['g1t'])
    s_new = S * pick(q['decay']) + pick(q['s_add']) - _mmp(_sp(ut), pick(q['bbar']))
    return y, s_new


def _rwkv_kernel(r_ref, lw_ref, k_ref, v_ref, vt_ref, kk_ref, a_ref, s0_ref, y_ref, sT_ref, s_scr):
    H = r_ref.shape[1]
    T = RWKV_CHUNK

    @pl.when(pl.program_id(1) == 0)
    def _():
        s_scr[...] = s0_ref[0]

    J = RWKV_BLOCK // T
    rows = lambda ref: jnp.concatenate([ref[0, :, j * T:(j + 1) * T, :] for j in range(J)], axis=0)
    vt = jnp.concatenate([vt_ref[0, :, :, j * T:(j + 1) * T] for j in range(J)], axis=0)
    q = _rwkv_chunk_prep(rows(r_ref), rows(lw_ref), rows(k_ref), rows(v_ref), vt, rows(kk_ref), rows(a_ref))
    S = s_scr[...]
    for j in range(J):
        y, S = _rwkv_chunk_step(S, q, j * H, (j + 1) * H)
        y_ref[0, :, j * T:(j + 1) * T, :] = y
    s_scr[...] = S
    sT_ref[0] = S


def rwkv7_recurrence_pallas(r, lw, k, v, kk, a, s0):
    N, L, H, D = r.shape
    hm = lambda t: jnp.transpose(t, (0, 2, 1, 3))
    vt = jnp.transpose(v, (0, 2, 3, 1))
    row_spec = pl.BlockSpec((1, H, RWKV_BLOCK, D), lambda n, c: (n, 0, c, 0))
    st_spec = pl.BlockSpec((1, H, D, D), lambda n, c: (n, 0, 0, 0))
    y, s_last = pl.pallas_call(
        _rwkv_kernel,
        grid=(N, L // RWKV_BLOCK),
        in_specs=[row_spec, row_spec, row_spec, row_spec,
                  pl.BlockSpec((1, H, D, RWKV_BLOCK), lambda n, c: (n, 0, 0, c)),
                  row_spec, row_spec, st_spec],
        out_specs=[row_spec, st_spec],
        out_shape=[jax.ShapeDtypeStruct((N, H, L, D), jnp.float32),
                   jax.ShapeDtypeStruct((N, H, D, D), jnp.float32)],
        scratch_shapes=[pltpu.VMEM((H, D, D), jnp.float32)],
        compiler_params=pltpu.CompilerParams(dimension_semantics=("parallel", "arbitrary")),
        name="rwkv7_chunked",
    )(hm(r), hm(lw), hm(k), hm(v), vt, hm(kk), hm(a), s0)
    return jnp.transpose(y, (0, 2, 1, 3)), s_last


RWKV_SHORT_SEQS = 8


def _rwkv_short_kernel(r_ref, w_ref, k_ref, vt_ref, kk_ref, a_ref, s0_ref, yt_ref, sT_ref):
    nb, H, L, D = r_ref.shape
    B = nb * H
    lane_t = lax.broadcasted_iota(jnp.int32, (B, D, L), 2)
    S = s0_ref[...].reshape(B, D, D)
    yt = jnp.zeros((B, D, L), F32)
    for t in range(L):
        row = lambda ref: ref[:, :, t:t + 1, :].reshape(B, 1, D)
        kk_t = row(kk_ref)
        s_kk = jnp.sum(S * kk_t, axis=-1, keepdims=True)
        S = S * row(w_ref) - s_kk * (kk_t * row(a_ref)) + vt_ref[:, :, :, t:t + 1].reshape(B, D, 1) * row(k_ref)
        yt = jnp.where(lane_t == t, jnp.sum(S * row(r_ref), axis=-1, keepdims=True), yt)
    yt_ref[...] = yt.reshape(nb, H, D, L)
    sT_ref[...] = S.reshape(nb, H, D, D)


def rwkv7_recurrence_short(r, decay, k, v, kk, a, s0):
    N, L, H, D = r.shape
    nb = RWKV_SHORT_SEQS
    hm = lambda t: jnp.transpose(t, (0, 2, 1, 3))
    row_spec = pl.BlockSpec((nb, H, L, D), lambda n: (n, 0, 0, 0))
    col_spec = pl.BlockSpec((nb, H, D, L), lambda n: (n, 0, 0, 0))
    st_spec = pl.BlockSpec((nb, H, D, D), lambda n: (n, 0, 0, 0))
    yt, s_last = pl.pallas_call(
        _rwkv_short_kernel,
        grid=(N // nb,),
        in_specs=[row_spec, row_spec, row_spec, col_spec, row_spec, row_spec, st_spec],
        out_specs=[col_spec, st_spec],
        out_shape=[jax.ShapeDtypeStruct((N, H, D, L), F32), jax.ShapeDtypeStruct((N, H, D, D), F32)],
        compiler_params=pltpu.CompilerParams(dimension_semantics=("parallel",)),
        name="rwkv7_short",
    )(hm(r), hm(decay), hm(k), jnp.transpose(v, (0, 2, 3, 1)), hm(kk), hm(a), s0)
    return jnp.transpose(yt, (0, 3, 1, 2)), s_last


RWKV_PREP_TILE = 512


def _rwkv_prep_kernel(cols_ref, prev_ref, shift0_ref, mu_ref, w0_ref, w2_ref, a0_ref, a2_ref, g2_ref, kkw_ref,
                      ka_ref, r_ref, lw_ref, k_ref, v_ref, kk_ref, a_ref, g_ref):
    i = pl.program_id(1)
    cols = cols_ref[0]
    t = cols.shape[0]
    before = jnp.where(i == 0, shift0_ref[0], prev_ref[SUBLANE - 1:SUBLANE, :])
    prev = jnp.where(lax.broadcasted_iota(jnp.int32, cols.shape, 0) == 0, before, pltpu.roll(cols, 1, 0))
    xs = cols + (prev - cols) * mu_ref[...]
    W = A_WIDTH
    r, k, v = xs[:, 0:W], xs[:, W:2 * W], xs[:, 2 * W:3 * W]
    lora_in = xs[:, 3 * W:3 * W + A_DECAY_LORA + A_ICL_LORA]
    xg = xs[:, 3 * W + A_DECAY_LORA + A_ICL_LORA:]
    pre = w0_ref[...] + jnp.dot(jnp.tanh(lora_in).astype(BF16), w2_ref[...], preferred_element_type=F32)
    w = -(jnp.maximum(-pre, 0.0) + jnp.log(1.0 + jnp.exp(-jnp.abs(pre)))) - 0.5
    a = jax.nn.sigmoid(a0_ref[...] + jnp.dot(lora_in.astype(BF16), a2_ref[...], preferred_element_type=F32))
    same_head = (lax.broadcasted_iota(jnp.int32, (W, W), 0) // HEAD_DIM
                 == lax.broadcasted_iota(jnp.int32, (W, W), 1) // HEAD_DIM).astype(BF16)
    kk = k * kkw_ref[...]
    norm = jnp.sqrt(_exact_rhs_mm(kk * kk, same_head))
    r_ref[0] = r
    lw_ref[0] = -jnp.exp(w)
    k_ref[0] = k * (1.0 + (a - 1.0) * ka_ref[...])
    v_ref[0] = v
    kk_ref[0] = kk / jnp.maximum(norm, 1e-12)
    a_ref[0] = a
    g_ref[0] = jnp.dot(jax.nn.sigmoid(xg).astype(BF16), g2_ref[...], preferred_element_type=F32)


def rwkv7_prep(cols, shift0, mu, w0, w2, a0, a2, g2, k_k, k_a):
    n, L, _ = cols.shape
    t = min(L, RWKV_PREP_TILE)
    nt = L // t
    row = lambda v: v.astype(F32)[None, :]
    w2p = jnp.pad(w2, ((0, A_ICL_LORA), (0, 0))).astype(BF16)
    a2p = jnp.pad(a2, ((A_DECAY_LORA, 0), (0, 0))).astype(BF16)
    full = lambda a: pl.BlockSpec(a.shape, lambda b, i: (0,) * a.ndim)
    consts = (row(mu), row(w0), w2p, row(a0), a2p, g2.astype(BF16), row(k_k), row(k_a))
    out = pl.BlockSpec((1, t, A_WIDTH), lambda b, i: (b, i, 0))
    prev_spec = pl.BlockSpec((SUBLANE, A_IN), lambda b, i: (jnp.maximum((b * L + i * t) // SUBLANE - 1, 0), 0))
    return pl.pallas_call(
        _rwkv_prep_kernel,
        grid=(n, nt),
        in_specs=[pl.BlockSpec((1, t, A_IN), lambda b, i: (b, i, 0)), prev_spec,
                  pl.BlockSpec((1, 1, A_IN), lambda b, i: (b, 0, 0))] + [full(c) for c in consts],
        out_specs=[out] * 7,
        out_shape=[jax.ShapeDtypeStruct((n, L, A_WIDTH), F32)] * 7,
        compiler_params=pltpu.CompilerParams(dimension_semantics=("parallel", "parallel")),
        name="rwkv7_prep",
    )(cols, cols.reshape(n * L, A_IN), shift0.astype(F32)[:, None, :], *consts)


def _rwkv_post_kernel(y_ref, r_ref, k_ref, v_ref, g_ref, rk_ref, lnw_ref, lnb_ref, o_ref):
    W = y_ref.shape[1]
    same_head = (lax.broadcasted_iota(jnp.int32, (W, W), 0) // HEAD_DIM
                 == lax.broadcasted_iota(jnp.int32, (W, W), 1) // HEAD_DIM).astype(BF16)
    head_sum = lambda x: _exact_rhs_mm(x, same_head)
    y = y_ref[...]
    d = y - head_sum(y) * (1.0 / HEAD_DIM)
    var = head_sum(d * d) * (1.0 / HEAD_DIM)
    gn = d * lax.rsqrt(var + GN_EPS) * lnw_ref[...] + lnb_ref[...]
    bonus = head_sum(r_ref[...] * k_ref[...] * rk_ref[...]) * v_ref[...]
    o_ref[...] = (gn + bonus) * g_ref[...]


def rwkv7_post(y, r, k, v, g, r_k, ln_w, ln_b):
    T, W = y.shape
    t = min(T, RWKV_PREP_TILE)
    rows = pl.BlockSpec((t, W), lambda i: (i, 0))
    vec = pl.BlockSpec((1, W), lambda i: (0, 0))
    row = lambda a: a.astype(F32).reshape(1, W)
    return pl.pallas_call(
        _rwkv_post_kernel,
        grid=(T // t,),
        in_specs=[rows] * 5 + [vec] * 3,
        out_specs=rows,
        out_shape=jax.ShapeDtypeStruct((T, W), F32),
        compiler_params=pltpu.CompilerParams(dimension_semantics=("parallel",)),
        name="rwkv7_post",
    )(y, r, k, v, g, row(r_k), row(ln_w), row(ln_b))


def rwkv7_mixer(cols, shift0, s0, mu, w0, w2, a0, a2, g2, k_k, k_a, r_k, ln_w, ln_b):
    n, L, _ = cols.shape
    f32 = jnp.float32
    heads = lambda t: t.astype(f32).reshape(n, L, A_HEADS, HEAD_DIM)
    if L % RWKV_BLOCK == 0:
        r, lw, k, v, kk, a, g = rwkv7_prep(cols, shift0, mu, w0, w2, a0, a2, g2, k_k, k_a)
        r_h, k_h, v_h = heads(r), heads(k), heads(v)
        y, s_last = rwkv7_recurrence_pallas(r_h, heads(lw), k_h, v_h, heads(kk), heads(a), s0.astype(f32))
    else:
        assert L <= SUBLANE and n % RWKV_SHORT_SEQS == 0, (n, L)
        prev = jnp.concatenate([shift0[:, None, :].astype(cols.dtype), cols[:, :-1]], axis=1)
        xs = cols + (prev - cols) * mu
        r, k, v, xw, xa, xg = _split(xs, [A_WIDTH, A_WIDTH, A_WIDTH, A_DECAY_LORA, A_ICL_LORA, A_GATE_LORA])
        w = -jax.nn.softplus(-(w0 + jnp.tanh(xw) @ w2)) - 0.5
        decay = jnp.exp(-jnp.exp(w.astype(f32)))
        a = jax.nn.sigmoid(a0 + xa @ a2)
        g = jax.nn.sigmoid(xg) @ g2
        kk = heads(k * k_k)
        kk = kk / jnp.maximum(jnp.sqrt(jnp.sum(kk * kk, axis=-1, keepdims=True)), 1e-12)
        k = k * (1.0 + (a - 1.0) * k_a)
        r_h, k_h, v_h = heads(r), heads(k), heads(v)
        y, s_last = rwkv7_recurrence_short(r_h, heads(decay), k_h, v_h, kk, heads(a), s0.astype(f32))
    fl = lambda t: t.reshape(n * L, A_WIDTH)
    y = rwkv7_post(fl(y), fl(r_h), fl(k_h), fl(v_h), fl(g), r_k, ln_w, ln_b).reshape(n, L, A_WIDTH)
    return y.astype(cols.dtype), s_last, cols[:, -1]


def ssd_chunked(x, dt, A, B, C, h0):
    N, L, H, P = x.shape
    q = min(SSD_CHUNK, L)
    pad = (-L) % q
    if pad:
        padl = lambda t: jnp.pad(t, [(0, 0), (0, pad)] + [(0, 0)] * (t.ndim - 2))
        x, dt, B, C = padl(x), padl(dt), padl(B), padl(C)
    nc = (L + pad) // q
    ch = lambda t: t.reshape((N, nc, q) + t.shape[2:])
    xc, dtc, Bc, Cc = ch(x), ch(dt), ch(B), ch(C)
    cs = jnp.cumsum(dtc * A, axis=2)
    causal = jnp.tril(jnp.ones((q, q), bool))[None, None, :, :, None]
    seg = cs[:, :, :, None, :] - cs[:, :, None, :, :]
    decay = jnp.exp(jnp.where(causal, seg, -jnp.inf))
    scores = jnp.einsum('nclhd,ncshd->nclsh', Cc, Bc) * decay * dtc[:, :, None, :, :]
    y_intra = jnp.einsum('nclsh,ncshp->nclhp', scores, xc)
    to_end = jnp.exp(cs[:, :, -1:, :] - cs) * dtc
    chunk_states = jnp.einsum('ncsh,ncshd,ncshp->nchpd', to_end, Bc, xc)
    chunk_decay = jnp.exp(cs[:, :, -1, :])
    def carry(h, inp):
        dec, st = inp
        return h * dec[:, :, None, None] + st, h
    h_last, h_in = lax.scan(carry, h0, (jnp.moveaxis(chunk_decay, 1, 0), jnp.moveaxis(chunk_states, 1, 0)))
    h_in = jnp.moveaxis(h_in, 0, 1)
    y_inter = jnp.einsum('nclhd,nchpd->nclhp', Cc, h_in) * jnp.exp(cs)[..., None]
    y = (y_intra + y_inter).reshape(N, nc * q, H, P)[:, :L]
    return y, h_last


def mamba2_mixer(z, xbc, dt_raw, conv0, ssm0, conv_w, conv_b, dt_bias, a_log, d_skip, norm_w):
    n, L, _ = xbc.shape
    f32 = jnp.float32
    xpad = jnp.concatenate([conv0.astype(xbc.dtype), xbc], axis=1)
    conv = conv_b + sum(xpad[:, j:j + L] * conv_w[j] for j in range(CONV_W))
    xs, Bm, Cm = _split(jax.nn.silu(conv), [B_WIDTH, B_GROUPS * D_STATE, B_GROUPS * D_STATE])
    xh = xs.astype(f32).reshape(n, L, B_HEADS, HEAD_DIM)
    rep = B_HEADS // B_GROUPS
    Bh = jnp.repeat(Bm.astype(f32).reshape(n, L, B_GROUPS, D_STATE), rep, axis=2)
    Ch = jnp.repeat(Cm.astype(f32).reshape(n, L, B_GROUPS, D_STATE), rep, axis=2)
    dt = jax.nn.softplus(dt_raw.astype(f32) + dt_bias)
    A = -jnp.exp(a_log.astype(f32))
    y, h_last = ssd_chunked(xh, dt, A, Bh, Ch, ssm0.astype(f32))
    y = (y + d_skip[:, None] * xh).reshape(n, L, B_WIDTH) * jax.nn.silu(z.astype(f32))
    yg = y.reshape(n, L, B_GROUPS, B_WIDTH // B_GROUPS)
    yg = yg * lax.rsqrt(jnp.mean(yg * yg, axis=-1, keepdims=True) + NORM_EPS)
    y = yg.reshape(n, L, B_WIDTH) * norm_w
    return y.astype(z.dtype), h_last, xpad[:, -(CONV_W - 1):]


SSD_PAIRS = B_HEADS // 2
CONV_TAIL = 8


def _exact_rhs_mm(x, sel_bf16):
    h1 = x.astype(BF16)
    r1 = x - h1.astype(F32)
    h2 = r1.astype(BF16)
    h3 = (r1 - h2.astype(F32)).astype(BF16)
    d = lambda a: jnp.dot(a, sel_bf16, preferred_element_type=F32)
    return d(h1) + d(h2) + d(h3)


def _ssd_kernel(xbc_ref, z_ref, sm_ref, conv0_ref, h0_ref, cw_ref, cb_ref, dtb_ref, a_ref, dskip_ref, nw_ref,
                y_ref, hT_ref, xbuf, h_scr):
    Q = SSD_CHUNK
    c = pl.program_id(1)

    @pl.when(c == 0)
    def _():
        xbuf[0:CONV_TAIL, :] = conv0_ref[0]
        h_scr[...] = h0_ref[0]

    @pl.when(c > 0)
    def _():
        xbuf[0:CONV_TAIL, :] = xbuf[Q:Q + CONV_TAIL, :]

    xbuf[CONV_TAIL:CONV_TAIL + Q, :] = xbc_ref[0]
    conv = cb_ref[...]
    for j in range(CONV_W):
        conv = conv + cw_ref[j:j + 1, :] * xbuf[pl.ds(CONV_TAIL - (CONV_W - 1) + j, Q), :]
    act = conv * jax.nn.sigmoid(conv)
    xs = act[:, :B_WIDTH]
    n_bc = B_GROUPS * D_STATE
    Bm, Cm = act[:, B_WIDTH:B_WIDTH + n_bc], act[:, B_WIDTH + n_bc:]

    pre = sm_ref[0] + dtb_ref[...]
    dt = jnp.maximum(pre, 0.0) + jnp.log(1.0 + jnp.exp(-jnp.abs(pre)))
    dA = dt * a_ref[...]
    row = lax.broadcasted_iota(jnp.int32, (Q, Q), 0)
    col = lax.broadcasted_iota(jnp.int32, (Q, Q), 1)
    causal = row >= col
    tri = causal.astype(BF16)
    h1 = dA.astype(BF16)
    r1 = dA - h1.astype(F32)
    h2 = r1.astype(BF16)
    h3 = (r1 - h2.astype(F32)).astype(BF16)
    cs = (jnp.dot(tri, h1, preferred_element_type=F32) + jnp.dot(tri, h2, preferred_element_type=F32)
          + jnp.dot(tri, h3, preferred_element_type=F32))
    csT, dtT = cs.T, dt.T
    sel_b = (lax.broadcasted_iota(jnp.int32, (LANE, B_WIDTH), 0)
             == lax.broadcasted_iota(jnp.int32, (LANE, B_WIDTH), 1) // HEAD_DIM).astype(BF16)
    sel_h = (lax.broadcasted_iota(jnp.int32, (LANE, B_HEADS * Q), 0)
             == lax.broadcasted_iota(jnp.int32, (LANE, B_HEADS * Q), 1) // Q).astype(BF16)
    cs_b = _exact_rhs_mm(cs, sel_b)
    dt_b = _exact_rhs_mm(dt, sel_b)
    cs_full = _exact_rhs_mm(cs, sel_h)
    cs_last_b = cs_b[Q - 1:Q, :]
    e_b = jnp.exp(cs_b)
    xs_bf = xs.astype(BF16)
    xs_te = (xs * (jnp.exp(cs_last_b - cs_b) * dt_b)).astype(BF16)
    first_half = lax.broadcasted_iota(jnp.int32, (Q, LANE), 1) < HEAD_DIM
    ys = []
    for g in range(B_GROUPS):
        gs = slice(g * D_STATE, (g + 1) * D_STATE)
        Cg, Bg = Cm[:, gs].astype(BF16), Bm[:, gs]
        cb = lax.dot_general(Cg, Bg.astype(BF16), _NT, preferred_element_type=F32)
        BTg = Bg.T.astype(BF16)
        for pp in range(SSD_PAIRS // B_GROUPS):
            pair = g * (SSD_PAIRS // B_GROUPS) + pp
            lanes = slice(pair * LANE, (pair + 1) * LANE)
            hT = h_scr[pair]
            y_pair = jnp.dot(Cg, hT.astype(BF16), preferred_element_type=F32) * e_b[:, lanes]
            for j in range(2):
                h = 2 * pair + j
                seg = cs_full[:, h * Q:(h + 1) * Q] - csT[h:h + 1, :]
                dec = jnp.exp(jnp.where(causal, seg, -jnp.inf))
                sc = (cb * dec * dtT[h:h + 1, :]).astype(BF16)
                xm = jnp.where(first_half if j == 0 else jnp.logical_not(first_half), xs_bf[:, lanes], 0.0)
                y_pair = y_pair + jnp.dot(sc, xm.astype(BF16), preferred_element_type=F32)
            h_scr[pair] = hT * jnp.exp(cs_last_b[:, lanes]) + jnp.dot(BTg, xs_te[:, lanes],
                                                                      preferred_element_type=F32)
            ys.append(y_pair)
    y = jnp.concatenate(ys, axis=-1)
    z = z_ref[0]
    y = (y + dskip_ref[...] * xs) * (z * jax.nn.sigmoid(z))
    gw = B_WIDTH // B_GROUPS
    outs = []
    for g in range(B_GROUPS):
        yg = y[:, g * gw:(g + 1) * gw]
        outs.append(yg * lax.rsqrt(jnp.mean(yg * yg, axis=-1, keepdims=True) + NORM_EPS))
    y_ref[0] = jnp.concatenate(outs, axis=-1) * nw_ref[...]
    hT_ref[0] = h_scr[...]


def mamba2_mixer_pallas(z, xbc, small, conv0, ssm0, conv_w, conv_b, dt_bias, a_log, d_skip, norm_w):
    n, L, _ = xbc.shape
    Q = SSD_CHUNK
    pad_l = lambda v: jnp.pad(v.astype(F32), (0, LANE - v.shape[0]))[None, :]
    tail0 = jnp.pad(conv0.astype(F32), ((0, 0), (CONV_TAIL - (CONV_W - 1), 0), (0, 0)))
    h0 = ssm0.astype(F32).reshape(n, SSD_PAIRS, 2, HEAD_DIM, D_STATE)
    h0 = jnp.transpose(h0, (0, 1, 4, 2, 3)).reshape(n, SSD_PAIRS, D_STATE, 2 * HEAD_DIM)
    full = lambda a: pl.BlockSpec(a.shape, lambda i, c: (0,) * a.ndim)
    seq = lambda w: pl.BlockSpec((1, Q, w), lambda i, c: (i, c, 0))
    args = (xbc, z, small, tail0, h0, conv_w.astype(F32), conv_b.astype(F32)[None, :], pad_l(dt_bias),
            pad_l(-jnp.exp(a_log.astype(F32))), jnp.repeat(d_skip.astype(F32), HEAD_DIM)[None, :],
            norm_w.astype(F32)[None, :])
    st_spec = pl.BlockSpec((1, SSD_PAIRS, D_STATE, 2 * HEAD_DIM), lambda i, c: (i, 0, 0, 0))
    y, hT = pl.pallas_call(
        _ssd_kernel,
        grid=(n, L // Q),
        in_specs=[seq(B_CONV_DIM), seq(B_WIDTH), seq(LANE),
                  pl.BlockSpec((1, CONV_TAIL, B_CONV_DIM), lambda i, c: (i, 0, 0)), st_spec]
                 + [full(a) for a in args[5:]],
        out_specs=[seq(B_WIDTH), st_spec],
        out_shape=[jax.ShapeDtypeStruct((n, L, B_WIDTH), F32),
                   jax.ShapeDtypeStruct((n, SSD_PAIRS, D_STATE, 2 * HEAD_DIM), F32)],
        scratch_shapes=[pltpu.VMEM((Q + CONV_TAIL, B_CONV_DIM), F32),
                        pltpu.VMEM((SSD_PAIRS, D_STATE, 2 * HEAD_DIM), F32)],
        compiler_params=pltpu.CompilerParams(dimension_semantics=("parallel", "arbitrary"),
                                             vmem_limit_bytes=VMEM_LIMIT),
        name="mamba2_ssd",
    )(*args)
    h_last = jnp.transpose(hT.reshape(n, SSD_PAIRS, D_STATE, 2, HEAD_DIM), (0, 1, 3, 4, 2))
    h_last = h_last.reshape(n, B_HEADS, HEAD_DIM, D_STATE)
    tail = jnp.concatenate([conv0.astype(xbc.dtype), xbc[:, -(CONV_W - 1):]], axis=1)[:, -(CONV_W - 1):]
    return y, h_last, tail


SUBLANE = 8


def _fox_paged_kernel(pt_ref, qbd_ref, knew_ref, vnew_ref, cq_ref, ckn_ref, *refs, n_pages, n_new):
    k_refs, v_refs, lf_refs = refs[:n_pages], refs[n_pages:2 * n_pages], refs[2 * n_pages:3 * n_pages]
    o_ref = refs[3 * n_pages]
    PS = k_refs[0].shape[2]
    qbd = qbd_ref[0]
    cq = cq_ref[0]
    lf = jnp.concatenate([r[0] for r in lf_refs], axis=0)
    after = (lax.broadcasted_iota(jnp.int32, (PS, PS), 0)
             > lax.broadcasted_iota(jnp.int32, (PS, PS), 1)).astype(BF16)
    suf = _exact_rhs_mm(lf, after)
    page_sum = jnp.sum(lf, axis=-1, keepdims=True)
    tails = [jnp.zeros((SUBLANE, 1), F32)]
    for j in range(n_pages - 1, 0, -1):
        tails.append(tails[-1] + page_sum[j * SUBLANE:(j + 1) * SUBLANE])
    tails = tails[::-1]
    scores = []
    for j in range(n_pages):
        s = jnp.dot(qbd, k_refs[j][0].astype(BF16), preferred_element_type=F32)
        brow = suf[j * SUBLANE:(j + 1) * SUBLANE] + tails[j]
        bias = jnp.concatenate([jnp.broadcast_to(brow[h:h + 1], (SUBLANE, PS)) for h in range(C_HEADS)], axis=0)
        scores.append(s + cq + bias)
    sn = lax.dot_general(qbd, knew_ref[0], _NT, preferred_element_type=F32)
    t_idx = lax.broadcasted_iota(jnp.int32, sn.shape, 0) % SUBLANE
    s_idx = lax.broadcasted_iota(jnp.int32, sn.shape, 1)
    sn = jnp.where((s_idx <= t_idx) & (s_idx < n_new), sn + cq - ckn_ref[0], -jnp.inf)
    m = jnp.max(sn, axis=-1, keepdims=True)
    for s in scores:
        m = jnp.maximum(m, jnp.max(s, axis=-1, keepdims=True))
    pn = jnp.exp(sn - m)
    l = jnp.sum(pn, axis=-1, keepdims=True)
    acc = jnp.dot(pn.astype(BF16), vnew_ref[0], preferred_element_type=F32)
    for j in range(n_pages):
        pj = jnp.exp(scores[j] - m)
        l = l + jnp.sum(pj, axis=-1, keepdims=True)
        acc = acc + lax.dot_general(pj.astype(BF16), v_refs[j][0].astype(BF16), _NT, preferred_element_type=F32)
    out = acc / l
    lane_head = lax.broadcasted_iota(jnp.int32, (SUBLANE, C_WIDTH), 1) // HEAD_DIM
    o = jnp.zeros((SUBLANE, C_WIDTH), F32)
    for h in range(C_HEADS):
        o = o + jnp.where(lane_head == h, out[h * SUBLANE:(h + 1) * SUBLANE], 0.0)
    o_ref[0] = o


def fox_paged_attention(qh, kh, vh, logf, pools_k, pools_v, pools_logf, page_table, layer):
    n, L, H, hd = qh.shape
    depth, n_phys, PS = pools_k.shape[:3]
    n_pages = page_table.shape[1]
    R = H * SUBLANE
    pad_t = lambda t: jnp.pad(t, ((0, 0), (0, 0), (0, SUBLANE - L)) + ((0, 0),) * (t.ndim - 3))
    q_hm = pad_t(jnp.transpose(qh * (hd ** -0.5), (0, 2, 1, 3)))
    qbd = (jnp.eye(H, dtype=F32)[None, :, None, :, None] * q_hm[:, :, :, None, :]).reshape(n, R, H * hd)
    new_rows = lambda t: jnp.pad(t.reshape(n, L, H * hd), ((0, 0), (0, SUBLANE - L), (0, 0))).astype(BF16)
    cum = pad_t(jnp.transpose(jnp.cumsum(logf, axis=1), (0, 2, 1)))
    cq = cum.reshape(n, R, 1)
    ckn = jnp.broadcast_to(cum[:, :, None, :], (n, H, SUBLANE, SUBLANE)).reshape(n, R, SUBLANE)
    lf_t = jnp.pad(jnp.transpose(pools_logf[layer].astype(F32), (0, 2, 1)), ((0, 0), (0, SUBLANE - H), (0, 0)))
    pk, pv = (jnp.transpose(t, (0, 1, 3, 4, 2)).reshape(depth * n_phys, H * hd, PS) for t in (pools_k, pools_v))
    base = layer * n_phys
    seq = lambda r, c: pl.BlockSpec((1, r, c), lambda i, pt: (i, 0, 0))
    page = lambda j, r, c, off: pl.BlockSpec((1, r, c), lambda i, pt, j=j: (pt[i, j] + off, 0, 0))
    grid_spec = pltpu.PrefetchScalarGridSpec(
        num_scalar_prefetch=1,
        grid=(n,),
        in_specs=[seq(R, H * hd), seq(SUBLANE, H * hd), seq(SUBLANE, H * hd), seq(R, 1), seq(R, SUBLANE)]
                 + [page(j, H * hd, PS, base) for j in range(n_pages)] * 2
                 + [page(j, SUBLANE, PS, 0) for j in range(n_pages)],
        out_specs=seq(SUBLANE, H * hd))
    o = pl.pallas_call(
        functools.partial(_fox_paged_kernel, n_pages=n_pages, n_new=L),
        grid_spec=grid_spec,
        out_shape=jax.ShapeDtypeStruct((n, SUBLANE, H * hd), F32),
        compiler_params=pltpu.CompilerParams(dimension_semantics=("parallel",), vmem_limit_bytes=VMEM_LIMIT),
        name="fox_paged_attention",
    )(page_table, qbd.astype(BF16), new_rows(kh), new_rows(vh), cq, ckn,
      *([pk] * n_pages), *([pv] * n_pages), *([lf_t] * n_pages))
    return o[:, :L]


def fox_mixer(q, k, v, f_raw, f_bias, qn_w, kn_w, past, layer):
    n, L, _ = q.shape
    vh = v.reshape(n, L, C_HEADS, HEAD_DIM)
    logf = jax.nn.log_sigmoid(f_raw.astype(jnp.float32) + f_bias)
    if past is None and L % ATT_TILE == 0:
        fl = lambda t: t.reshape(n * L, C_WIDTH)
        qb, kn, kb, vb = (t.reshape(n, L, C_WIDTH) for t in fox_prep(fl(q), fl(k), fl(v), qn_w, kn_w))
        cum = jnp.transpose(jnp.cumsum(logf, axis=1), (0, 2, 1))
        o = fox_flash_attention(qb, kb, vb, cum[..., None], cum[:, :, None, :])
        return o.astype(q.dtype), kn.reshape(n, L, C_HEADS, HEAD_DIM), vh, logf
    qh = head_rms(q.reshape(n, L, C_HEADS, HEAD_DIM), qn_w)
    kh = head_rms(k.reshape(n, L, C_HEADS, HEAD_DIM), kn_w)
    assert past is not None and L <= SUBLANE, "supported: long fresh sequences, or few new tokens over a paged past"
    o = fox_paged_attention(qh, kh, vh, logf, *past, layer)
    return o.reshape(n, L, C_WIDTH).astype(q.dtype), kh, vh, logf


def mix_group(x, c, l, p, s0, shift0, ssm0, conv0, past):
    n, L, d = x.shape
    mods = ada_modulation(c, p['w_ada'][l], p['b_ada'][l])
    if L >= ROW_TILE:
        xg = x
    else:
        xg = x.reshape(1, n * L, d)
        mods = [jnp.broadcast_to(m, (n, L, d)).reshape(1, n * L, d) for m in mods]
    sh1, sc1, g1, sh2, sc2, g2 = mods
    a_cols, b_z, b_xbc, c_q, c_k, c_v, small = in_projection(xg, p['norm1_w'][l], sc1, sh1, p['w_in_perm'][l])
    rs = lambda t: t.reshape(n, L, t.shape[-1])
    a_cols, b_z, b_xbc = rs(a_cols), rs(b_z), rs(b_xbc)
    b_dt, c_f = rs(small[:, :B_HEADS]), rs(small[:, B_HEADS:B_HEADS + C_HEADS])
    c_q, c_k, c_v = rs(c_q), rs(c_k), rs(c_v)
    y_a, s_new, shift_new = rwkv7_mixer(
        a_cols, shift0, s0, p['a_mu'][l], p['a_w0'][l], p['a_w2'][l], p['a_a0'][l], p['a_a2'][l],
        p['a_g2'][l], p['a_kk'][l], p['a_ka'][l], p['a_rk'][l], p['a_ln_w'][l], p['a_ln_b'][l])
    mamba = mamba2_mixer_pallas if L % SSD_CHUNK == 0 else mamba2_mixer
    y_b, ssm_new, conv_new = mamba(
        b_z, b_xbc, rs(small) if L % SSD_CHUNK == 0 else b_dt, conv0, ssm0, p['b_conv_w'][l], p['b_conv_b'][l],
        p['b_dt_bias'][l], p['b_a_log'][l], p['b_d'][l], p['b_norm_w'][l])
    y_c, k_rows, v_rows, logf_rows = fox_mixer(
        c_q, c_k, c_v, c_f, p['c_f_bias'][l], p['c_qnorm_w'][l], p['c_knorm_w'][l], past, l)
    fl = lambda t: t.reshape(n * L, t.shape[-1])
    x1, u2, logits = out_projection(fl(y_a), fl(y_b), fl(y_c), p['w_out_bf16'][l], xg, g1, p['norm2_w'][l],
                                    sc2, sh2, p['router_w_pad'][l // 2], p['router_b_pad'][l // 2])
    dt = x.dtype
    states = (s_new.astype(dt), shift_new.astype(dt), ssm_new.astype(dt), conv_new.astype(dt),
              k_rows.astype(dt), v_rows.astype(dt), logf_rows.astype(dt))
    return x1.reshape(xg.shape), u2, logits, g2, states


def channel_mix(l, p, x1s, u2s, logits, g2s):
    u2 = jnp.concatenate(u2s, axis=0)
    j = l // 2
    if l % 2 == 0:
        f = dense_swiglu(u2, p['ffn_w_gate'][j], p['ffn_w_up'][j], p['ffn_w_down'][j], FFN_TILE_DENSE)
    else:
        lg = jnp.concatenate(logits, axis=0)[:, :N_EXPERTS]
        f = moe_swiglu(u2, lg, p['moe_w_gate'][j], p['moe_w_up'][j], p['moe_w_down'][j], FFN_TILE_EXPERT)
    outs, row0 = [], 0
    for x1, g2 in zip(x1s, g2s):
        outs.append(gated_residual(x1, g2, f, row0))
        row0 += x1.shape[0] * x1.shape[1]
    return outs


def run_trunk(xs, cs, p, init_states, pasts):
    outs = [[] for _ in xs]
    shapes = [x.shape for x in xs]
    for l in range(DEPTH):
        halves = [mix_group(x.reshape(s), c, l, p, *st[l], past)
                  for x, s, c, st, past in zip(xs, shapes, cs, init_states, pasts)]
        for o, h in zip(outs, halves):
            o.append(h[4])
        xs = channel_mix(l, p, [h[0] for h in halves], [h[1] for h in halves], [h[2] for h in halves],
                         [h[3] for h in halves])
    stacked = [[jnp.stack([o[i] for o in og]) for i in range(7)] for og in outs]
    return [x.reshape(s) for x, s in zip(xs, shapes)], stacked


def kernel(x_prompt, x_sample, cache_k, cache_v, cache_logf, state_rwkv, state_shift, state_ssm,
           state_conv, page_table, c_prompt, c_sample, norm1_w, norm2_w, w_ada, b_ada, w_in, w_out,
           a_mu, a_w0, a_w2, a_a0, a_a2, a_g2, a_kk, a_ka, a_rk, a_ln_w, a_ln_b,
           b_conv_w, b_conv_b, b_dt_bias, b_a_log, b_d, b_norm_w,
           c_f_bias, c_qnorm_w, c_knorm_w, ffn_w_gate, ffn_w_up, ffn_w_down,
           moe_router_w, moe_router_b, moe_w_gate, moe_w_up, moe_w_down):
    p = dict(norm1_w=norm1_w, norm2_w=norm2_w, w_ada=w_ada, b_ada=b_ada, w_in=w_in, w_out=w_out,
             a_mu=a_mu, a_w0=a_w0, a_w2=a_w2, a_a0=a_a0, a_a2=a_a2, a_g2=a_g2, a_kk=a_kk, a_ka=a_ka,
             a_rk=a_rk, a_ln_w=a_ln_w, a_ln_b=a_ln_b, b_conv_w=b_conv_w, b_conv_b=b_conv_b,
             b_dt_bias=b_dt_bias, b_a_log=b_a_log, b_d=b_d, b_norm_w=b_norm_w, c_f_bias=c_f_bias,
             c_qnorm_w=c_qnorm_w, c_knorm_w=c_knorm_w, ffn_w_gate=ffn_w_gate, ffn_w_up=ffn_w_up,
             ffn_w_down=ffn_w_down, moe_router_w=moe_router_w, moe_router_b=moe_router_b,
             moe_w_gate=moe_w_gate, moe_w_up=moe_w_up, moe_w_down=moe_w_down)
    o_dt, o_q, o_f = A_IN + B_WIDTH + B_CONV_DIM, A_IN + B_IN, A_IN + B_IN + 3 * C_WIDTH
    p['w_in_perm'] = jnp.concatenate(
        [w_in[:, :, :o_dt], w_in[:, :, o_q:o_f], w_in[:, :, o_dt:o_q], w_in[:, :, o_f:],
         jnp.zeros((DEPTH, D_MODEL, LANE - B_HEADS - C_HEADS), w_in.dtype)], axis=-1).astype(BF16)
    p['w_out_bf16'] = w_out.astype(BF16)
    p['router_w_pad'] = jnp.pad(moe_router_w, ((0, 0), (0, 0), (0, LANE - N_EXPERTS))).astype(BF16)
    p['router_b_pad'] = jnp.pad(moe_router_b.astype(F32), ((0, 0), (0, LANE - N_EXPERTS)))[:, None, :]
    for name in ('ffn_w_gate', 'ffn_w_up', 'ffn_w_down'):
        p[name] = p[name].astype(BF16)
    for name in ('moe_w_gate', 'moe_w_up', 'moe_w_down'):
        p[name] = jnp.stack([cast_bf16(w) for w in p[name]]) if N_MOE > 1 else cast_bf16(p[name][0])[None]
    n_p = x_prompt.shape[0]
    zero_state = (jnp.zeros((n_p, A_HEADS, HEAD_DIM, HEAD_DIM), jnp.float32),
                  jnp.zeros((n_p, A_IN), x_prompt.dtype),
                  jnp.zeros((n_p, B_HEADS, HEAD_DIM, D_STATE), jnp.float32),
                  jnp.zeros((n_p, CONV_W - 1, B_CONV_DIM), x_prompt.dtype))
    init_s = [(state_rwkv[l], state_shift[l], state_ssm[l], state_conv[l]) for l in range(DEPTH)]
    (y_prompt, y_sample), (st_p, st_s) = run_trunk(
        [x_prompt, x_sample], [c_prompt, c_sample], p, [[zero_state] * DEPTH, init_s],
        [None, (cache_k, cache_v, cache_logf, page_table)])
    rwkv_prompt, shift_prompt, ssm_prompt, conv_prompt, k_prompt, v_prompt, logf_prompt = st_p
    rwkv_sample, shift_sample, ssm_sample, conv_sample, k_sample, v_sample, logf_sample = st_s
    return (y_prompt, y_sample,
            k_prompt, v_prompt, logf_prompt, rwkv_prompt, shift_prompt, ssm_prompt, conv_prompt,
            k_sample, v_sample, logf_sample, rwkv_sample, shift_sample, ssm_sample, conv_sample)
```
